```python
import math
import jax, jax.numpy as jnp
from jax import lax
import numpy as np

D_MODEL = 1024
BATCH = 4
SEQ = 4096
DEPTH = 1
DEC_BATCH = 128
DEC_SEQ = 8
PAST_LEN = 8192
PAGE_SIZE = 128

HEAD_DIM = 64
N_GROUPS = 3
WINDOWS = (128, 512, 2048)
DILATIONS = (1, 4, 16)
N_BACK = 128
HEADS_PER_GROUP = D_MODEL // 128
A_HEADS = N_GROUPS * HEADS_PER_GROUP
A_QKV = A_HEADS * HEAD_DIM
A_OUT = HEADS_PER_GROUP * HEAD_DIM
ATTN_BLOCK = 128
ATTN_SCALE = HEAD_DIM ** -0.5
B_HEADS = D_MODEL // 256
B_F_HEAD = 128
B_I_HEAD = D_MODEL // (2 * B_HEADS)
B_F = B_HEADS * B_F_HEAD
B_I = B_HEADS * B_I_HEAD
B_OUT = B_I
CHUNK = 64
IN_COLS = 3 * A_QKV + A_OUT + 2 * B_F + B_I + B_OUT + 2 * D_MODEL
EPS = 1e-6

kernel_name = 'hybrid_dilated_attn_hgrn2_decode_step'


def rmsnorm(x, gain):
    xf = x.astype(jnp.float32)
    y = xf * lax.rsqrt(jnp.mean(xf * xf, axis=-1, keepdims=True) + EPS) * gain.astype(jnp.float32)
    return y.astype(x.dtype)


def alibi_slopes():
    s = 2.0 ** (-8.0 * (np.arange(A_HEADS) + 1) / A_HEADS)
    return jnp.asarray(s.astype(np.float32).reshape(N_GROUPS, HEADS_PER_GROUP))


def dilated_group_prompt(q, k, v, dil, slopes):
    n, t, h, hd = q.shape
    L = t // dil
    nb = -(-L // ATTN_BLOCK)
    lp = nb * ATTN_BLOCK

    def split(a):
        a = a.reshape(n, L, dil, h, hd).transpose(0, 2, 1, 3, 4)
        a = jnp.pad(a, ((0, 0), (0, 0), (0, lp - L), (0, 0), (0, 0)))
        return a.reshape(n, dil, nb, ATTN_BLOCK, h, hd)

    def with_prev(a):
        prev = jnp.pad(a[:, :, :-1], ((0, 0), (0, 0), (1, 0), (0, 0), (0, 0), (0, 0)))
        return jnp.concatenate([prev, a], axis=3)

    qb = split(q)
    kk = with_prev(split(k))
    vv = with_prev(split(v))
    s = jnp.einsum('brnqhd,brnkhd->brnhqk', qb, kk).astype(jnp.float32) * ATTN_SCALE
    delta = np.arange(ATTN_BLOCK)[:, None] + ATTN_BLOCK - np.arange(2 * ATTN_BLOCK)[None, :]
    key_pos = (np.arange(nb)[:, None, None] * ATTN_BLOCK
               + np.arange(2 * ATTN_BLOCK)[None, None, :] - ATTN_BLOCK)
    valid = (delta >= 0) & (delta <= N_BACK) & (key_pos >= 0)
    dist = jnp.asarray((delta * dil).astype(np.float32))
    s = s - slopes[:, None, None] * dist
    s = jnp.where(valid[:, None], s, -jnp.inf)
    m = jnp.max(s, axis=-1, keepdims=True)
    p = jnp.exp(s - m)
    l = jnp.sum(p, axis=-1, keepdims=True)
    o = jnp.einsum('brnhqk,brnkhd->brnqhd', p, vv.astype(jnp.float32)) / jnp.swapaxes(l, 3, 4)
    lse = jnp.swapaxes((m + jnp.log(l))[..., 0], 3, 4)
    o = o.reshape(n, dil, lp, h, hd)[:, :, :L].transpose(0, 2, 1, 3, 4).reshape(n, t, h, hd)
    lse = lse.reshape(n, dil, lp, h)[:, :, :L].transpose(0, 2, 1, 3).reshape(n, t, h)
    return o, lse


def dilated_group_sample(q, k, v, kv_buf, dil, slopes):
    n, t, h, hd = q.shape
    wb = kv_buf.shape[1]
    kcat = jnp.concatenate([kv_buf[:, :, 0], k], axis=1)
    vcat = jnp.concatenate([kv_buf[:, :, 1], v], axis=1)
    steps = np.arange(N_BACK + 1)
    idx = wb + np.arange(t)[:, None] - steps[None, :] * dil
    valid = idx >= 0
    idx = np.maximum(idx, 0)
    kg = kcat[:, idx]
    vg = vcat[:, idx]
    dist = jnp.asarray((steps * dil).astype(np.float32))
    s = jnp.einsum('nthd,ntkhd->nthk', q, kg).astype(jnp.float32) * ATTN_SCALE - slopes[:, None] * dist
    s = jnp.where(valid[:, None, :], s, -jnp.inf)
    m = jnp.max(s, axis=-1, keepdims=True)
    p = jnp.exp(s - m)
    l = jnp.sum(p, axis=-1, keepdims=True)
    o = jnp.einsum('nthk,ntkhd->nthd', p, vg.astype(jnp.float32)) / l
    lse = (m + jnp.log(l))[..., 0]
    return o, lse


def hgrn2_recurrence(q, k, v, log_f, s0):
    n, t, h, fd = q.shape
    iv = v.shape[-1]
    c = min(CHUNK, t)
    nc = -(-t // c)
    tp = nc * c

    def chunks(a):
        a = jnp.pad(a, ((0, 0), (0, tp - t), (0, 0), (0, 0)))
        return jnp.moveaxis(a.reshape(n, nc, c, h, a.shape[-1]), 1, 0)

    causal = np.tril(np.ones((c, c), dtype=bool))[None, :, :, None, None]

    def step(state, inp):
        qc, kc, vc, gc = inp
        b = jnp.cumsum(gc, axis=1)
        o_inter = jnp.einsum('nchf,nhfi->nchi', qc * jnp.exp(b), state)
        decay = jnp.exp(jnp.where(causal, b[:, :, None] - b[:, None, :], -jnp.inf))
        att = jnp.einsum('nthf,nshf,ntshf->nths', qc, kc, decay)
        o_intra = jnp.einsum('nths,nshi->nthi', att, vc)
        b_last = b[:, -1]
        state = (jnp.exp(b_last)[..., None] * state
                 + jnp.einsum('nshf,nshi->nhfi', kc * jnp.exp(b_last[:, None] - b), vc))
        return state, o_inter + o_intra

    s_fin, o = lax.scan(step, s0.astype(jnp.float32),
                        (chunks(q), chunks(k), chunks(v), chunks(log_f)))
    o = jnp.moveaxis(o, 0, 1).reshape(n, tp, h, iv)[:, :t]
    return o, s_fin


def mixer_layer(x, c, kv_bufs, hgrn_state, lb, norm_gain, w_ada, b_ada, w_in, q_gain, k_gain,
                o_gain, w_ba, w_bb, w_o):
    n, t, _ = x.shape
    shift, scale, gate = jnp.split(jax.nn.silu(c) @ w_ada + b_ada, 3, axis=-1)
    h = rmsnorm(x, norm_gain) * (1 + scale[:, None]) + shift[:, None]
    sizes = (A_QKV, A_QKV, A_QKV, A_OUT, B_F, B_F, B_I, B_OUT, D_MODEL, D_MODEL)
    cuts = np.cumsum(sizes)[:-1].tolist()
    qa, ka, va, za, qb, fb, ib, zb, ga, gb = jnp.split(h @ w_in, cuts, axis=-1)

    shp = (n, t, N_GROUPS, HEADS_PER_GROUP, HEAD_DIM)
    qa = rmsnorm(qa.reshape(shp), q_gain)
    ka = rmsnorm(ka.reshape(shp), k_gain)
    va = va.reshape(shp)
    slopes = alibi_slopes()
    outs, lses, new_kv = [], [], []
    for g in range(N_GROUPS):
        qg, kg, vg = qa[:, :, g], ka[:, :, g], va[:, :, g]
        kv_rows = jnp.stack([kg, vg], axis=2)
        if kv_bufs is None:
            o, lse = dilated_group_prompt(qg, kg, vg, DILATIONS[g], slopes[g])
            kv_rows = kv_rows[:, t - min(WINDOWS[g], t):]
        else:
            o, lse = dilated_group_sample(qg, kg, vg, kv_bufs[g], DILATIONS[g], slopes[g])
        outs.append(o)
        lses.append(lse)
        new_kv.append(kv_rows)
    alpha = jax.nn.softmax(jnp.stack(lses), axis=0)
    o_a = jnp.einsum('gnth,gnthd->nthd', alpha, jnp.stack(outs)).reshape(n, t, A_OUT).astype(x.dtype)
    branch_a = (o_a * jax.nn.silu(za)) @ w_ba

    lbh = lb.reshape(B_HEADS, B_F_HEAD)
    f = lbh + (1.0 - lbh) * jax.nn.sigmoid(fb.astype(jnp.float32).reshape(n, t, B_HEADS, B_F_HEAD))
    q_h = jax.nn.silu(qb.astype(jnp.float32)).reshape(n, t, B_HEADS, B_F_HEAD)
    i_h = ib.astype(jnp.float32).reshape(n, t, B_HEADS, B_I_HEAD)
    o_b, s_new = hgrn2_recurrence(q_h, 1.0 - f, i_h, jnp.log(f), hgrn_state)
    o_b = rmsnorm(o_b, o_gain).reshape(n, t, B_OUT).astype(x.dtype)
    branch_b = (o_b * jax.nn.silu(zb)) @ w_bb

    merged = jax.nn.sigmoid(ga) * branch_a + jax.nn.sigmoid(gb) * branch_b
    y = x + gate[:, None] * (merged @ w_o)
    return y, new_kv, s_new.astype(x.dtype)


def setup_inputs(seed: int = 0) -> dict:
    key = jax.random.key(seed)
    ks = jax.random.split(key, 20)

    def nrm(k, shape, s):
        return jax.random.normal(k, shape, jnp.float32) * s

    kv_shape = lambda g: (DEPTH, DEC_BATCH, min(WINDOWS[g], PAST_LEN), 2, HEADS_PER_GROUP, HEAD_DIM)
    return {
        'x_prompt': nrm(ks[0], (BATCH, SEQ, D_MODEL), 1.0),
        'x_sample': nrm(ks[1], (DEC_BATCH, DEC_SEQ, D_MODEL), 1.0),
        'cache_kv_g0': nrm(ks[2], kv_shape(0), 1.0),
        'cache_kv_g1': nrm(ks[3], kv_shape(1), 1.0),
        'cache_kv_g2': nrm(ks[4], kv_shape(2), 1.0),
        'state_hgrn': nrm(ks[5], (DEPTH, DEC_BATCH, B_HEADS, B_F_HEAD, B_I_HEAD), 0.3),
        'c_prompt': nrm(ks[6], (BATCH, D_MODEL), 1.0),
        'c_sample': nrm(ks[7], (DEC_BATCH, D_MODEL), 1.0),
        'norm_gain': 1.0 + nrm(ks[8], (DEPTH, D_MODEL), 0.1),
        'w_ada': nrm(ks[9], (DEPTH, D_MODEL, 3 * D_MODEL), 0.5 * D_MODEL ** -0.5),
        'b_ada': nrm(ks[10], (DEPTH, 3 * D_MODEL), 0.01),
        'w_in': nrm(ks[11], (DEPTH, D_MODEL, IN_COLS), D_MODEL ** -0.5),
        'q_norm_gain': 1.0 + nrm(ks[12], (DEPTH, HEAD_DIM), 0.1),
        'k_norm_gain': 1.0 + nrm(ks[13], (DEPTH, HEAD_DIM), 0.1),
        'hgrn_lb_logits': nrm(ks[14], (DEPTH + 1, B_F), 0.5),
        'hgrn_out_norm_gain': 1.0 + nrm(ks[15], (DEPTH, B_I_HEAD), 0.1),
        'w_branch_a': nrm(ks[16], (DEPTH, A_OUT, D_MODEL), A_OUT ** -0.5),
        'w_branch_b': nrm(ks[17], (DEPTH, B_OUT, D_MODEL), B_OUT ** -0.5),
        'w_out': nrm(ks[18], (DEPTH, D_MODEL, D_MODEL), D_MODEL ** -0.5),
    }


def reference(x_prompt, x_sample, cache_kv_g0, cache_kv_g1, cache_kv_g2, state_hgrn, c_prompt, c_sample,
              norm_gain, w_ada, b_ada, w_in, q_norm_gain, k_norm_gain, hgrn_lb_logits, hgrn_out_norm_gain,
              w_branch_a, w_branch_b, w_out):
    lbs = jnp.cumsum(jax.nn.softmax(hgrn_lb_logits.astype(jnp.float32), axis=0), axis=0)
    xp, xs = x_prompt, x_sample
    kvp = ([], [], [])
    kvs = ([], [], [])
    sp, ss = [], []
    for layer in range(DEPTH):
        w = (norm_gain[layer], w_ada[layer], b_ada[layer], w_in[layer], q_norm_gain[layer],
             k_norm_gain[layer], hgrn_out_norm_gain[layer], w_branch_a[layer], w_branch_b[layer], w_out[layer])
        s0 = jnp.zeros((xp.shape[0], B_HEADS, B_F_HEAD, B_I_HEAD), jnp.float32)
        xp, kv_new, s_new = mixer_layer(xp, c_prompt, None, s0, lbs[layer], *w)
        for g in range(N_GROUPS):
            kvp[g].append(kv_new[g])
        sp.append(s_new)
        bufs = (cache_kv_g0[layer], cache_kv_g1[layer], cache_kv_g2[layer])
        xs, kv_new, s_new = mixer_layer(xs, c_sample, bufs, state_hgrn[layer], lbs[layer], *w)
        for g in range(N_GROUPS):
            kvs[g].append(kv_new[g])
        ss.append(s_new)
    return (xp, xs,
            jnp.stack(kvp[0]), jnp.stack(kvp[1]), jnp.stack(kvp[2]), jnp.stack(sp),
            jnp.stack(kvs[0]), jnp.stack(kvs[1]), jnp.stack(kvs[2]), jnp.stack(ss))
```

```python
import functools

import numpy as np
import jax
import jax.numpy as jnp
from jax import lax
from jax.experimental import pallas as pl
from jax.experimental.pallas import tpu as pltpu

F32 = jnp.float32
BF16 = jnp.bfloat16

HEAD_DIM = 64
N_GROUPS = 3
WINDOWS = (128, 512, 2048)
DILATIONS = (1, 4, 16)
N_BACK = 128
ATTN_BLOCK = 128
ATTN_SCALE = HEAD_DIM ** -0.5
HGRN_HEAD = 128
EPS = 1e-6
NEG = -1e30

V7X_VMEM_LIMIT = 56 * 1024 * 1024
PROJ_TOKENS = 512
HGRN_CHUNK = 64
HGRN_STEP_TOKENS = 512


def _sigmoid(x):
    return 1.0 / (1.0 + jnp.exp(-x))


def _silu(x):
    return x * _sigmoid(x)


def _dot(a, b):
    return jnp.dot(a, b, preferred_element_type=F32)


def _dot_nt(a, b):
    return lax.dot_general(a, b, (((1,), (1,)), ((), ())), preferred_element_type=F32)


def _dot_tn(a, b):
    return lax.dot_general(a, b, (((0,), (0,)), ((), ())), preferred_element_type=F32)


def _alibi_slopes(n_heads_per_group):
    a_heads = N_GROUPS * n_heads_per_group
    s = 2.0 ** (-8.0 * (np.arange(a_heads) + 1) / a_heads)
    return s.astype(np.float32).reshape(N_GROUPS, n_heads_per_group)


def _resident(shape):
    nd = len(shape)
    return pl.BlockSpec(shape, lambda *_: (0,) * nd, pipeline_mode=pl.Buffered(1))


def _ada_kernel(c_ref, w_ref, b_ref, o_ref):
    o_ref[...] = _dot(_silu(c_ref[...]).astype(BF16), w_ref[...]) + b_ref[...]


def _ada_call(c, w_ada, b_ada):
    n, d = c.shape
    cols = w_ada.shape[1]
    blk = d
    return pl.pallas_call(
        _ada_kernel,
        grid=(cols // blk,),
        in_specs=[pl.BlockSpec((n, d), lambda j: (0, 0)),
                  pl.BlockSpec((d, blk), lambda j: (0, j)),
                  pl.BlockSpec((1, blk), lambda j: (0, j))],
        out_specs=pl.BlockSpec((n, blk), lambda j: (0, j)),
        out_shape=jax.ShapeDtypeStruct((n, cols), F32),
        name="ada",
    )(c, w_ada.astype(BF16), b_ada.reshape(1, cols))


def _proj_kernel(x_ref, shift_ref, scale_ref, ng_ref, w_ref, bd_ref, qg_ref, kg_ref, lbl_ref,
                 q_ref, k_ref, v_ref, sza_ref, sqb_ref, g_ref, kf_ref, ib_ref, szb_ref, sga_ref, sgb_ref,
                 *, a_qkv, a_out, b_f, d_model):
    x = x_ref[...]
    ms = jnp.mean(x * x, axis=-1, keepdims=True)
    h = x * lax.rsqrt(ms + EPS) * ng_ref[...]
    h = h * (1.0 + scale_ref[...]) + shift_ref[...]
    tm = h.shape[0] * h.shape[1]
    hb = h.reshape(tm, d_model).astype(BF16)
    tile = a_out

    def proj(c0):
        return _dot(hb, w_ref[:, c0:c0 + tile])

    def head_norm(u, gain):
        ss = _dot((u * u).astype(BF16), bd_ref[...])
        return u * lax.rsqrt(ss * (1.0 / HEAD_DIM) + EPS) * gain

    c0 = 0
    for g in range(a_qkv // tile):
        q_ref[:, g * tile:(g + 1) * tile] = head_norm(proj(c0), qg_ref[...]).astype(q_ref.dtype)
        c0 += tile
    for g in range(a_qkv // tile):
        k_ref[:, g * tile:(g + 1) * tile] = head_norm(proj(c0), kg_ref[...]).astype(k_ref.dtype)
        c0 += tile
    for g in range(a_qkv // tile):
        v_ref[:, g * tile:(g + 1) * tile] = proj(c0).astype(v_ref.dtype)
        c0 += tile
    sza_ref[...] = _silu(proj(c0)).astype(BF16)
    c0 += tile
    sqb_ref[...] = _silu(proj(c0)).astype(BF16)
    c0 += tile
    lbl = lbl_ref[...]
    lbe = jnp.exp(lbl - jnp.max(lbl, axis=0, keepdims=True))
    lb = lbe[0:1] / jnp.sum(lbe, axis=0, keepdims=True)
    fr = proj(c0)
    g_ref[...] = jnp.log(lb + (1.0 - lb) * _sigmoid(fr))
    kf_ref[...] = ((1.0 - lb) * _sigmoid(-fr)).astype(BF16)
    c0 += tile
    ib_ref[...] = proj(c0).astype(BF16)
    c0 += tile
    szb_ref[...] = _silu(proj(c0)).astype(BF16)
    c0 += tile
    for j in range(d_model // tile):
        sga_ref[:, j * tile:(j + 1) * tile] = _sigmoid(proj(c0)).astype(BF16)
        c0 += tile
    for j in range(d_model // tile):
        sgb_ref[:, j * tile:(j + 1) * tile] = _sigmoid(proj(c0)).astype(BF16)
        c0 += tile


def _token_tiling(s, t):
    if t >= PROJ_TOKENS:
        assert t % PROJ_TOKENS == 0
        return 1, PROJ_TOKENS
    bs = min(s, PROJ_TOKENS // t)
    assert s % bs == 0 and t % 8 == 0
    return bs, t


def _proj_call(x, shift, scale, norm_gain, w_in_bf, bd, qg, kg, lb_logits, *, a_qkv, a_out, b_f, kv_dtype):
    s, t, d = x.shape
    bs, bt = _token_tiling(s, t)
    tm = bs * bt
    n_t = t // bt
    ntok = s * t
    grid = (s // bs, n_t)

    def tok_spec(cols):
        return pl.BlockSpec((tm, cols), lambda i, j: (i * n_t + j, 0))

    out_cols = [(a_qkv, BF16), (a_qkv, kv_dtype), (a_qkv, kv_dtype), (a_out, BF16), (b_f, BF16), (b_f, F32),
                (b_f, BF16), (b_f, BF16), (b_f, BF16), (d, BF16), (d, BF16)]
    kern = functools.partial(_proj_kernel, a_qkv=a_qkv, a_out=a_out, b_f=b_f, d_model=d)
    return pl.pallas_call(
        kern,
        grid=grid,
        in_specs=[pl.BlockSpec((bs, bt, d), lambda i, j: (i, j, 0)),
                  pl.BlockSpec((bs, 1, d), lambda i, j: (i, 0, 0)),
                  pl.BlockSpec((bs, 1, d), lambda i, j: (i, 0, 0)),
                  _resident((1, d)),
                  _resident(w_in_bf.shape),
                  _resident(bd.shape),
                  _resident(qg.shape),
                  _resident(kg.shape),
                  _resident(lb_logits.shape)],
        out_specs=[tok_spec(c) for c, _ in out_cols],
        out_shape=[jax.ShapeDtypeStruct((ntok, c), dt) for c, dt in out_cols],
        compiler_params=pltpu.CompilerParams(dimension_semantics=("arbitrary", "arbitrary"),
                                             vmem_limit_bytes=V7X_VMEM_LIMIT),
        name="proj",
    )(x, shift, scale, norm_gain.reshape(1, d), w_in_bf, bd, qg, kg, lb_logits)


def _attn_kernel(q_ref, kp_ref, kc_ref, vp_ref, vc_ref, bias_ref, pm_ref, o_ref, lse_ref, *, n_heads):
    first = (pl.program_id(2) == 0).astype(F32)
    pm = pm_ref[...] * first
    lane = lax.broadcasted_iota(jnp.int32, (ATTN_BLOCK, 2 * HEAD_DIM), 1)
    low = lane < HEAD_DIM
    for hp in range(n_heads // 2):
        sl = slice(hp * 2 * HEAD_DIM, (hp + 1) * 2 * HEAD_DIM)
        q2 = q_ref[0, :, sl]
        k2 = jnp.concatenate([kp_ref[0, :, sl], kc_ref[0, :, sl]], axis=0)
        v2 = jnp.concatenate([vp_ref[0, :, sl], vc_ref[0, :, sl]], axis=0)
        res = []
        for par in range(2):
            keep = low if par == 0 else jnp.logical_not(low)
            qm = jnp.where(keep, q2, jnp.zeros_like(q2))
            s = _dot_nt(qm, k2) + bias_ref[2 * hp + par] + pm
            m = jnp.max(s, axis=-1, keepdims=True)
            p = jnp.exp(s - m)
            l = jnp.sum(p, axis=-1, keepdims=True)
            o = _dot(p.astype(BF16), v2) / l
            res.append((o, m + jnp.log(l)))
        o_ref[0, :, sl] = jnp.where(low, res[0][0], res[1][0]).astype(o_ref.dtype)
        lse_ref[0, :, sl] = jnp.where(low, res[0][1], res[1][1])


def _attn_tables(dil, slopes_g):
    i = np.arange(ATTN_BLOCK)[:, None]
    j = np.arange(2 * ATTN_BLOCK)[None, :]
    delta = i + ATTN_BLOCK - j
    valid = (delta >= 0) & (delta <= N_BACK)
    dist = (delta * dil).astype(np.float32)
    bias = np.where(valid[None], -slopes_g[:, None, None] * dist[None], NEG).astype(np.float32)
    pm = np.where(j < ATTN_BLOCK, NEG, 0.0).astype(np.float32) * np.ones((ATTN_BLOCK, 1), np.float32)
    return jnp.asarray(bias), jnp.asarray(pm)


def _attn_call(q, k, v, g, n, t, n_heads, slopes_g):
    dil = DILATIONS[g]
    width = n_heads * HEAD_DIM
    sub = t // dil
    assert t % dil == 0 and sub % ATTN_BLOCK == 0
    nb = sub // ATTN_BLOCK
    cols = N_GROUPS * width

    def view(a):
        return a.reshape(n, sub, dil * cols)

    cur = pl.BlockSpec((1, ATTN_BLOCK, width), lambda b, r, i: (b, i, r * N_GROUPS + g))
    prev = pl.BlockSpec((1, ATTN_BLOCK, width), lambda b, r, i: (b, jnp.maximum(i - 1, 0), r * N_GROUPS + g))
    out = pl.BlockSpec((1, ATTN_BLOCK, width), lambda b, r, i: (b, i, r))
    bias, pm = _attn_tables(dil, slopes_g)
    o, lse = pl.pallas_call(
        functools.partial(_attn_kernel, n_heads=n_heads),
        grid=(n, dil, nb),
        in_specs=[cur, prev, cur, prev, cur, _resident(bias.shape), _resident(pm.shape)],
        out_specs=[out, out],
        out_shape=[jax.ShapeDtypeStruct((n, sub, dil * width), BF16),
                   jax.ShapeDtypeStruct((n, sub, dil * width), F32)],
        compiler_params=pltpu.CompilerParams(dimension_semantics=("arbitrary", "arbitrary", "arbitrary"),
                                             vmem_limit_bytes=V7X_VMEM_LIMIT),
        name=f"attn_g{g}",
    )(view(q), view(k), view(k), view(v), view(v), bias, pm)
    return o.reshape(n * t, width), lse.reshape(n * t, width)


def _sattn_kernel(q_ref, k_ref, v_ref, c0_ref, c1_ref, c2_ref, b0_ref, b1_ref, b2_ref, bn_ref, o_ref,
                  *, n_heads):
    width = n_heads * HEAD_DIM
    caches = (c0_ref, c1_ref, c2_ref)
    biases = (b0_ref, b1_ref, b2_ref)
    for h in range(n_heads):
        scores, values = [], []
        for g in range(N_GROUPS):
            hs = slice(g * width + h * HEAD_DIM, g * width + (h + 1) * HEAD_DIM)
            qh = q_ref[0, :, hs]
            scores.append(_dot(qh, caches[g][0, h].astype(BF16)) + biases[g][h])
            values.append(caches[g][1, h].astype(BF16))
            scores.append(_dot_nt(qh, k_ref[0, :, hs].astype(BF16)) + bn_ref[g, h])
            values.append(v_ref[0, :, hs].astype(BF16))
        m = functools.reduce(jnp.maximum, [jnp.max(s, axis=-1, keepdims=True) for s in scores])
        ps = [jnp.exp(s - m) for s in scores]
        l = functools.reduce(lambda a, b: a + b, [jnp.sum(p, axis=-1, keepdims=True) for p in ps])
        acc = jnp.zeros((qh.shape[0], HEAD_DIM), F32)
        for i, (p, v) in enumerate(zip(ps, values)):
            pb = p.astype(BF16)
            acc = acc + (_dot_nt(pb, v) if i % 2 == 0 else _dot(pb, v))
        o_ref[0, :, h * HEAD_DIM:(h + 1) * HEAD_DIM] = (acc / l).astype(o_ref.dtype)


def _sattn_tables(n_heads, t_new, slopes, wbs):
    t = np.arange(t_new)[:, None]
    biases, bn = [], []
    for g in range(N_GROUPS):
        d, wb = DILATIONS[g], wbs[g]
        dist = wb + t - np.arange(wb)[None, :]
        valid = (dist % d == 0) & (dist // d <= N_BACK)
        biases.append(np.where(valid[None], -slopes[g][:, None, None] * dist[None], NEG).astype(np.float32))
        dist = t - np.arange(t_new)[None, :]
        valid = (dist >= 0) & (dist % d == 0) & (dist // d <= N_BACK)
        bn.append(np.where(valid[None], -slopes[g][:, None, None] * dist[None], NEG).astype(np.float32))
    return [jnp.asarray(b) for b in biases], jnp.asarray(np.stack(bn))


def _sattn_call(q, k, v, caches, n, t_new, n_heads, slopes):
    width = n_heads * HEAD_DIM
    cols = N_GROUPS * width
    wbs = [c.shape[1] for c in caches]
    biases, bn = _sattn_tables(n_heads, t_new, slopes, wbs)
    views = [jnp.transpose(c, (0, 2, 3, 4, 1)) for c in caches]
    new = pl.BlockSpec((1, t_new, cols), lambda i: (i, 0, 0))
    o = pl.pallas_call(
        functools.partial(_sattn_kernel, n_heads=n_heads),
        grid=(n,),
        in_specs=[new, new, new]
        + [pl.BlockSpec((None, 2, n_heads, HEAD_DIM, wb), lambda i: (i, 0, 0, 0, 0)) for wb in wbs]
        + [_resident(b.shape) for b in biases] + [_resident(bn.shape)],
        out_specs=pl.BlockSpec((1, t_new, width), lambda i: (i, 0, 0)),
        out_shape=jax.ShapeDtypeStruct((n, t_new, width), BF16),
        compiler_params=pltpu.CompilerParams(dimension_semantics=("arbitrary",), vmem_limit_bytes=V7X_VMEM_LIMIT),
        name="sattn",
    )(q.reshape(n, t_new, cols), k.reshape(n, t_new, cols), v.reshape(n, t_new, cols), *views, *biases, bn)
    return o.reshape(n * t_new, width)


def _hgrn_tables(c):
    t = np.arange(c)
    u = t[None, :]
    sel = [u <= t[:, None], u > t[:, None]]
    masks = [np.eye(c, dtype=bool)]
    roles = []
    w = c // 2
    while w >= 1:
        blk = t // (2 * w)
        late = (t // w) % 2 == 1
        mid = blk * 2 * w + w - 1
        later_rows = (u > mid[:, None]) & (u <= t[:, None])
        early_rows = (u > t[:, None]) & (u <= mid[:, None])
        sel.append(np.where(late[:, None], later_rows, early_rows))
        masks.append((blk[:, None] == blk[None, :]) & late[:, None] & ~late[None, :])
        roles.append(late)
        w //= 2
    sel = np.concatenate(sel, axis=0).astype(np.float32)
    masks = np.stack(masks).astype(np.float32)
    roles = np.stack(roles).astype(np.float32)
    roles = np.repeat(roles.reshape(-1, 1), HGRN_HEAD, axis=1)
    return jnp.asarray(sel, BF16), jnp.asarray(masks), jnp.asarray(roles)


def _hgrn_chunk(sq, g, kf, iv, state, sel, masks, roles, og, c):
    n_lev = masks.shape[0] - 1
    g1 = g.astype(BF16)
    r1 = g - g1.astype(F32)
    g2 = r1.astype(BF16)
    g3 = (r1 - g2.astype(F32)).astype(BF16)
    sel_v = sel[...]
    e = jnp.exp(_dot(sel_v, g1) + _dot(sel_v, g2) + _dot(sel_v, g3))
    og = og[...]
    outs, new_state = [], []
    for h in range(len(state)):
        hs = slice(h * HGRN_HEAD, (h + 1) * HGRN_HEAD)
        sqh, kfh, ivh = sq[:, hs], kf[:, hs], iv[:, hs]
        sqf, kff = sqh.astype(F32), kfh.astype(F32)
        eb = e[0:c, hs]
        o = _dot((sqf * eb).astype(BF16), state[h].astype(BF16))
        att = masks[0] * _dot_nt(sqh, kfh)
        for lv in range(n_lev):
            late = roles[lv * c:(lv + 1) * c] > 0.5
            x = (jnp.where(late, sqf, kff) * e[(2 + lv) * c:(3 + lv) * c, hs]).astype(BF16)
            att = att + masks[1 + lv] * _dot_nt(x, x)
        o = o + _dot(att.astype(BF16), ivh)
        kt = (kff * e[c:2 * c, hs]).astype(BF16)
        decay = jnp.broadcast_to(eb[c - 1:c, :], (HGRN_HEAD, HGRN_HEAD)).T
        new_state.append(decay * state[h] + _dot_tn(kt, ivh))
        outs.append(o * lax.rsqrt(jnp.mean(o * o, axis=-1, keepdims=True) + EPS) * og)
    return outs, new_state


def _hgrn_prompt_kernel(sq_ref, g_ref, kf_ref, iv_ref, sel_ref, masks_ref, roles_ref, og_ref, o_ref, s_ref,
                        *, c, n_heads):
    @pl.when(pl.program_id(1) == 0)
    def _():
        s_ref[...] = jnp.zeros_like(s_ref)

    sel, masks, roles, og = sel_ref, masks_ref, roles_ref, og_ref

    def body(ci, carry):
        rows = pl.ds(pl.multiple_of(ci * c, c), c)
        state = [s_ref[0, h] for h in range(n_heads)]
        outs, new_state = _hgrn_chunk(sq_ref[0, rows, :], g_ref[0, rows, :], kf_ref[0, rows, :], iv_ref[0, rows, :],
                                      state, sel, masks, roles, og, c)
        for h in range(n_heads):
            s_ref[0, h] = new_state[h]
            o_ref[0, rows, h * HGRN_HEAD:(h + 1) * HGRN_HEAD] = outs[h].astype(o_ref.dtype)
        return carry

    lax.fori_loop(0, sq_ref.shape[1] // c, body, 0)


def _hgrn_prompt_call(sq, g, kf, iv, og, n, t, n_heads):
    width = n_heads * HGRN_HEAD
    c = min(HGRN_CHUNK, t)
    tt = min(HGRN_STEP_TOKENS, t)
    assert t % tt == 0 and tt % c == 0
    sel, masks, roles = _hgrn_tables(c)
    tok = pl.BlockSpec((1, tt, width), lambda b, i: (b, i, 0))
    o, s = pl.pallas_call(
        functools.partial(_hgrn_prompt_kernel, c=c, n_heads=n_heads),
        grid=(n, t // tt),
        in_specs=[tok, tok, tok, tok, _resident(sel.shape), _resident(masks.shape), _resident(roles.shape),
                  _resident((1, HGRN_HEAD))],
        out_specs=[tok, pl.BlockSpec((1, n_heads, HGRN_HEAD, HGRN_HEAD), lambda b, i: (b, 0, 0, 0))],
        out_shape=[jax.ShapeDtypeStruct((n, t, width), BF16),
                   jax.ShapeDtypeStruct((n, n_heads, HGRN_HEAD, HGRN_HEAD), F32)],
        compiler_params=pltpu.CompilerParams(dimension_semantics=("arbitrary", "arbitrary"),
                                             vmem_limit_bytes=V7X_VMEM_LIMIT),
        name="hgrn_prompt",
    )(sq.reshape(n, t, width), g.reshape(n, t, width), kf.reshape(n, t, width), iv.reshape(n, t, width),
      sel, masks, roles, og.reshape(1, HGRN_HEAD))
    return o.reshape(n * t, width), s


def _hgrn_sample_kernel(sq_ref, g_ref, kf_ref, iv_ref, s0_ref, sel_ref, masks_ref, roles_ref, og_ref, o_ref, s_ref,
                        *, c, n_heads):
    sel, masks, roles, og = sel_ref, masks_ref, roles_ref, og_ref
    for b in range(sq_ref.shape[0]):
        state = [s0_ref[b, h] for h in range(n_heads)]
        outs, new_state = _hgrn_chunk(sq_ref[b], g_ref[b], kf_ref[b], iv_ref[b], state, sel, masks, roles, og, c)
        for h in range(n_heads):
            s_ref[b, h] = new_state[h]
            o_ref[b, :, h * HGRN_HEAD:(h + 1) * HGRN_HEAD] = outs[h].astype(o_ref.dtype)


def _hgrn_sample_call(sq, g, kf, iv, s0, og, n, t, n_heads):
    width = n_heads * HGRN_HEAD
    bn = min(4, n)
    assert n % bn == 0
    sel, masks, roles = _hgrn_tables(t)
    tok = pl.BlockSpec((bn, t, width), lambda i: (i, 0, 0))
    st = pl.BlockSpec((bn, n_heads, HGRN_HEAD, HGRN_HEAD), lambda i: (i, 0, 0, 0))
    o, s = pl.pallas_call(
        functools.partial(_hgrn_sample_kernel, c=t, n_heads=n_heads),
        grid=(n // bn,),
        in_specs=[tok, tok, tok, tok, st, _resident(sel.shape), _resident(masks.shape), _resident(roles.shape),
                  _resident((1, HGRN_HEAD))],
        out_specs=[tok, st],
        out_shape=[jax.ShapeDtypeStruct((n, t, width), BF16),
                   jax.ShapeDtypeStruct((n, n_heads, HGRN_HEAD, HGRN_HEAD), F32)],
        compiler_params=pltpu.CompilerParams(dimension_semantics=("arbitrary",), vmem_limit_bytes=V7X_VMEM_LIMIT),
        name="hgrn_sample",
    )(sq.reshape(n, t, width), g.reshape(n, t, width), kf.reshape(n, t, width), iv.reshape(n, t, width),
      s0, sel, masks, roles, og.reshape(1, HGRN_HEAD))
    return o.reshape(n * t, width), s


def _out_kernel(*refs, n_att):
    att = refs[:2 * n_att] if n_att > 1 else refs[:1]
    n_in = len(att)
    sza_ref, ob_ref, szb_ref, sga_ref, sgb_ref, x_ref, gate_ref, wa_ref, wb_ref, wo_ref, y_ref = refs[n_in:]
    if n_att > 1:
        os_, ls_ = att[:n_att], [r[...] for r in att[n_att:]]
        m = functools.reduce(jnp.maximum, ls_)
        es = [jnp.exp(l - m) for l in ls_]
        den = functools.reduce(lambda a, b: a + b, es)
        num = functools.reduce(lambda a, b: a + b, [e * o[...].astype(F32) for e, o in zip(es, os_)])
        o_a = num / den
    else:
        o_a = att[0][...].astype(F32)
    branch_a = _dot((o_a * sza_ref[...].astype(F32)).astype(BF16), wa_ref[...])
    branch_b = _dot((ob_ref[...].astype(F32) * szb_ref[...].astype(F32)).astype(BF16), wb_ref[...])
    merged = sga_ref[...].astype(F32) * branch_a + sgb_ref[...].astype(F32) * branch_b
    upd = _dot(merged.astype(BF16), wo_ref[...])
    x = x_ref[...]
    y_ref[...] = x + gate_ref[...] * upd.reshape(x.shape)


def _out_call(att_inputs, sza, ob, szb, sga, sgb, x, gate, wa_bf, wb_bf, wo_bf):
    s, t, d = x.shape
    bs, bt = _token_tiling(s, t)
    tm = bs * bt
    n_t = t // bt

    def tok_spec(a):
        return pl.BlockSpec((tm, a.shape[1]), lambda i, j: (i * n_t + j, 0))

    n_att = (len(att_inputs) + 1) // 2
    toks = list(att_inputs) + [sza, ob, szb, sga, sgb]
    return pl.pallas_call(
        functools.partial(_out_kernel, n_att=n_att),
        grid=(s // bs, n_t),
        in_specs=[tok_spec(a) for a in toks]
        + [pl.BlockSpec((bs, bt, d), lambda i, j: (i, j, 0)),
           pl.BlockSpec((bs, 1, d), lambda i, j: (i, 0, 0)),
           _resident(wa_bf.shape), _resident(wb_bf.shape), _resident(wo_bf.shape)],
        out_specs=pl.BlockSpec((bs, bt, d), lambda i, j: (i, j, 0)),
        out_shape=jax.ShapeDtypeStruct((s, t, d), F32),
        compiler_params=pltpu.CompilerParams(dimension_semantics=("arbitrary", "arbitrary"),
                                             vmem_limit_bytes=V7X_VMEM_LIMIT),
        name="out",
    )(*toks, x, gate, wa_bf, wb_bf, wo_bf)


def kernel(x_prompt, x_sample, cache_kv_g0, cache_kv_g1, cache_kv_g2, state_hgrn, c_prompt, c_sample, norm_gain, w_ada, b_ada, w_in, q_norm_gain, k_norm_gain, hgrn_lb_logits, hgrn_out_norm_gain, w_branch_a, w_branch_b, w_out):
    depth = w_in.shape[0]
    assert depth == 1, "single-layer step"
    n_p, t_p, d = x_prompt.shape
    n_s, t_s, _ = x_sample.shape
    a_out = w_branch_a.shape[1]
    n_heads = a_out // HEAD_DIM
    a_qkv = N_GROUPS * a_out
    b_f = w_branch_b.shape[1]
    b_heads = b_f // HGRN_HEAD
    slopes = _alibi_slopes(n_heads)

    n_c = n_p + n_s
    pad = (-n_c) % 8
    c_all = jnp.concatenate([c_prompt, c_sample, jnp.zeros((pad, d), F32)], axis=0)
    ada = _ada_call(c_all, w_ada[0], b_ada[0])
    shift, scale, gate = ada[:, :d], ada[:, d:2 * d], ada[:, 2 * d:]

    w_in_bf = w_in[0].astype(BF16)
    wa_bf, wb_bf, wo_bf = w_branch_a[0].astype(BF16), w_branch_b[0].astype(BF16), w_out[0].astype(BF16)
    head_of = np.arange(a_out) // HEAD_DIM
    bd = jnp.asarray(head_of[:, None] == head_of[None, :], BF16)
    qg = (jnp.tile(q_norm_gain[0], n_heads) * ATTN_SCALE).reshape(1, a_out)
    kg = jnp.tile(k_norm_gain[0], n_heads).reshape(1, a_out)
    og = hgrn_out_norm_gain[0]

    def cond(lo, n):
        return [a[lo:lo + n].reshape(n, 1, d) for a in (shift, scale, gate)]

    sh, sc, gt = cond(0, n_p)
    q, k, v, sza, sqb, g, kf, ib, szb, sga, sgb = _proj_call(
        x_prompt, sh, sc, norm_gain[0], w_in_bf, bd, qg, kg, hgrn_lb_logits, a_qkv=a_qkv, a_out=a_out, b_f=b_f,
        kv_dtype=BF16)
    att = [_attn_call(q, k, v, gi, n_p, t_p, n_heads, slopes[gi]) for gi in range(N_GROUPS)]
    ob, hgrn_p = _hgrn_prompt_call(sqb, g, kf, ib, og, n_p, t_p, b_heads)
    y_p = _out_call([a[0] for a in att] + [a[1] for a in att], sza, ob, szb, sga, sgb, x_prompt, gt,
                    wa_bf, wb_bf, wo_bf)
    k3, v3 = k.reshape(n_p, t_p, a_qkv), v.reshape(n_p, t_p, a_qkv)
    kv_p = []
    for gi in range(N_GROUPS):
        w = min(WINDOWS[gi], t_p)
        rows = jnp.stack([k3[:, t_p - w:, gi * a_out:(gi + 1) * a_out], v3[:, t_p - w:, gi * a_out:(gi + 1) * a_out]],
                         axis=2)
        kv_p.append(rows.astype(F32).reshape(1, n_p, w, 2, n_heads, HEAD_DIM))

    sh, sc, gt = cond(n_p, n_s)
    q, k, v, sza, sqb, g, kf, ib, szb, sga, sgb = _proj_call(
        x_sample, sh, sc, norm_gain[0], w_in_bf, bd, qg, kg, hgrn_lb_logits, a_qkv=a_qkv, a_out=a_out, b_f=b_f,
        kv_dtype=F32)
    o_a = _sattn_call(q, k, v, (cache_kv_g0[0], cache_kv_g1[0], cache_kv_g2[0]), n_s, t_s, n_heads, slopes)
    ob, hgrn_s = _hgrn_sample_call(sqb, g, kf, ib, state_hgrn[0], og, n_s, t_s, b_heads)
    y_s = _out_call([o_a], sza, ob, szb, sga, sgb, x_sample, gt, wa_bf, wb_bf, wo_bf)
    k3, v3 = k.reshape(n_s, t_s, a_qkv), v.reshape(n_s, t_s, a_qkv)
    kv_s = []
    for gi in range(N_GROUPS):
        rows = jnp.stack([k3[:, :, gi * a_out:(gi + 1) * a_out], v3[:, :, gi * a_out:(gi + 1) * a_out]], axis=2)
        kv_s.append(rows.reshape(1, n_s, t_s, 2, n_heads, HEAD_DIM))

    return (y_p, y_s, kv_p[0], kv_p[1], kv_p[2], hgrn_p[None], kv_s[0], kv_s[1], kv_s[2], hgrn_s[None])
```

```python
import functools

import numpy as np
import jax
import jax.numpy as jnp
from jax import lax
from jax.experimental import pallas as pl
from jax.experimental.pallas import tpu as pltpu

F32 = jnp.float32
BF16 = jnp.bfloat16

HEAD_DIM = 64
N_GROUPS = 3
WINDOWS = (128, 512, 2048)
DILATIONS = (1, 4, 16)
N_BACK = 128
ATTN_BLOCK = 128
ATTN_SCALE = HEAD_DIM ** -0.5
HGRN_HEAD = 128
LANES = 128
EPS = 1e-6
NEG = -1e30

V7X_VMEM_LIMIT = 56 * 1024 * 1024
PROJ_TOKENS = 512
HGRN_CHUNK = 64
HGRN_STEP_TOKENS = 512
HGRN_MAX_SEQS = 4


def _sigmoid(x):
    return 1.0 / (1.0 + jnp.exp(-x))


def _silu(x):
    return x * _sigmoid(x)


def _dot(a, b):
    return jnp.dot(a, b, preferred_element_type=F32)


def _dot_nt(a, b):
    return lax.dot_general(a, b, (((1,), (1,)), ((), ())), preferred_element_type=F32)


def _dot_tn(a, b):
    return lax.dot_general(a, b, (((0,), (0,)), ((), ())), preferred_element_type=F32)


def _alibi_slopes(n_heads_per_group):
    a_heads = N_GROUPS * n_heads_per_group
    s = 2.0 ** (-8.0 * (np.arange(a_heads) + 1) / a_heads)
    return s.astype(np.float32).reshape(N_GROUPS, n_heads_per_group)


def _resident(shape):
    nd = len(shape)
    return pl.BlockSpec(shape, lambda *_: (0,) * nd, pipeline_mode=pl.Buffered(1))


def _ada_kernel(c_ref, w_ref, b_ref, o_ref):
    o_ref[...] = _dot(_silu(c_ref[...]).astype(BF16), w_ref[...]) + b_ref[...]


def _ada_call(c, w_ada, b_ada):
    n, d = c.shape
    cols = w_ada.shape[1]
    blk = d
    return pl.pallas_call(
        _ada_kernel,
        grid=(cols // blk,),
        in_specs=[pl.BlockSpec((n, d), lambda j: (0, 0)),
                  pl.BlockSpec((d, blk), lambda j: (0, j)),
                  pl.BlockSpec((1, blk), lambda j: (0, j))],
        out_specs=pl.BlockSpec((n, blk), lambda j: (0, j)),
        out_shape=jax.ShapeDtypeStruct((n, cols), F32),
        name="ada",
    )(c, w_ada.astype(BF16), b_ada.reshape(1, cols))


def _store_dilated(out_ref, val, scr, dil):
    bs, d, rows, width = out_ref.shape
    if dil == 1:
        out_ref[...] = val.reshape(bs, 1, rows, width).astype(out_ref.dtype)
        return
    for c in range(width // LANES):
        scr[c] = val[:, c * LANES:(c + 1) * LANES]
    for r in range(dil):
        for c in range(width // LANES):
            out_ref[0, r, :, c * LANES:(c + 1) * LANES] = scr[c, pl.ds(r, rows, stride=dil), :].astype(out_ref.dtype)


def _proj_kernel(x_ref, shift_ref, scale_ref, ng_ref, w_ref, bd_ref, qg_ref, kg_ref, lbl_ref, *refs,
                 n_groups, dils, a_out, d_model):
    qkv_refs = refs[:3 * n_groups]
    sza_ref, sqb_ref, g_ref, kf_ref, ib_ref, szb_ref, sga_ref, sgb_ref, scr = refs[3 * n_groups:]
    x = x_ref[...]
    ms = jnp.mean(x * x, axis=-1, keepdims=True)
    h = x * lax.rsqrt(ms + EPS) * ng_ref[...]
    h = h * (1.0 + scale_ref[...]) + shift_ref[...]
    tm = h.shape[0] * h.shape[1]
    hb = h.reshape(tm, d_model).astype(BF16)
    tile = a_out

    def proj(c0):
        return _dot(hb, w_ref[:, c0:c0 + tile])

    def head_norm(u, gain):
        ss = _dot((u * u).astype(BF16), bd_ref[...])
        return u * lax.rsqrt(ss * (1.0 / HEAD_DIM) + EPS) * gain

    c0 = 0
    for kind in range(3):
        for g in range(n_groups):
            u = proj(c0)
            if kind == 0:
                u = head_norm(u, qg_ref[...])
            elif kind == 1:
                u = head_norm(u, kg_ref[...])
            _store_dilated(qkv_refs[kind * n_groups + g], u, scr, dils[g])
            c0 += tile
    sza_ref[...] = _silu(proj(c0)).astype(BF16)
    c0 += tile
    sqb_ref[...] = _silu(proj(c0)).astype(BF16)
    c0 += tile
    lbl = lbl_ref[...]
    lbe = jnp.exp(lbl - jnp.max(lbl, axis=0, keepdims=True))
    lb = lbe[0:1] / jnp.sum(lbe, axis=0, keepdims=True)
    fr = proj(c0)
    g_ref[...] = jnp.log(lb + (1.0 - lb) * _sigmoid(fr))
    kf_ref[...] = ((1.0 - lb) * _sigmoid(-fr)).astype(BF16)
    c0 += tile
    ib_ref[...] = proj(c0).astype(BF16)
    c0 += tile
    szb_ref[...] = _silu(proj(c0)).astype(BF16)
    c0 += tile
    for j in range(d_model // tile):
        sga_ref[:, j * tile:(j + 1) * tile] = _sigmoid(proj(c0)).astype(BF16)
        c0 += tile
    for j in range(d_model // tile):
        sgb_ref[:, j * tile:(j + 1) * tile] = _sigmoid(proj(c0)).astype(BF16)
        c0 += tile


def _token_tiling(s, t):
    if t >= PROJ_TOKENS:
        assert t % PROJ_TOKENS == 0
        return 1, PROJ_TOKENS
    bs = min(s, PROJ_TOKENS // t)
    assert s % bs == 0 and t % 8 == 0
    return bs, t


def _proj_call(x, shift, scale, norm_gain, w_in_bf, bd, qg, kg, lb_logits, *, n_groups, dils, a_out, b_f, kv_dtype):
    s, t, d = x.shape
    bs, bt = _token_tiling(s, t)
    tm = bs * bt
    n_t = t // bt
    ntok = s * t
    grid = (s // bs, n_t)
    assert all(dl == 1 or (bs == 1 and bt % (dl * 16) == 0) for dl in dils)

    def tok_spec(cols):
        return pl.BlockSpec((tm, cols), lambda i, j: (i * n_t + j, 0))

    qkv_dtypes = [BF16] * n_groups + [kv_dtype] * (2 * n_groups)
    qkv_shapes = [jax.ShapeDtypeStruct((s, dils[g], t // dils[g], a_out), dt)
                  for g, dt in zip(list(range(n_groups)) * 3, qkv_dtypes)]
    qkv_specs = [pl.BlockSpec((bs, dils[g], bt // dils[g], a_out), lambda i, j: (i, 0, j, 0))
                 for g in list(range(n_groups)) * 3]
    out_cols = [(a_out, BF16), (b_f, BF16), (b_f, F32), (b_f, BF16), (b_f, BF16), (b_f, BF16), (d, BF16), (d, BF16)]
    kern = functools.partial(_proj_kernel, n_groups=n_groups, dils=tuple(dils), a_out=a_out, d_model=d)
    return pl.pallas_call(
        kern,
        grid=grid,
        in_specs=[pl.BlockSpec((bs, bt, d), lambda i, j: (i, j, 0)),
                  pl.BlockSpec((bs, 1, d), lambda i, j: (i, 0, 0)),
                  pl.BlockSpec((bs, 1, d), lambda i, j: (i, 0, 0)),
                  _resident((1, d)),
                  _resident(w_in_bf.shape),
                  _resident(bd.shape),
                  _resident(qg.shape),
                  _resident(kg.shape),
                  _resident(lb_logits.shape)],
        out_specs=qkv_specs + [tok_spec(c) for c, _ in out_cols],
        out_shape=qkv_shapes + [jax.ShapeDtypeStruct((ntok, c), dt) for c, dt in out_cols],
        scratch_shapes=[pltpu.VMEM((a_out // LANES, tm, LANES), F32)],
        compiler_params=pltpu.CompilerParams(dimension_semantics=("arbitrary", "arbitrary"),
                                             vmem_limit_bytes=V7X_VMEM_LIMIT),
        name="proj",
    )(x, shift, scale, norm_gain.reshape(1, d), w_in_bf, bd, qg, kg, lb_logits)


def _attn_kernel(q_ref, kp_ref, kc_ref, vp_ref, vc_ref, bias_ref, pm_ref, o_ref, lse_ref, *, n_heads):
    first = (pl.program_id(2) == 0).astype(F32)
    pm = pm_ref[...] * first
    lane = lax.broadcasted_iota(jnp.int32, (ATTN_BLOCK, 2 * HEAD_DIM), 1)
    low = lane < HEAD_DIM
    for hp in range(n_heads // 2):
        sl = slice(hp * 2 * HEAD_DIM, (hp + 1) * 2 * HEAD_DIM)
        q2 = q_ref[:, sl]
        k2 = jnp.concatenate([kp_ref[:, sl], kc_ref[:, sl]], axis=0)
        v2 = jnp.concatenate([vp_ref[:, sl], vc_ref[:, sl]], axis=0)
        res = []
        for par in range(2):
            keep = low if par == 0 else jnp.logical_not(low)
            qm = jnp.where(keep, q2, jnp.zeros_like(q2))
            s = _dot_nt(qm, k2) + bias_ref[2 * hp + par] + pm
            m = jnp.max(s, axis=-1, keepdims=True)
            p = jnp.exp(s - m)
            l = jnp.sum(p, axis=-1, keepdims=True)
            o = _dot(p.astype(BF16), v2) / l
            res.append((o, m + jnp.log(l)))
        o_ref[:, sl] = jnp.where(low, res[0][0], res[1][0]).astype(o_ref.dtype)
        lse_ref[:, sl] = jnp.where(low, res[0][1], res[1][1])


def _attn_tables(dil, slopes_g):
    i = np.arange(ATTN_BLOCK)[:, None]
    j = np.arange(2 * ATTN_BLOCK)[None, :]
    delta = i + ATTN_BLOCK - j
    valid = (delta >= 0) & (delta <= N_BACK)
    dist = (delta * dil).astype(np.float32)
    bias = np.where(valid[None], -slopes_g[:, None, None] * dist[None], NEG).astype(np.float32)
    pm = np.where(j < ATTN_BLOCK, NEG, 0.0).astype(np.float32) * np.ones((ATTN_BLOCK, 1), np.float32)
    return jnp.asarray(bias), jnp.asarray(pm)


def _attn_call(q, k, v, g, n_heads, slopes_g):
    n, dil, sub, width = q.shape
    assert dil == DILATIONS[g] and sub % ATTN_BLOCK == 0 and width == n_heads * HEAD_DIM
    nb = sub // ATTN_BLOCK
    cur = pl.BlockSpec((None, None, ATTN_BLOCK, width), lambda b, r, i: (b, r, i, 0))
    prev = pl.BlockSpec((None, None, ATTN_BLOCK, width), lambda b, r, i: (b, r, jnp.maximum(i - 1, 0), 0))
    bias, pm = _attn_tables(dil, slopes_g)
    return pl.pallas_call(
        functools.partial(_attn_kernel, n_heads=n_heads),
        grid=(n, dil, nb),
        in_specs=[cur, prev, cur, prev, cur, _resident(bias.shape), _resident(pm.shape)],
        out_specs=[cur, cur],
        out_shape=[jax.ShapeDtypeStruct(q.shape, BF16), jax.ShapeDtypeStruct(q.shape, F32)],
        compiler_params=pltpu.CompilerParams(dimension_semantics=("arbitrary", "arbitrary", "arbitrary"),
                                             vmem_limit_bytes=V7X_VMEM_LIMIT),
        name=f"attn_g{g}",
    )(q, k, k, v, v, bias, pm)


def _sattn_kernel(*refs, n_heads):
    qs, ks, vs = refs[0:N_GROUPS], refs[N_GROUPS:2 * N_GROUPS], refs[2 * N_GROUPS:3 * N_GROUPS]
    caches = refs[3 * N_GROUPS:4 * N_GROUPS]
    biases = refs[4 * N_GROUPS:5 * N_GROUPS]
    bn_ref, o_ref = refs[5 * N_GROUPS:]
    for h in range(n_heads):
        scores, values = [], []
        hs = slice(h * HEAD_DIM, (h + 1) * HEAD_DIM)
        for g in range(N_GROUPS):
            qh = qs[g][0, :, hs]
            scores.append(_dot(qh, caches[g][0, h].astype(BF16)) + biases[g][h])
            values.append(caches[g][1, h].astype(BF16))
            scores.append(_dot_nt(qh, ks[g][0, :, hs].astype(BF16)) + bn_ref[g, h])
            values.append(vs[g][0, :, hs].astype(BF16))
        m = functools.reduce(jnp.maximum, [jnp.max(s, axis=-1, keepdims=True) for s in scores])
        ps = [jnp.exp(s - m) for s in scores]
        l = functools.reduce(lambda a, b: a + b, [jnp.sum(p, axis=-1, keepdims=True) for p in ps])
        acc = jnp.zeros((qh.shape[0], HEAD_DIM), F32)
        for i, (p, v) in enumerate(zip(ps, values)):
            pb = p.astype(BF16)
            acc = acc + (_dot_nt(pb, v) if i % 2 == 0 else _dot(pb, v))
        o_ref[0, :, h * HEAD_DIM:(h + 1) * HEAD_DIM] = (acc / l).astype(o_ref.dtype)


def _sattn_tables(n_heads, t_new, slopes, wbs):
    t = np.arange(t_new)[:, None]
    biases, bn = [], []
    for g in range(N_GROUPS):
        d, wb = DILATIONS[g], wbs[g]
        dist = wb + t - np.arange(wb)[None, :]
        valid = (dist % d == 0) & (dist // d <= N_BACK)
        biases.append(np.where(valid[None], -slopes[g][:, None, None] * dist[None], NEG).astype(np.float32))
        dist = t - np.arange(t_new)[None, :]
        valid = (dist >= 0) & (dist % d == 0) & (dist // d <= N_BACK)
        bn.append(np.where(valid[None], -slopes[g][:, None, None] * dist[None], NEG).astype(np.float32))
    return [jnp.asarray(b) for b in biases], jnp.asarray(np.stack(bn))


def _sattn_call(qs, ks, vs, caches, n_heads, slopes):
    n, t_new, width = qs[0].shape
    wbs = [c.shape[1] for c in caches]
    biases, bn = _sattn_tables(n_heads, t_new, slopes, wbs)
    views = [jnp.transpose(c, (0, 2, 3, 4, 1)) for c in caches]
    new = pl.BlockSpec((1, t_new, width), lambda i: (i, 0, 0))
    return pl.pallas_call(
        functools.partial(_sattn_kernel, n_heads=n_heads),
        grid=(n,),
        in_specs=[new] * (3 * N_GROUPS)
        + [pl.BlockSpec((None, 2, n_heads, HEAD_DIM, wb), lambda i: (i, 0, 0, 0, 0)) for wb in wbs]
        + [_resident(b.shape) for b in biases] + [_resident(bn.shape)],
        out_specs=new,
        out_shape=jax.ShapeDtypeStruct((n, t_new, width), BF16),
        compiler_params=pltpu.CompilerParams(dimension_semantics=("arbitrary",), vmem_limit_bytes=V7X_VMEM_LIMIT),
        name="sattn",
    )(*qs, *ks, *vs, *views, *biases, bn)


def _hgrn_tables(c):
    t = np.arange(c)
    tri = (t[None, :] <= t[:, None]).astype(np.float32)
    masks = [np.eye(c, dtype=bool)]
    signs = []
    w = c // 2
    while w >= 1:
        blk = t // (2 * w)
        late = (t // w) % 2 == 1
        masks.append((blk[:, None] == blk[None, :]) & late[:, None] & ~late[None, :])
        signs.append(np.where(late, 1.0, -1.0))
        w //= 2
    masks = np.stack(masks).astype(np.float32)
    signs = np.repeat(np.stack(signs).reshape(-1, 1), HGRN_HEAD, axis=1).astype(np.float32)
    return jnp.asarray(tri, BF16), jnp.asarray(masks), jnp.asarray(signs)


def _level_offsets(b, g, c):
    width = b.shape[1]
    out = []
    w = c // 2
    while w >= 8:
        parts = []
        for k in range(c // (2 * w)):
            mid = k * 2 * w + w - 1
            parts.append(b[k * 2 * w:(k + 1) * 2 * w] - b[mid:mid + 1])
        out.append(parts[0] if len(parts) == 1 else jnp.concatenate(parts, axis=0))
        w //= 2
    b3 = b.reshape(c // 8, 8, width)
    sub = lax.broadcasted_iota(jnp.int32, b3.shape, 1)
    if c >= 8:
        out.append((b3 - jnp.broadcast_to(b3[:, 3:4], b3.shape)).reshape(c, width))
    ref = jnp.where(sub < 4, jnp.broadcast_to(b3[:, 1:2], b3.shape), jnp.broadcast_to(b3[:, 5:6], b3.shape))
    out.append((b3 - ref).reshape(c, width))
    row = lax.broadcasted_iota(jnp.int32, b.shape, 0)
    out.append(jnp.where(row % 2 == 1, g, 0.0))
    return out


def _hgrn_chunk(sq, g, kf, iv, state, tri, masks, signs, og, c):
    n_lev = masks.shape[0] - 1
    g1 = g.astype(BF16)
    r1 = g - g1.astype(F32)
    g2 = r1.astype(BF16)
    g3 = (r1 - g2.astype(F32)).astype(BF16)
    tri_v = tri[...]
    b = _dot(tri_v, g1) + _dot(tri_v, g2) + _dot(tri_v, g3)
    offs = _level_offsets(b, g, c)
    e_cum = jnp.exp(b)
    e_tail = jnp.exp(b[c - 1:c] - b)
    og = og[...]
    outs, new_state = [], []
    for h in range(len(state)):
        hs = slice(h * HGRN_HEAD, (h + 1) * HGRN_HEAD)
        sqh, kfh, ivh = sq[:, hs], kf[:, hs], iv[:, hs]
        sqf, kff = sqh.astype(F32), kfh.astype(F32)
        eb = e_cum[:, hs]
        o = _dot((sqf * eb).astype(BF16), state[h].astype(BF16))
        att = masks[0] * _dot_nt(sqh, kfh)
        for lv in range(n_lev):
            sgn = signs[lv * c:(lv + 1) * c]
            x = (jnp.where(sgn > 0.0, sqf, kff) * jnp.exp(offs[lv][:, hs] * sgn)).astype(BF16)
            att = att + masks[1 + lv] * _dot_nt(x, x)
        o = o + _dot(att.astype(BF16), ivh)
        kt = (kff * e_tail[:, hs]).astype(BF16)
        decay = jnp.broadcast_to(eb[c - 1:c, :], (HGRN_HEAD, HGRN_HEAD)).T
        new_state.append(decay * state[h] + _dot_tn(kt, ivh))
        outs.append(o * lax.rsqrt(jnp.mean(o * o, axis=-1, keepdims=True) + EPS) * og)
    return outs, new_state


def _hgrn_prompt_kernel(sq_ref, g_ref, kf_ref, iv_ref, tri_ref, masks_ref, signs_ref, og_ref, o_ref, s_ref,
                        *, c, n_heads):
    @pl.when(pl.program_id(0) == 0)
    def _():
        s_ref[...] = jnp.zeros_like(s_ref)

    def body(ci, carry):
        rows = pl.ds(pl.multiple_of(ci * c, c), c)
        for b in range(sq_ref.shape[0]):
            state = [s_ref[b, h] for h in range(n_heads)]
            outs, new_state = _hgrn_chunk(sq_ref[b, rows, :], g_ref[b, rows, :], kf_ref[b, rows, :],
                                          iv_ref[b, rows, :], state, tri_ref, masks_ref, signs_ref, og_ref, c)
            for h in range(n_heads):
                s_ref[b, h] = new_state[h]
                o_ref[b, rows, h * HGRN_HEAD:(h + 1) * HGRN_HEAD] = outs[h].astype(o_ref.dtype)
        return carry

    lax.fori_loop(0, sq_ref.shape[1] // c, body, 0)


def _hgrn_prompt_call(sq, g, kf, iv, og, n, t, n_heads):
    width = n_heads * HGRN_HEAD
    c = min(HGRN_CHUNK, t)
    tt = min(HGRN_STEP_TOKENS, t)
    assert t % tt == 0 and tt % c == 0 and n <= HGRN_MAX_SEQS
    tri, masks, signs = _hgrn_tables(c)
    tok = pl.BlockSpec((n, tt, width), lambda i: (0, i, 0))
    o, s = pl.pallas_call(
        functools.partial(_hgrn_prompt_kernel, c=c, n_heads=n_heads),
        grid=(t // tt,),
        in_specs=[tok, tok, tok, tok, _resident(tri.shape), _resident(masks.shape), _resident(signs.shape),
                  _resident((1, HGRN_HEAD))],
        out_specs=[tok, pl.BlockSpec((n, n_heads, HGRN_HEAD, HGRN_HEAD), lambda i: (0, 0, 0, 0))],
        out_shape=[jax.ShapeDtypeStruct((n, t, width), BF16),
                   jax.ShapeDtypeStruct((n, n_heads, HGRN_HEAD, HGRN_HEAD), F32)],
        compiler_params=pltpu.CompilerParams(dimension_semantics=("arbitrary",), vmem_limit_bytes=V7X_VMEM_LIMIT),
        name="hgrn_prompt",
    )(sq.reshape(n, t, width), g.reshape(n, t, width), kf.reshape(n, t, width), iv.reshape(n, t, width),
      tri, masks, signs, og.reshape(1, HGRN_HEAD))
    return o.reshape(n * t, width), s


def _hgrn_sample_kernel(sq_ref, g_ref, kf_ref, iv_ref, s0_ref, tri_ref, masks_ref, signs_ref, og_ref, o_ref, s_ref,
                        *, c, n_heads):
    for b in range(sq_ref.shape[0]):
        state = [s0_ref[b, h] for h in range(n_heads)]
        outs, new_state = _hgrn_chunk(sq_ref[b], g_ref[b], kf_ref[b], iv_ref[b], state, tri_ref, masks_ref, signs_ref,
                                      og_ref, c)
        for h in range(n_heads):
            s_ref[b, h] = new_state[h]
            o_ref[b, :, h * HGRN_HEAD:(h + 1) * HGRN_HEAD] = outs[h].astype(o_ref.dtype)


def _hgrn_sample_call(sq, g, kf, iv, s0, og, n, t, n_heads):
    width = n_heads * HGRN_HEAD
    bn = min(HGRN_MAX_SEQS, n)
    assert n % bn == 0 and t % 8 == 0
    tri, masks, signs = _hgrn_tables(t)
    tok = pl.BlockSpec((bn, t, width), lambda i: (i, 0, 0))
    st = pl.BlockSpec((bn, n_heads, HGRN_HEAD, HGRN_HEAD), lambda i: (i, 0, 0, 0))
    o, s = pl.pallas_call(
        functools.partial(_hgrn_sample_kernel, c=t, n_heads=n_heads),
        grid=(n // bn,),
        in_specs=[tok, tok, tok, tok, st, _resident(tri.shape), _resident(masks.shape), _resident(signs.shape),
                  _resident((1, HGRN_HEAD))],
        out_specs=[tok, st],
        out_shape=[jax.ShapeDtypeStruct((n, t, width), BF16),
                   jax.ShapeDtypeStruct((n, n_heads, HGRN_HEAD, HGRN_HEAD), F32)],
        compiler_params=pltpu.CompilerParams(dimension_semantics=("arbitrary",), vmem_limit_bytes=V7X_VMEM_LIMIT),
        name="hgrn_sample",
    )(sq.reshape(n, t, width), g.reshape(n, t, width), kf.reshape(n, t, width), iv.reshape(n, t, width),
      s0, tri, masks, signs, og.reshape(1, HGRN_HEAD))
    return o.reshape(n * t, width), s


def _load_tokens(ref, scr):
    bs, dil, rows, width = ref.shape
    if dil == 1:
        return ref[...].reshape(bs * rows, width).astype(F32)
    for r in range(dil):
        for c in range(width // LANES):
            scr[c, pl.ds(r, rows, stride=dil), :] = ref[0, r, :, c * LANES:(c + 1) * LANES].astype(F32)
    return jnp.concatenate([scr[c] for c in range(width // LANES)], axis=1)


def _out_kernel(*refs, n_att):
    att = refs[:2 * n_att] if n_att > 1 else refs[:1]
    n_in = len(att)
    sza_ref, ob_ref, szb_ref, sga_ref, sgb_ref, x_ref, gate_ref, wa_ref, wb_ref, wo_ref, y_ref, scr = refs[n_in:]
    if n_att > 1:
        os_ = [_load_tokens(r, scr) for r in att[:n_att]]
        ls_ = [_load_tokens(r, scr) for r in att[n_att:]]
        m = functools.reduce(jnp.maximum, ls_)
        es = [jnp.exp(l - m) for l in ls_]
        den = functools.reduce(lambda a, b: a + b, es)
        num = functools.reduce(lambda a, b: a + b, [e * o for e, o in zip(es, os_)])
        o_a = num / den
    else:
        o_a = _load_tokens(att[0], scr)
    branch_a = _dot((o_a * sza_ref[...].astype(F32)).astype(BF16), wa_ref[...])
    branch_b = _dot((ob_ref[...].astype(F32) * szb_ref[...].astype(F32)).astype(BF16), wb_ref[...])
    merged = sga_ref[...].astype(F32) * branch_a + sgb_ref[...].astype(F32) * branch_b
    upd = _dot(merged.astype(BF16), wo_ref[...])
    x = x_ref[...]
    y_ref[...] = x + gate_ref[...] * upd.reshape(x.shape)


def _out_call(att_inputs, sza, ob, szb, sga, sgb, x, gate, wa_bf, wb_bf, wo_bf):
    s, t, d = x.shape
    bs, bt = _token_tiling(s, t)
    tm = bs * bt
    n_t = t // bt

    def tok_spec(a):
        return pl.BlockSpec((tm, a.shape[1]), lambda i, j: (i * n_t + j, 0))

    def att_spec(a):
        dil = a.shape[1]
        assert dil == 1 or bs == 1
        return pl.BlockSpec((bs, dil, bt // dil, a.shape[3]), lambda i, j: (i, 0, j, 0))

    n_att = (len(att_inputs) + 1) // 2
    toks = [sza, ob, szb, sga, sgb]
    width = att_inputs[0].shape[3]
    return pl.pallas_call(
        functools.partial(_out_kernel, n_att=n_att),
        grid=(s // bs, n_t),
        in_specs=[att_spec(a) for a in att_inputs] + [tok_spec(a) for a in toks]
        + [pl.BlockSpec((bs, bt, d), lambda i, j: (i, j, 0)),
           pl.BlockSpec((bs, 1, d), lambda i, j: (i, 0, 0)),
           _resident(wa_bf.shape), _resident(wb_bf.shape), _resident(wo_bf.shape)],
        out_specs=pl.BlockSpec((bs, bt, d), lambda i, j: (i, j, 0)),
        out_shape=jax.ShapeDtypeStruct((s, t, d), F32),
        scratch_shapes=[pltpu.VMEM((width // LANES, tm, LANES), F32)],
        compiler_params=pltpu.CompilerParams(dimension_semantics=("arbitrary", "arbitrary"),
                                             vmem_limit_bytes=V7X_VMEM_LIMIT),
        name="out",
    )(*att_inputs, *toks, x, gate, wa_bf, wb_bf, wo_bf)


def kernel(x_prompt, x_sample, cache_kv_g0, cache_kv_g1, cache_kv_g2, state_hgrn, c_prompt, c_sample, norm_gain, w_ada, b_ada, w_in, q_norm_gain, k_norm_gain, hgrn_lb_logits, hgrn_out_norm_gain, w_branch_a, w_branch_b, w_out):
    depth = w_in.shape[0]
    assert depth == 1, "single-layer step"
    n_p, t_p, d = x_prompt.shape
    n_s, t_s, _ = x_sample.shape
    a_out = w_branch_a.shape[1]
    n_heads = a_out // HEAD_DIM
    b_f = w_branch_b.shape[1]
    b_heads = b_f // HGRN_HEAD
    slopes = _alibi_slopes(n_heads)

    n_c = n_p + n_s
    pad = (-n_c) % 8
    c_all = jnp.concatenate([c_prompt, c_sample, jnp.zeros((pad, d), F32)], axis=0)
    ada = _ada_call(c_all, w_ada[0], b_ada[0])
    shift, scale, gate = ada[:, :d], ada[:, d:2 * d], ada[:, 2 * d:]

    w_in_bf = w_in[0].astype(BF16)
    wa_bf, wb_bf, wo_bf = w_branch_a[0].astype(BF16), w_branch_b[0].astype(BF16), w_out[0].astype(BF16)
    head_of = np.arange(a_out) // HEAD_DIM
    bd = jnp.asarray(head_of[:, None] == head_of[None, :], BF16)
    qg = (jnp.tile(q_norm_gain[0], n_heads) * ATTN_SCALE).reshape(1, a_out)
    kg = jnp.tile(k_norm_gain[0], n_heads).reshape(1, a_out)
    og = hgrn_out_norm_gain[0]

    def cond(lo, n):
        return [a[lo:lo + n].reshape(n, 1, d) for a in (shift, scale, gate)]

    sh, sc, gt = cond(0, n_p)
    outs = _proj_call(x_prompt, sh, sc, norm_gain[0], w_in_bf, bd, qg, kg, hgrn_lb_logits, n_groups=N_GROUPS,
                      dils=DILATIONS, a_out=a_out, b_f=b_f, kv_dtype=BF16)
    qs, ks, vs = outs[0:N_GROUPS], outs[N_GROUPS:2 * N_GROUPS], outs[2 * N_GROUPS:3 * N_GROUPS]
    sza, sqb, g, kf, ib, szb, sga, sgb = outs[3 * N_GROUPS:]
    att = [_attn_call(qs[gi], ks[gi], vs[gi], gi, n_heads, slopes[gi]) for gi in range(N_GROUPS)]
    ob, hgrn_p = _hgrn_prompt_call(sqb, g, kf, ib, og, n_p, t_p, b_heads)
    y_p = _out_call([a[0] for a in att] + [a[1] for a in att], sza, ob, szb, sga, sgb, x_prompt, gt,
                    wa_bf, wb_bf, wo_bf)
    kv_p = []
    for gi in range(N_GROUPS):
        w, dl = min(WINDOWS[gi], t_p), DILATIONS[gi]

        def last_rows(a):
            return jnp.swapaxes(a[:, :, (t_p - w) // dl:, :], 1, 2).reshape(n_p, w, a_out)

        rows = jnp.stack([last_rows(ks[gi]), last_rows(vs[gi])], axis=2)
        kv_p.append(rows.astype(F32).reshape(1, n_p, w, 2, n_heads, HEAD_DIM))

    sh, sc, gt = cond(n_p, n_s)
    outs = _proj_call(x_sample, sh, sc, norm_gain[0], w_in_bf, bd, qg, kg, hgrn_lb_logits, n_groups=N_GROUPS,
                      dils=(1,) * N_GROUPS, a_out=a_out, b_f=b_f, kv_dtype=F32)
    qs, ks, vs = [[a.reshape(n_s, t_s, a_out) for a in outs[i * N_GROUPS:(i + 1) * N_GROUPS]] for i in range(3)]
    sza, sqb, g, kf, ib, szb, sga, sgb = outs[3 * N_GROUPS:]
    o_a = _sattn_call(qs, ks, vs, (cache_kv_g0[0], cache_kv_g1[0], cache_kv_g2[0]), n_heads, slopes)
    ob, hgrn_s = _hgrn_sample_call(sqb, g, kf, ib, state_hgrn[0], og, n_s, t_s, b_heads)
    y_s = _out_call([o_a.reshape(n_s, 1, t_s, a_out)], sza, ob, szb, sga, sgb, x_sample, gt, wa_bf, wb_bf, wo_bf)
    kv_s = [jnp.stack([ks[gi], vs[gi]], axis=2).reshape(1, n_s, t_s, 2, n_heads, HEAD_DIM) for gi in range(N_GROUPS)]

    return (y_p, y_s, kv_p[0], kv_p[1], kv_p[2], hgrn_p[None], kv_s[0], kv_s[1], kv_s[2], hgrn_s[None])
```

```python
import functools

import numpy as np
import jax
import jax.numpy as jnp
from jax import lax
from jax.experimental import pallas as pl
from jax.experimental.pallas import tpu as pltpu

F32 = jnp.float32
BF16 = jnp.bfloat16

HEAD_DIM = 64
N_GROUPS = 3
WINDOWS = (128, 512, 2048)
DILATIONS = (1, 4, 16)
N_BACK = 128
ATTN_BLOCK = 128
ATTN_BLOCKS_PER_STEP = 4
ATTN_SCALE = HEAD_DIM ** -0.5
HGRN_HEAD = 128
LANES = 128
MXU_WIDTH = 256
LOG2E = 1.4426950408889634
EPS = 1e-6
NEG = -1e30

V7X_VMEM_LIMIT = 56 * 1024 * 1024
PROJ_TOKENS = 512
HGRN_CHUNK = 64
HGRN_STEP_TOKENS = 512
HGRN_MAX_SEQS = 4


def _sigmoid(x):
    return 1.0 / (1.0 + jnp.exp(-x))


def _silu(x):
    return x * _sigmoid(x)


def _dot(a, b):
    return jnp.dot(a, b, preferred_element_type=F32)


def _dot_nt(a, b):
    return lax.dot_general(a, b, (((1,), (1,)), ((), ())), preferred_element_type=F32)


def _dot_tn(a, b):
    return lax.dot_general(a, b, (((0,), (0,)), ((), ())), preferred_element_type=F32)


def _alibi_slopes(n_heads_per_group):
    a_heads = N_GROUPS * n_heads_per_group
    s = 2.0 ** (-8.0 * (np.arange(a_heads) + 1) / a_heads)
    return s.astype(np.float32).reshape(N_GROUPS, n_heads_per_group)


def _resident(shape):
    nd = len(shape)
    return pl.BlockSpec(shape, lambda *_: (0,) * nd, pipeline_mode=pl.Buffered(1))


def _ada_kernel(c_ref, w_ref, b_ref, o_ref):
    o_ref[...] = _dot(_silu(c_ref[...]).astype(BF16), w_ref[...]) + b_ref[...]


def _ada_call(c, w_ada, b_ada):
    n, d = c.shape
    cols = w_ada.shape[1]
    blk = d
    return pl.pallas_call(
        _ada_kernel,
        grid=(cols // blk,),
        in_specs=[pl.BlockSpec((n, d), lambda j: (0, 0)),
                  pl.BlockSpec((d, blk), lambda j: (0, j)),
                  pl.BlockSpec((1, blk), lambda j: (0, j))],
        out_specs=pl.BlockSpec((n, blk), lambda j: (0, j)),
        out_shape=jax.ShapeDtypeStruct((n, cols), F32),
        name="ada",
    )(c, w_ada.astype(BF16), b_ada.reshape(1, cols))


def _store_dilated(out_ref, val, scr, dil):
    bs, d, rows, width = out_ref.shape
    if dil == 1:
        out_ref[...] = val.reshape(bs, 1, rows, width).astype(out_ref.dtype)
        return
    for c in range(width // LANES):
        scr[c] = val[:, c * LANES:(c + 1) * LANES]
    for r in range(dil):
        for c in range(width // LANES):
            out_ref[0, r, :, c * LANES:(c + 1) * LANES] = scr[c, pl.ds(r, rows, stride=dil), :].astype(out_ref.dtype)


def _proj_kernel(x_ref, shift_ref, scale_ref, ng_ref, w_ref, bd_ref, qg_ref, kg_ref, lbl_ref, *refs,
                 n_groups, dils, a_out, d_model):
    qkv_refs = refs[:3 * n_groups]
    sza_ref, sqb_ref, g_ref, kf_ref, ib_ref, szb_ref, sga_ref, sgb_ref, scr = refs[3 * n_groups:]
    x = x_ref[...]
    ms = jnp.mean(x * x, axis=-1, keepdims=True)
    h = x * lax.rsqrt(ms + EPS) * ng_ref[...]
    h = h * (1.0 + scale_ref[...]) + shift_ref[...]
    tm = h.shape[0] * h.shape[1]
    hb = h.reshape(tm, d_model).astype(BF16)
    tile = a_out

    def proj(c0):
        return _dot(hb, w_ref[:, c0:c0 + tile])

    def head_norm(u, gain):
        u2 = (u * u).astype(BF16)
        nb = bd_ref.shape[0]
        ss = jnp.concatenate([_dot(u2[:, c:c + nb], bd_ref[...]) for c in range(0, tile, nb)], axis=1)
        return u * lax.rsqrt(ss * (1.0 / HEAD_DIM) + EPS) * gain

    c0 = 0
    for kind in range(3):
        for g in range(n_groups):
            u = proj(c0)
            if kind == 0:
                u = head_norm(u, qg_ref[...])
            elif kind == 1:
                u = head_norm(u, kg_ref[...])
            _store_dilated(qkv_refs[kind * n_groups + g], u, scr, dils[g])
            c0 += tile
    sza_ref[...] = _silu(proj(c0)).astype(BF16)
    c0 += tile
    sqb_ref[...] = _silu(proj(c0)).astype(BF16)
    c0 += tile
    lbl = lbl_ref[...]
    lbe = jnp.exp(lbl - jnp.max(lbl, axis=0, keepdims=True))
    lb = lbe[0:1] / jnp.sum(lbe, axis=0, keepdims=True)
    fr = proj(c0)
    g_ref[...] = jnp.log(lb + (1.0 - lb) * _sigmoid(fr))
    kf_ref[...] = ((1.0 - lb) * _sigmoid(-fr)).astype(BF16)
    c0 += tile
    ib_ref[...] = proj(c0).astype(BF16)
    c0 += tile
    szb_ref[...] = _silu(proj(c0)).astype(BF16)
    c0 += tile
    for j in range(d_model // tile):
        sga_ref[:, j * tile:(j + 1) * tile] = _sigmoid(proj(c0)).astype(BF16)
        c0 += tile
    for j in range(d_model // tile):
        sgb_ref[:, j * tile:(j + 1) * tile] = _sigmoid(proj(c0)).astype(BF16)
        c0 += tile


def _token_tiling(s, t):
    if t >= PROJ_TOKENS:
        assert t % PROJ_TOKENS == 0
        return 1, PROJ_TOKENS
    bs = min(s, PROJ_TOKENS // t)
    assert s % bs == 0 and t % 8 == 0
    return bs, t


def _proj_call(x, shift, scale, norm_gain, w_in_bf, bd, qg, kg, lb_logits, *, n_groups, dils, a_out, b_f, kv_dtype):
    s, t, d = x.shape
    bs, bt = _token_tiling(s, t)
    tm = bs * bt
    n_t = t // bt
    ntok = s * t
    grid = (s // bs, n_t)
    assert all(dl == 1 or (bs == 1 and bt % (dl * 16) == 0) for dl in dils)

    def tok_spec(cols):
        return pl.BlockSpec((tm, cols), lambda i, j: (i * n_t + j, 0))

    qkv_dtypes = [BF16] * n_groups + [kv_dtype] * (2 * n_groups)
    qkv_shapes = [jax.ShapeDtypeStruct((s, dils[g], t // dils[g], a_out), dt)
                  for g, dt in zip(list(range(n_groups)) * 3, qkv_dtypes)]
    qkv_specs = [pl.BlockSpec((bs, dils[g], bt // dils[g], a_out), lambda i, j: (i, 0, j, 0))
                 for g in list(range(n_groups)) * 3]
    out_cols = [(a_out, BF16), (b_f, BF16), (b_f, F32), (b_f, BF16), (b_f, BF16), (b_f, BF16), (d, BF16), (d, BF16)]
    kern = functools.partial(_proj_kernel, n_groups=n_groups, dils=tuple(dils), a_out=a_out, d_model=d)
    return pl.pallas_call(
        kern,
        grid=grid,
        in_specs=[pl.BlockSpec((bs, bt, d), lambda i, j: (i, j, 0)),
                  pl.BlockSpec((bs, 1, d), lambda i, j: (i, 0, 0)),
                  pl.BlockSpec((bs, 1, d), lambda i, j: (i, 0, 0)),
                  _resident((1, d)),
                  _resident(w_in_bf.shape),
                  _resident(bd.shape),
                  _resident(qg.shape),
                  _resident(kg.shape),
                  _resident(lb_logits.shape)],
        out_specs=qkv_specs + [tok_spec(c) for c, _ in out_cols],
        out_shape=qkv_shapes + [jax.ShapeDtypeStruct((ntok, c), dt) for c, dt in out_cols],
        scratch_shapes=[pltpu.VMEM((a_out // LANES, tm, LANES), F32)],
        compiler_params=pltpu.CompilerParams(dimension_semantics=("arbitrary", "arbitrary"),
                                             vmem_limit_bytes=V7X_VMEM_LIMIT),
        name="proj",
    )(x, shift, scale, norm_gain.reshape(1, d), w_in_bf, bd, qg, kg, lb_logits)


def _attn_kernel(q_ref, kp_ref, kc_ref, vp_ref, vc_ref, bias_ref, o_ref, lse_ref, *, n_heads, n_blk):
    first = (pl.program_id(2) == 0).astype(jnp.int32)
    lane = lax.broadcasted_iota(jnp.int32, (ATTN_BLOCK, 2 * HEAD_DIM), 1)
    low = lane < HEAD_DIM
    for blk in range(n_blk):
        rows = slice(blk * ATTN_BLOCK, (blk + 1) * ATTN_BLOCK)
        before = slice((blk - 1) * ATTN_BLOCK, blk * ATTN_BLOCK)
        table = first if blk == 0 else 0
        for hp in range(n_heads // 2):
            sl = slice(hp * 2 * HEAD_DIM, (hp + 1) * 2 * HEAD_DIM)
            q2 = q_ref[rows, sl]
            k_prev = kp_ref[:, sl] if blk == 0 else kc_ref[before, sl]
            v_prev = vp_ref[:, sl] if blk == 0 else vc_ref[before, sl]
            k2 = jnp.concatenate([k_prev, kc_ref[rows, sl]], axis=0)
            v2 = jnp.concatenate([v_prev, vc_ref[rows, sl]], axis=0)
            res = []
            for par in range(2):
                keep = low if par == 0 else jnp.logical_not(low)
                qm = jnp.where(keep, q2, jnp.zeros_like(q2))
                s = _dot_nt(qm, k2) + bias_ref[table, 2 * hp + par]
                m = jnp.max(s, axis=-1, keepdims=True)
                p = jnp.exp2(s - m)
                l = jnp.sum(p, axis=-1, keepdims=True)
                o = _dot(p.astype(BF16), v2) / l
                res.append((o, m + jnp.log2(l)))
            o_ref[rows, sl] = jnp.where(low, res[0][0], res[1][0]).astype(o_ref.dtype)
            lse_ref[rows, sl] = jnp.where(low, res[0][1], res[1][1])


def _attn_tables(dil, slopes_g):
    i = np.arange(ATTN_BLOCK)[:, None]
    j = np.arange(2 * ATTN_BLOCK)[None, :]
    delta = i + ATTN_BLOCK - j
    valid = (delta >= 0) & (delta <= N_BACK)
    dist = (delta * dil).astype(np.float32)
    bias = np.where(valid[None], -slopes_g[:, None, None] * dist[None] * LOG2E, NEG).astype(np.float32)
    first = np.where((j >= ATTN_BLOCK)[None], bias, NEG).astype(np.float32)
    return jnp.asarray(np.stack([bias, first]))


def _attn_call(q, k, v, g, n_heads, slopes_g):
    n, dil, sub, width = q.shape
    assert dil == DILATIONS[g] and sub % ATTN_BLOCK == 0 and width == n_heads * HEAD_DIM
    nb = sub // ATTN_BLOCK
    n_blk = min(ATTN_BLOCKS_PER_STEP, nb)
    assert nb % n_blk == 0
    cur = pl.BlockSpec((None, None, n_blk * ATTN_BLOCK, width), lambda b, r, i: (b, r, i, 0))
    prev = pl.BlockSpec((None, None, ATTN_BLOCK, width), lambda b, r, i: (b, r, jnp.maximum(i * n_blk - 1, 0), 0))
    bias = _attn_tables(dil, slopes_g)
    return pl.pallas_call(
        functools.partial(_attn_kernel, n_heads=n_heads, n_blk=n_blk),
        grid=(n, dil, nb // n_blk),
        in_specs=[cur, prev, cur, prev, cur, _resident(bias.shape)],
        out_specs=[cur, cur],
        out_shape=[jax.ShapeDtypeStruct(q.shape, BF16), jax.ShapeDtypeStruct(q.shape, F32)],
        compiler_params=pltpu.CompilerParams(dimension_semantics=("arbitrary", "arbitrary", "arbitrary"),
                                             vmem_limit_bytes=V7X_VMEM_LIMIT),
        name=f"attn_g{g}",
    )(q, k, k, v, v, bias)


def _sattn_kernel(*refs, n_heads):
    qs, ks, vs = refs[0:N_GROUPS], refs[N_GROUPS:2 * N_GROUPS], refs[2 * N_GROUPS:3 * N_GROUPS]
    caches = refs[3 * N_GROUPS:4 * N_GROUPS]
    biases = refs[4 * N_GROUPS:5 * N_GROUPS]
    bn_ref, o_ref = refs[5 * N_GROUPS:]
    for h in range(n_heads):
        scores, values = [], []
        hs = slice(h * HEAD_DIM, (h + 1) * HEAD_DIM)
        for g in range(N_GROUPS):
            qh = qs[g][0, :, hs]
            scores.append(_dot(qh, caches[g][0, h].astype(BF16)) + biases[g][h])
            values.append(caches[g][1, h].astype(BF16))
            scores.append(_dot_nt(qh, ks[g][0, :, hs].astype(BF16)) + bn_ref[g, h])
            values.append(vs[g][0, :, hs].astype(BF16))
        m = functools.reduce(jnp.maximum, [jnp.max(s, axis=-1, keepdims=True) for s in scores])
        ps = [jnp.exp2(s - m) for s in scores]
        l = functools.reduce(lambda a, b: a + b, [jnp.sum(p, axis=-1, keepdims=True) for p in ps])
        acc = jnp.zeros((qh.shape[0], HEAD_DIM), F32)
        for i, (p, v) in enumerate(zip(ps, values)):
            pb = p.astype(BF16)
            acc = acc + (_dot_nt(pb, v) if i % 2 == 0 else _dot(pb, v))
        o_ref[0, :, h * HEAD_DIM:(h + 1) * HEAD_DIM] = (acc / l).astype(o_ref.dtype)


def _sattn_tables(n_heads, t_new, slopes, wbs):
    t = np.arange(t_new)[:, None]
    biases, bn = [], []
    for g in range(N_GROUPS):
        d, wb = DILATIONS[g], wbs[g]
        dist = wb + t - np.arange(wb)[None, :]
        valid = (dist % d == 0) & (dist // d <= N_BACK)
        biases.append(np.where(valid[None], -slopes[g][:, None, None] * dist[None] * LOG2E, NEG).astype(np.float32))
        dist = t - np.arange(t_new)[None, :]
        valid = (dist >= 0) & (dist % d == 0) & (dist // d <= N_BACK)
        bn.append(np.where(valid[None], -slopes[g][:, None, None] * dist[None] * LOG2E, NEG).astype(np.float32))
    return [jnp.asarray(b) for b in biases], jnp.asarray(np.stack(bn))


def _sattn_call(qs, ks, vs, caches, n_heads, slopes):
    n, t_new, width = qs[0].shape
    wbs = [c.shape[1] for c in caches]
    biases, bn = _sattn_tables(n_heads, t_new, slopes, wbs)
    views = [jnp.transpose(c, (0, 2, 3, 4, 1)) for c in caches]
    new = pl.BlockSpec((1, t_new, width), lambda i: (i, 0, 0))
    return pl.pallas_call(
        functools.partial(_sattn_kernel, n_heads=n_heads),
        grid=(n,),
        in_specs=[new] * (3 * N_GROUPS)
        + [pl.BlockSpec((None, 2, n_heads, HEAD_DIM, wb), lambda i: (i, 0, 0, 0, 0)) for wb in wbs]
        + [_resident(b.shape) for b in biases] + [_resident(bn.shape)],
        out_specs=new,
        out_shape=jax.ShapeDtypeStruct((n, t_new, width), BF16),
        compiler_params=pltpu.CompilerParams(dimension_semantics=("arbitrary",), vmem_limit_bytes=V7X_VMEM_LIMIT),
        name="sattn",
    )(*qs, *ks, *vs, *views, *biases, bn)


def _hgrn_tables(c):
    t = np.arange(c)
    tri = (t[None, :] <= t[:, None]).astype(np.float32)
    masks = [np.eye(c, dtype=bool)]
    signs = []
    w = c // 2
    while w >= 1:
        blk = t // (2 * w)
        late = (t // w) % 2 == 1
        masks.append((blk[:, None] == blk[None, :]) & late[:, None] & ~late[None, :])
        signs.append(np.where(late, 1.0, -1.0))
        w //= 2
    masks = np.stack(masks).astype(np.float32)
    signs = np.repeat(np.stack(signs).reshape(-1, 1), HGRN_HEAD, axis=1).astype(np.float32)
    return jnp.asarray(tri, BF16), jnp.asarray(masks), jnp.asarray(signs)


def _level_offsets(b, g, c):
    width = b.shape[1]
    out = []
    w = c // 2
    while w >= 8:
        parts = []
        for k in range(c // (2 * w)):
            mid = k * 2 * w + w - 1
            parts.append(b[k * 2 * w:(k + 1) * 2 * w] - b[mid:mid + 1])
        out.append(parts[0] if len(parts) == 1 else jnp.concatenate(parts, axis=0))
        w //= 2
    b3 = b.reshape(c // 8, 8, width)
    sub = lax.broadcasted_iota(jnp.int32, b3.shape, 1)
    if c >= 8:
        out.append((b3 - jnp.broadcast_to(b3[:, 3:4], b3.shape)).reshape(c, width))
    ref = jnp.where(sub < 4, jnp.broadcast_to(b3[:, 1:2], b3.shape), jnp.broadcast_to(b3[:, 5:6], b3.shape))
    out.append((b3 - ref).reshape(c, width))
    row = lax.broadcasted_iota(jnp.int32, b.shape, 0)
    out.append(jnp.where(row % 2 == 1, g, 0.0))
    return out


def _hgrn_chunk(sq, g, kf, iv, state, tri, masks, signs, og, c):
    n_lev = masks.shape[0] - 1
    g1 = g.astype(BF16)
    r1 = g - g1.astype(F32)
    g2 = r1.astype(BF16)
    g3 = (r1 - g2.astype(F32)).astype(BF16)
    tri_v = tri[...]
    b = _dot(tri_v, g1) + _dot(tri_v, g2) + _dot(tri_v, g3)
    offs = _level_offsets(b, g, c)
    e_cum = jnp.exp(b)
    e_tail = jnp.exp(b[c - 1:c] - b)
    og = og[...]
    outs, new_state = [], []
    for h in range(len(state)):
        hs = slice(h * HGRN_HEAD, (h + 1) * HGRN_HEAD)
        sqh, kfh, ivh = sq[:, hs], kf[:, hs], iv[:, hs]
        sqf, kff = sqh.astype(F32), kfh.astype(F32)
        eb = e_cum[:, hs]
        o = _dot((sqf * eb).astype(BF16), state[h].astype(BF16))
        att = masks[0] * _dot_nt(sqh, kfh)
        for lv in range(n_lev):
            sgn = signs[lv * c:(lv + 1) * c]
            x = (jnp.where(sgn > 0.0, sqf, kff) * jnp.exp(offs[lv][:, hs] * sgn)).astype(BF16)
            att = att + masks[1 + lv] * _dot_nt(x, x)
        o = o + _dot(att.astype(BF16), ivh)
        kt = (kff * e_tail[:, hs]).astype(BF16)
        decay = jnp.broadcast_to(eb[c - 1:c, :], (HGRN_HEAD, HGRN_HEAD)).T
        new_state.append(decay * state[h] + _dot_tn(kt, ivh))
        outs.append(o * lax.rsqrt(jnp.mean(o * o, axis=-1, keepdims=True) + EPS) * og)
    return outs, new_state


def _hgrn_prompt_kernel(sq_ref, g_ref, kf_ref, iv_ref, tri_ref, masks_ref, signs_ref, og_ref, o_ref, s_ref,
                        *, c, n_heads):
    @pl.when(pl.program_id(0) == 0)
    def _():
        s_ref[...] = jnp.zeros_like(s_ref)

    def body(ci, carry):
        rows = pl.ds(pl.multiple_of(ci * c, c), c)
        for b in range(sq_ref.shape[0]):
            state = [s_ref[b, h] for h in range(n_heads)]
            outs, new_state = _hgrn_chunk(sq_ref[b, rows, :], g_ref[b, rows, :], kf_ref[b, rows, :],
                                          iv_ref[b, rows, :], state, tri_ref, masks_ref, signs_ref, og_ref, c)
            for h in range(n_heads):
                s_ref[b, h] = new_state[h]
                o_ref[b, rows, h * HGRN_HEAD:(h + 1) * HGRN_HEAD] = outs[h].astype(o_ref.dtype)
        return carry

    lax.fori_loop(0, sq_ref.shape[1] // c, body, 0)


def _hgrn_prompt_call(sq, g, kf, iv, og, n, t, n_heads):
    width = n_heads * HGRN_HEAD
    c = min(HGRN_CHUNK, t)
    tt = min(HGRN_STEP_TOKENS, t)
    assert t % tt == 0 and tt % c == 0 and n <= HGRN_MAX_SEQS
    tri, masks, signs = _hgrn_tables(c)
    tok = pl.BlockSpec((n, tt, width), lambda i: (0, i, 0))
    o, s = pl.pallas_call(
        functools.partial(_hgrn_prompt_kernel, c=c, n_heads=n_heads),
        grid=(t // tt,),
        in_specs=[tok, tok, tok, tok, _resident(tri.shape), _resident(masks.shape), _resident(signs.shape),
                  _resident((1, HGRN_HEAD))],
        out_specs=[tok, pl.BlockSpec((n, n_heads, HGRN_HEAD, HGRN_HEAD), lambda i: (0, 0, 0, 0))],
        out_shape=[jax.ShapeDtypeStruct((n, t, width), BF16),
                   jax.ShapeDtypeStruct((n, n_heads, HGRN_HEAD, HGRN_HEAD), F32)],
        compiler_params=pltpu.CompilerParams(dimension_semantics=("arbitrary",), vmem_limit_bytes=V7X_VMEM_LIMIT),
        name="hgrn_prompt",
    )(sq.reshape(n, t, width), g.reshape(n, t, width), kf.reshape(n, t, width), iv.reshape(n, t, width),
      tri, masks, signs, og.reshape(1, HGRN_HEAD))
    return o.reshape(n * t, width), s


def _hgrn_sample_kernel(sq_ref, g_ref, kf_ref, iv_ref, s0_ref, tri_ref, masks_ref, signs_ref, og_ref, o_ref, s_ref,
                        *, c, n_heads):
    for b in range(sq_ref.shape[0]):
        state = [s0_ref[b, h] for h in range(n_heads)]
        outs, new_state = _hgrn_chunk(sq_ref[b], g_ref[b], kf_ref[b], iv_ref[b], state, tri_ref, masks_ref, signs_ref,
                                      og_ref, c)
        for h in range(n_heads):
            s_ref[b, h] = new_state[h]
            o_ref[b, :, h * HGRN_HEAD:(h + 1) * HGRN_HEAD] = outs[h].astype(o_ref.dtype)


def _hgrn_sample_call(sq, g, kf, iv, s0, og, n, t, n_heads):
    width = n_heads * HGRN_HEAD
    bn = min(HGRN_MAX_SEQS, n)
    assert n % bn == 0 and t % 8 == 0
    tri, masks, signs = _hgrn_tables(t)
    tok = pl.BlockSpec((bn, t, width), lambda i: (i, 0, 0))
    st = pl.BlockSpec((bn, n_heads, HGRN_HEAD, HGRN_HEAD), lambda i: (i, 0, 0, 0))
    o, s = pl.pallas_call(
        functools.partial(_hgrn_sample_kernel, c=t, n_heads=n_heads),
        grid=(n // bn,),
        in_specs=[tok, tok, tok, tok, st, _resident(tri.shape), _resident(masks.shape), _resident(signs.shape),
                  _resident((1, HGRN_HEAD))],
        out_specs=[tok, st],
        out_shape=[jax.ShapeDtypeStruct((n, t, width), BF16),
                   jax.ShapeDtypeStruct((n, n_heads, HGRN_HEAD, HGRN_HEAD), F32)],
        compiler_params=pltpu.CompilerParams(dimension_semantics=("arbitrary",), vmem_limit_bytes=V7X_VMEM_LIMIT),
        name="hgrn_sample",
    )(sq.reshape(n, t, width), g.reshape(n, t, width), kf.reshape(n, t, width), iv.reshape(n, t, width),
      s0, tri, masks, signs, og.reshape(1, HGRN_HEAD))
    return o.reshape(n * t, width), s


def _load_tokens(ref, scr):
    bs, dil, rows, width = ref.shape
    if dil == 1:
        return ref[...].reshape(bs * rows, width).astype(F32)
    for r in range(dil):
        for c in range(width // LANES):
            scr[c, pl.ds(r, rows, stride=dil), :] = ref[0, r, :, c * LANES:(c + 1) * LANES].astype(F32)
    return jnp.concatenate([scr[c] for c in range(width // LANES)], axis=1)


def _out_kernel(*refs, n_att):
    att = refs[:2 * n_att] if n_att > 1 else refs[:1]
    n_in = len(att)
    sza_ref, ob_ref, szb_ref, sga_ref, sgb_ref, x_ref, gate_ref, wa_ref, wb_ref, wo_ref, y_ref, scr = refs[n_in:]
    if n_att > 1:
        os_ = [_load_tokens(r, scr) for r in att[:n_att]]
        ls_ = [_load_tokens(r, scr) for r in att[n_att:]]
        m = functools.reduce(jnp.maximum, ls_)
        es = [jnp.exp2(l - m) for l in ls_]
        den = functools.reduce(lambda a, b: a + b, es)
        num = functools.reduce(lambda a, b: a + b, [e * o for e, o in zip(es, os_)])
        o_a = num / den
    else:
        o_a = _load_tokens(att[0], scr)
    branch_a = _dot((o_a * sza_ref[...].astype(F32)).astype(BF16), wa_ref[...])
    branch_b = _dot((ob_ref[...].astype(F32) * szb_ref[...].astype(F32)).astype(BF16), wb_ref[...])
    merged = sga_ref[...].astype(F32) * branch_a + sgb_ref[...].astype(F32) * branch_b
    upd = _dot(merged.astype(BF16), wo_ref[...])
    x = x_ref[...]
    y_ref[...] = x + gate_ref[...] * upd.reshape(x.shape)


def _out_call(att_inputs, sza, ob, szb, sga, sgb, x, gate, wa_bf, wb_bf, wo_bf):
    s, t, d = x.shape
    bs, bt = _token_tiling(s, t)
    tm = bs * bt
    n_t = t // bt

    def tok_spec(a):
        return pl.BlockSpec((tm, a.shape[1]), lambda i, j: (i * n_t + j, 0))

    def att_spec(a):
        dil = a.shape[1]
        assert dil == 1 or bs == 1
        return pl.BlockSpec((bs, dil, bt // dil, a.shape[3]), lambda i, j: (i, 0, j, 0))

    n_att = (len(att_inputs) + 1) // 2
    toks = [sza, ob, szb, sga, sgb]
    width = att_inputs[0].shape[3]
    return pl.pallas_call(
        functools.partial(_out_kernel, n_att=n_att),
        grid=(s // bs, n_t),
        in_specs=[att_spec(a) for a in att_inputs] + [tok_spec(a) for a in toks]
        + [pl.BlockSpec((bs, bt, d), lambda i, j: (i, j, 0)),
           pl.BlockSpec((bs, 1, d), lambda i, j: (i, 0, 0)),
           _resident(wa_bf.shape), _resident(wb_bf.shape), _resident(wo_bf.shape)],
        out_specs=pl.BlockSpec((bs, bt, d), lambda i, j: (i, j, 0)),
        out_shape=jax.ShapeDtypeStruct((s, t, d), F32),
        scratch_shapes=[pltpu.VMEM((width // LANES, tm, LANES), F32)],
        compiler_params=pltpu.CompilerParams(dimension_semantics=("arbitrary", "arbitrary"),
                                             vmem_limit_bytes=V7X_VMEM_LIMIT),
        name="out",
    )(*att_inputs, *toks, x, gate, wa_bf, wb_bf, wo_bf)


def _kvwin_kernel(k_ref, v_ref, o_ref, scr):
    o_ref[0, 0] = _load_tokens(k_ref, scr).T
    o_ref[0, 1] = _load_tokens(v_ref, scr).T


def _kvwin_call(k, v, w):
    n, dil, sub, width = k.shape
    blk = min(w, PROJ_TOKENS)
    assert w % blk == 0 and (sub * dil - w) % blk == 0 and blk % dil == 0
    first = (sub * dil - w) // blk
    src = pl.BlockSpec((1, dil, blk // dil, width), lambda b, i: (b, 0, first + i, 0))
    return pl.pallas_call(
        _kvwin_kernel,
        grid=(n, w // blk),
        in_specs=[src, src],
        out_specs=pl.BlockSpec((1, 2, width, blk), lambda b, i: (b, 0, 0, i)),
        out_shape=jax.ShapeDtypeStruct((n, 2, width, w), F32),
        scratch_shapes=[pltpu.VMEM((width // LANES, blk, LANES), F32)],
        compiler_params=pltpu.CompilerParams(dimension_semantics=("arbitrary", "arbitrary")),
        name="kvwin",
    )(k, v)


def _kvnew_kernel(*refs):
    n_g = len(refs) // 3
    for g in range(n_g):
        for kv in range(2):
            src = refs[kv * n_g + g]
            for t in range(src.shape[1]):
                refs[2 * n_g + g][t, kv] = src[:, t, :].T


def _kvnew_call(ks, vs):
    n, t, width = ks[0].shape
    return pl.pallas_call(
        _kvnew_kernel,
        out_shape=[jax.ShapeDtypeStruct((t, 2, width, n), F32) for _ in ks],
        compiler_params=pltpu.CompilerParams(vmem_limit_bytes=V7X_VMEM_LIMIT),
        name="kvnew",
    )(*ks, *vs)


def kernel(x_prompt, x_sample, cache_kv_g0, cache_kv_g1, cache_kv_g2, state_hgrn, c_prompt, c_sample, norm_gain, w_ada, b_ada, w_in, q_norm_gain, k_norm_gain, hgrn_lb_logits, hgrn_out_norm_gain, w_branch_a, w_branch_b, w_out):
    depth = w_in.shape[0]
    assert depth == 1, "single-layer step"
    n_p, t_p, d = x_prompt.shape
    n_s, t_s, _ = x_sample.shape
    a_out = w_branch_a.shape[1]
    n_heads = a_out // HEAD_DIM
    b_f = w_branch_b.shape[1]
    b_heads = b_f // HGRN_HEAD
    slopes = _alibi_slopes(n_heads)

    n_c = n_p + n_s
    pad = (-n_c) % 8
    c_all = jnp.concatenate([c_prompt, c_sample, jnp.zeros((pad, d), F32)], axis=0)
    ada = _ada_call(c_all, w_ada[0], b_ada[0])
    shift, scale, gate = ada[:, :d], ada[:, d:2 * d], ada[:, 2 * d:]

    w_in_bf = w_in[0].astype(BF16)
    wa_bf, wb_bf, wo_bf = w_branch_a[0].astype(BF16), w_branch_b[0].astype(BF16), w_out[0].astype(BF16)
    head_of = np.arange(MXU_WIDTH) // HEAD_DIM
    bd = jnp.asarray(head_of[:, None] == head_of[None, :], BF16)
    qg = (jnp.tile(q_norm_gain[0], n_heads) * (ATTN_SCALE * LOG2E)).reshape(1, a_out)
    kg = jnp.tile(k_norm_gain[0], n_heads).reshape(1, a_out)
    og = hgrn_out_norm_gain[0]

    def cond(lo, n):
        return [a[lo:lo + n].reshape(n, 1, d) for a in (shift, scale, gate)]

    sh, sc, gt = cond(0, n_p)
    outs = _proj_call(x_prompt, sh, sc, norm_gain[0], w_in_bf, bd, qg, kg, hgrn_lb_logits, n_groups=N_GROUPS,
                      dils=DILATIONS, a_out=a_out, b_f=b_f, kv_dtype=BF16)
    qs, ks, vs = outs[0:N_GROUPS], outs[N_GROUPS:2 * N_GROUPS], outs[2 * N_GROUPS:3 * N_GROUPS]
    sza, sqb, g, kf, ib, szb, sga, sgb = outs[3 * N_GROUPS:]
    att = [_attn_call(qs[gi], ks[gi], vs[gi], gi, n_heads, slopes[gi]) for gi in range(N_GROUPS)]
    ob, hgrn_p = _hgrn_prompt_call(sqb, g, kf, ib, og, n_p, t_p, b_heads)
    y_p = _out_call([a[0] for a in att] + [a[1] for a in att], sza, ob, szb, sga, sgb, x_prompt, gt,
                    wa_bf, wb_bf, wo_bf)
    kv_p = []
    for gi in range(N_GROUPS):
        w = min(WINDOWS[gi], t_p)
        rows = _kvwin_call(ks[gi], vs[gi], w).reshape(n_p, 2, n_heads, HEAD_DIM, w)
        kv_p.append(jnp.transpose(rows, (0, 4, 1, 2, 3))[None])

    sh, sc, gt = cond(n_p, n_s)
    outs = _proj_call(x_sample, sh, sc, norm_gain[0], w_in_bf, bd, qg, kg, hgrn_lb_logits, n_groups=N_GROUPS,
                      dils=(1,) * N_GROUPS, a_out=a_out, b_f=b_f, kv_dtype=F32)
    qs, ks, vs = [[a.reshape(n_s, t_s, a_out) for a in outs[i * N_GROUPS:(i + 1) * N_GROUPS]] for i in range(3)]
    sza, sqb, g, kf, ib, szb, sga, sgb = outs[3 * N_GROUPS:]
    o_a = _sattn_call(qs, ks, vs, (cache_kv_g0[0], cache_kv_g1[0], cache_kv_g2[0]), n_heads, slopes)
    ob, hgrn_s = _hgrn_sample_call(sqb, g, kf, ib, state_hgrn[0], og, n_s, t_s, b_heads)
    y_s = _out_call([o_a.reshape(n_s, 1, t_s, a_out)], sza, ob, szb, sga, sgb, x_sample, gt, wa_bf, wb_bf, wo_bf)
    kv_s = [jnp.transpose(a.reshape(t_s, 2, n_heads, HEAD_DIM, n_s), (4, 0, 1, 2, 3))[None]
            for a in _kvnew_call(ks, vs)]

    return (y_p, y_s, kv_p[0], kv_p[1], kv_p[2], hgrn_p[None], kv_s[0], kv_s[1], kv_s[2], hgrn_s[None])
```

```python
import functools

import numpy as np
import jax
import jax.numpy as jnp
from jax import lax
from jax.experimental import pallas as pl
from jax.experimental.pallas import tpu as pltpu

F32 = jnp.float32
BF16 = jnp.bfloat16

HEAD_DIM = 64
N_GROUPS = 3
WINDOWS = (128, 512, 2048)
DILATIONS = (1, 4, 16)
N_BACK = 128
ATTN_BLOCK = 128
ATTN_BLOCKS_PER_STEP = 4
ATTN_SCALE = HEAD_DIM ** -0.5
HGRN_HEAD = 128
LANES = 128
MXU_WIDTH = 256
LOG2E = 1.4426950408889634
EPS = 1e-6
NEG = -1e30

V7X_VMEM_LIMIT = 56 * 1024 * 1024
PROJ_TOKENS = 512
HGRN_CHUNK = 64
HGRN_STEP_TOKENS = 512
HGRN_MAX_SEQS = 4
HGRN_SAMPLE_SEQS = 8


def _sigmoid(x):
    return 1.0 / (1.0 + jnp.exp(-x))


def _silu(x):
    return x * _sigmoid(x)


def _dot(a, b):
    return jnp.dot(a, b, preferred_element_type=F32)


def _dot_nt(a, b):
    return lax.dot_general(a, b, (((1,), (1,)), ((), ())), preferred_element_type=F32)


def _dot_tn(a, b):
    return lax.dot_general(a, b, (((0,), (0,)), ((), ())), preferred_element_type=F32)


def _alibi_slopes(n_heads_per_group):
    a_heads = N_GROUPS * n_heads_per_group
    s = 2.0 ** (-8.0 * (np.arange(a_heads) + 1) / a_heads)
    return s.astype(np.float32).reshape(N_GROUPS, n_heads_per_group)


def _resident(shape):
    nd = len(shape)
    return pl.BlockSpec(shape, lambda *_: (0,) * nd, pipeline_mode=pl.Buffered(1))


def _ada_kernel(c_ref, w_ref, b_ref, o_ref):
    o_ref[...] = _dot(_silu(c_ref[...]).astype(BF16), w_ref[...]) + b_ref[...]


def _ada_call(c, w_ada, b_ada):
    n, d = c.shape
    cols = w_ada.shape[1]
    blk = d
    return pl.pallas_call(
        _ada_kernel,
        grid=(cols // blk,),
        in_specs=[pl.BlockSpec((n, d), lambda j: (0, 0)),
                  pl.BlockSpec((d, blk), lambda j: (0, j)),
                  pl.BlockSpec((1, blk), lambda j: (0, j))],
        out_specs=pl.BlockSpec((n, blk), lambda j: (0, j)),
        out_shape=jax.ShapeDtypeStruct((n, cols), F32),
        name="ada",
    )(c, w_ada.astype(BF16), b_ada.reshape(1, cols))


def _store_dilated(out_ref, val, scr, dil):
    bs, d, rows, width = out_ref.shape
    if dil == 1:
        out_ref[...] = val.reshape(bs, 1, rows, width).astype(out_ref.dtype)
        return
    for c in range(width // LANES):
        scr[c] = val[:, c * LANES:(c + 1) * LANES]
    for r in range(dil):
        for c in range(width // LANES):
            out_ref[0, r, :, c * LANES:(c + 1) * LANES] = scr[c, pl.ds(r, rows, stride=dil), :].astype(out_ref.dtype)


def _proj_kernel(x_ref, shift_ref, scale_ref, ng_ref, w_ref, bd_ref, qg_ref, kg_ref, lbl_ref, *refs,
                 n_groups, dils, a_out, d_model):
    qkv_refs = refs[:3 * n_groups]
    sza_ref, sqb_ref, g_ref, kf_ref, ib_ref, szb_ref, sga_ref, sgb_ref, scr = refs[3 * n_groups:]
    x = x_ref[...]
    ms = jnp.mean(x * x, axis=-1, keepdims=True)
    h = x * lax.rsqrt(ms + EPS) * ng_ref[...]
    h = h * (1.0 + scale_ref[...]) + shift_ref[...]
    tm = h.shape[0] * h.shape[1]
    hb = h.reshape(tm, d_model).astype(BF16)
    tile = a_out

    def proj(c0):
        return _dot(hb, w_ref[:, c0:c0 + tile])

    def head_norm(u, gain):
        u2 = (u * u).astype(BF16)
        nb = bd_ref.shape[0]
        ss = jnp.concatenate([_dot(u2[:, c:c + nb], bd_ref[...]) for c in range(0, tile, nb)], axis=1)
        return u * lax.rsqrt(ss * (1.0 / HEAD_DIM) + EPS) * gain

    c0 = 0
    for kind in range(3):
        for g in range(n_groups):
            u = proj(c0)
            if kind == 0:
                u = head_norm(u, qg_ref[...])
            elif kind == 1:
                u = head_norm(u, kg_ref[...])
            _store_dilated(qkv_refs[kind * n_groups + g], u, scr, dils[g])
            c0 += tile
    sza_ref[...] = _silu(proj(c0)).astype(BF16)
    c0 += tile
    sqb_ref[...] = _silu(proj(c0)).astype(BF16)
    c0 += tile
    lbl = lbl_ref[...]
    lbe = jnp.exp(lbl - jnp.max(lbl, axis=0, keepdims=True))
    lb = lbe[0:1] / jnp.sum(lbe, axis=0, keepdims=True)
    fr = proj(c0)
    g_ref[...] = jnp.log2(lb + (1.0 - lb) * _sigmoid(fr))
    kf_ref[...] = ((1.0 - lb) * _sigmoid(-fr)).astype(BF16)
    c0 += tile
    ib_ref[...] = proj(c0).astype(BF16)
    c0 += tile
    szb_ref[...] = _silu(proj(c0)).astype(BF16)
    c0 += tile
    for j in range(d_model // tile):
        sga_ref[:, j * tile:(j + 1) * tile] = _sigmoid(proj(c0)).astype(BF16)
        c0 += tile
    for j in range(d_model // tile):
        sgb_ref[:, j * tile:(j + 1) * tile] = _sigmoid(proj(c0)).astype(BF16)
        c0 += tile


def _token_tiling(s, t):
    if t >= PROJ_TOKENS:
        assert t % PROJ_TOKENS == 0
        return 1, PROJ_TOKENS
    bs = min(s, PROJ_TOKENS // t)
    assert s % bs == 0 and t % 8 == 0
    return bs, t


def _proj_call(x, shift, scale, norm_gain, w_in_bf, bd, qg, kg, lb_logits, *, n_groups, dils, a_out, b_f, kv_dtype):
    s, t, d = x.shape
    bs, bt = _token_tiling(s, t)
    tm = bs * bt
    n_t = t // bt
    ntok = s * t
    grid = (s // bs, n_t)
    assert all(dl == 1 or (bs == 1 and bt % (dl * 16) == 0) for dl in dils)

    def tok_spec(cols):
        return pl.BlockSpec((tm, cols), lambda i, j: (i * n_t + j, 0))

    qkv_dtypes = [BF16] * n_groups + [kv_dtype] * (2 * n_groups)
    qkv_shapes = [jax.ShapeDtypeStruct((s, dils[g], t // dils[g], a_out), dt)
                  for g, dt in zip(list(range(n_groups)) * 3, qkv_dtypes)]
    qkv_specs = [pl.BlockSpec((bs, dils[g], bt // dils[g], a_out), lambda i, j: (i, 0, j, 0))
                 for g in list(range(n_groups)) * 3]
    out_cols = [(a_out, BF16), (b_f, BF16), (b_f, F32), (b_f, BF16), (b_f, BF16), (b_f, BF16), (d, BF16), (d, BF16)]
    kern = functools.partial(_proj_kernel, n_groups=n_groups, dils=tuple(dils), a_out=a_out, d_model=d)
    return pl.pallas_call(
        kern,
        grid=grid,
        in_specs=[pl.BlockSpec((bs, bt, d), lambda i, j: (i, j, 0)),
                  pl.BlockSpec((bs, 1, d), lambda i, j: (i, 0, 0)),
                  pl.BlockSpec((bs, 1, d), lambda i, j: (i, 0, 0)),
                  _resident((1, d)),
                  _resident(w_in_bf.shape),
                  _resident(bd.shape),
                  _resident(qg.shape),
                  _resident(kg.shape),
                  _resident(lb_logits.shape)],
        out_specs=qkv_specs + [tok_spec(c) for c, _ in out_cols],
        out_shape=qkv_shapes + [jax.ShapeDtypeStruct((ntok, c), dt) for c, dt in out_cols],
        scratch_shapes=[pltpu.VMEM((a_out // LANES, tm, LANES), F32)],
        compiler_params=pltpu.CompilerParams(dimension_semantics=("arbitrary", "arbitrary"),
                                             vmem_limit_bytes=V7X_VMEM_LIMIT),
        name="proj",
    )(x, shift, scale, norm_gain.reshape(1, d), w_in_bf, bd, qg, kg, lb_logits)


def _attn_kernel(q_ref, kp_ref, kc_ref, vp_ref, vc_ref, bias_ref, o_ref, lse_ref, *, n_heads, n_blk):
    first = (pl.program_id(2) == 0).astype(jnp.int32)
    lane = lax.broadcasted_iota(jnp.int32, (ATTN_BLOCK, 2 * HEAD_DIM), 1)
    low = lane < HEAD_DIM
    for blk in range(n_blk):
        rows = slice(blk * ATTN_BLOCK, (blk + 1) * ATTN_BLOCK)
        before = slice((blk - 1) * ATTN_BLOCK, blk * ATTN_BLOCK)
        table = first if blk == 0 else 0
        scores, values = [], []
        for hp in range(n_heads // 2):
            sl = slice(hp * 2 * HEAD_DIM, (hp + 1) * 2 * HEAD_DIM)
            q2 = q_ref[rows, sl]
            k_prev = kp_ref[:, sl] if blk == 0 else kc_ref[before, sl]
            v_prev = vp_ref[:, sl] if blk == 0 else vc_ref[before, sl]
            k2 = jnp.concatenate([k_prev, kc_ref[rows, sl]], axis=0)
            values.append(jnp.concatenate([v_prev, vc_ref[rows, sl]], axis=0))
            for par in range(2):
                keep = low if par == 0 else jnp.logical_not(low)
                qm = jnp.where(keep, q2, jnp.zeros_like(q2))
                scores.append(_dot_nt(qm, k2) + bias_ref[table, 2 * hp + par])
        soft = []
        for s in scores:
            m = jnp.max(s, axis=-1, keepdims=True)
            p = jnp.exp2(s - m)
            soft.append((p.astype(BF16), m, jnp.sum(p, axis=-1, keepdims=True)))
        outs = [_dot(p, values[i // 2]) for i, (p, m, l) in enumerate(soft)]
        for hp in range(n_heads // 2):
            sl = slice(hp * 2 * HEAD_DIM, (hp + 1) * 2 * HEAD_DIM)
            (_, m0, l0), (_, m1, l1) = soft[2 * hp], soft[2 * hp + 1]
            o_ref[rows, sl] = jnp.where(low, outs[2 * hp] / l0, outs[2 * hp + 1] / l1).astype(o_ref.dtype)
            lse_ref[rows, sl] = jnp.where(low, m0 + jnp.log2(l0), m1 + jnp.log2(l1))


def _attn_tables(dil, slopes_g):
    i = np.arange(ATTN_BLOCK)[:, None]
    j = np.arange(2 * ATTN_BLOCK)[None, :]
    delta = i + ATTN_BLOCK - j
    valid = (delta >= 0) & (delta <= N_BACK)
    dist = (delta * dil).astype(np.float32)
    bias = np.where(valid[None], -slopes_g[:, None, None] * dist[None] * LOG2E, NEG).astype(np.float32)
    first = np.where((j >= ATTN_BLOCK)[None], bias, NEG).astype(np.float32)
    return jnp.asarray(np.stack([bias, first]))


def _attn_call(q, k, v, g, n_heads, slopes_g):
    n, dil, sub, width = q.shape
    assert dil == DILATIONS[g] and sub % ATTN_BLOCK == 0 and width == n_heads * HEAD_DIM
    nb = sub // ATTN_BLOCK
    n_blk = min(ATTN_BLOCKS_PER_STEP, nb)
    assert nb % n_blk == 0
    cur = pl.BlockSpec((None, None, n_blk * ATTN_BLOCK, width), lambda b, r, i: (b, r, i, 0))
    prev = pl.BlockSpec((None, None, ATTN_BLOCK, width), lambda b, r, i: (b, r, jnp.maximum(i * n_blk - 1, 0), 0))
    bias = _attn_tables(dil, slopes_g)
    return pl.pallas_call(
        functools.partial(_attn_kernel, n_heads=n_heads, n_blk=n_blk),
        grid=(n, dil, nb // n_blk),
        in_specs=[cur, prev, cur, prev, cur, _resident(bias.shape)],
        out_specs=[cur, cur],
        out_shape=[jax.ShapeDtypeStruct(q.shape, BF16), jax.ShapeDtypeStruct(q.shape, F32)],
        compiler_params=pltpu.CompilerParams(dimension_semantics=("arbitrary", "arbitrary", "arbitrary"),
                                             vmem_limit_bytes=V7X_VMEM_LIMIT),
        name=f"attn_g{g}",
    )(q, k, k, v, v, bias)


def _sattn_kernel(*refs, n_heads):
    qs, ks, vs = refs[0:N_GROUPS], refs[N_GROUPS:2 * N_GROUPS], refs[2 * N_GROUPS:3 * N_GROUPS]
    caches = refs[3 * N_GROUPS:4 * N_GROUPS]
    biases = refs[4 * N_GROUPS:5 * N_GROUPS]
    bn_ref, o_ref = refs[5 * N_GROUPS:]
    for h in range(n_heads):
        scores, values = [], []
        hs = slice(h * HEAD_DIM, (h + 1) * HEAD_DIM)
        for g in range(N_GROUPS):
            qh = qs[g][0, :, hs]
            scores.append(_dot(qh, caches[g][0, h].astype(BF16)) + biases[g][h])
            values.append(caches[g][1, h].astype(BF16))
            scores.append(_dot_nt(qh, ks[g][0, :, hs].astype(BF16)) + bn_ref[g, h])
            values.append(vs[g][0, :, hs].astype(BF16))
        m = functools.reduce(jnp.maximum, [jnp.max(s, axis=-1, keepdims=True) for s in scores])
        ps = [jnp.exp2(s - m) for s in scores]
        l = functools.reduce(lambda a, b: a + b, [jnp.sum(p, axis=-1, keepdims=True) for p in ps])
        acc = jnp.zeros((qh.shape[0], HEAD_DIM), F32)
        for i, (p, v) in enumerate(zip(ps, values)):
            pb = p.astype(BF16)
            acc = acc + (_dot_nt(pb, v) if i % 2 == 0 else _dot(pb, v))
        o_ref[0, :, h * HEAD_DIM:(h + 1) * HEAD_DIM] = (acc / l).astype(o_ref.dtype)


def _sattn_tables(n_heads, t_new, slopes, wbs):
    t = np.arange(t_new)[:, None]
    biases, bn = [], []
    for g in range(N_GROUPS):
        d, wb = DILATIONS[g], wbs[g]
        dist = wb + t - np.arange(wb)[None, :]
        valid = (dist % d == 0) & (dist // d <= N_BACK)
        biases.append(np.where(valid[None], -slopes[g][:, None, None] * dist[None] * LOG2E, NEG).astype(np.float32))
        dist = t - np.arange(t_new)[None, :]
        valid = (dist >= 0) & (dist % d == 0) & (dist // d <= N_BACK)
        bn.append(np.where(valid[None], -slopes[g][:, None, None] * dist[None] * LOG2E, NEG).astype(np.float32))
    return [jnp.asarray(b) for b in biases], jnp.asarray(np.stack(bn))


def _sattn_call(qs, ks, vs, caches, n_heads, slopes):
    n, t_new, width = qs[0].shape
    wbs = [c.shape[1] for c in caches]
    biases, bn = _sattn_tables(n_heads, t_new, slopes, wbs)
    views = [jnp.transpose(c, (0, 2, 3, 4, 1)) for c in caches]
    new = pl.BlockSpec((1, t_new, width), lambda i: (i, 0, 0))
    return pl.pallas_call(
        functools.partial(_sattn_kernel, n_heads=n_heads),
        grid=(n,),
        in_specs=[new] * (3 * N_GROUPS)
        + [pl.BlockSpec((None, 2, n_heads, HEAD_DIM, wb), lambda i: (i, 0, 0, 0, 0)) for wb in wbs]
        + [_resident(b.shape) for b in biases] + [_resident(bn.shape)],
        out_specs=new,
        out_shape=jax.ShapeDtypeStruct((n, t_new, width), BF16),
        compiler_params=pltpu.CompilerParams(dimension_semantics=("arbitrary",), vmem_limit_bytes=V7X_VMEM_LIMIT),
        name="sattn",
    )(*qs, *ks, *vs, *views, *biases, bn)


def _hgrn_tables(c):
    t = np.arange(c)
    tri = (t[None, :] <= t[:, None]).astype(np.float32)
    masks = [np.eye(c, dtype=bool)]
    signs = []
    w = c // 2
    while w >= 1:
        blk = t // (2 * w)
        late = (t // w) % 2 == 1
        masks.append((blk[:, None] == blk[None, :]) & late[:, None] & ~late[None, :])
        if w < 8:
            signs.append(np.where(late, 1.0, -1.0))
        w //= 2
    masks = np.stack(masks).astype(np.float32)
    signs = np.repeat(np.stack(signs).reshape(-1, 1), HGRN_HEAD, axis=1).astype(np.float32)
    return jnp.asarray(tri, BF16), jnp.asarray(masks), jnp.asarray(signs)


def _level_exponents(b, g, c):
    width = b.shape[1]
    coarse, fine = [], []
    w = c // 2
    while w >= 8:
        parts = []
        for k in range(c // (2 * w)):
            lo, mid, hi = k * 2 * w, k * 2 * w + w, (k + 1) * 2 * w
            ref = b[mid - 1:mid]
            parts.append((slice(lo, mid), False, ref - b[lo:mid]))
            parts.append((slice(mid, hi), True, b[mid:hi] - ref))
        coarse.append(parts)
        w //= 2
    b3 = b.reshape(c // 8, 8, width)
    sub = lax.broadcasted_iota(jnp.int32, b3.shape, 1)
    if c >= 8:
        fine.append((b3 - jnp.broadcast_to(b3[:, 3:4], b3.shape)).reshape(c, width))
    ref = jnp.where(sub < 4, jnp.broadcast_to(b3[:, 1:2], b3.shape), jnp.broadcast_to(b3[:, 5:6], b3.shape))
    fine.append((b3 - ref).reshape(c, width))
    row = lax.broadcasted_iota(jnp.int32, b.shape, 0)
    fine.append(jnp.where(row % 2 == 1, g, 0.0))
    return coarse, fine


def _hgrn_chunks(seqs, tri, masks, signs, og, c):
    n_lev = masks.shape[0] - 1
    tri_v = tri[...]
    og = og[...]
    pre = []
    for sq, g, kf, iv, state in seqs:
        g1 = g.astype(BF16)
        r1 = g - g1.astype(F32)
        g2 = r1.astype(BF16)
        g3 = (r1 - g2.astype(F32)).astype(BF16)
        b = _dot(tri_v, g1) + _dot(tri_v, g2) + _dot(tri_v, g3)
        pre.append((b, _level_exponents(b, g, c)))
    heads = []
    for (sq, g, kf, iv, state), (b, (coarse, fine)) in zip(seqs, pre):
        e_cum = jnp.exp2(b)
        e_tail = jnp.exp2(b[c - 1:c] - b)
        for h in range(len(state)):
            hs = slice(h * HGRN_HEAD, (h + 1) * HGRN_HEAD)
            sqh, kfh, ivh = sq[:, hs], kf[:, hs], iv[:, hs]
            sqf, kff = sqh.astype(F32), kfh.astype(F32)
            eb = e_cum[:, hs]
            prods = [_dot_nt(sqh, kfh)]
            for parts in coarse:
                x = jnp.concatenate([(sqf if late else kff)[rows] * jnp.exp2(e[:, hs]) for rows, late, e in parts],
                                    axis=0).astype(BF16)
                prods.append(_dot_nt(x, x))
            for lv, e in enumerate(fine):
                sgn = signs[lv * c:(lv + 1) * c]
                x = (jnp.where(sgn > 0.0, sqf, kff) * jnp.exp2(e[:, hs] * sgn)).astype(BF16)
                prods.append(_dot_nt(x, x))
            o_inter = _dot((sqf * eb).astype(BF16), state[h].astype(BF16))
            kt = (kff * e_tail[:, hs]).astype(BF16)
            heads.append((prods, o_inter, _dot_tn(kt, ivh), eb, ivh, state[h]))
    results = []
    for prods, o_inter, upd, eb, ivh, st in heads:
        att = masks[0] * prods[0]
        for lv in range(n_lev):
            att = att + masks[1 + lv] * prods[1 + lv]
        o = o_inter + _dot(att.astype(BF16), ivh)
        decay = jnp.broadcast_to(eb[c - 1:c, :], (HGRN_HEAD, HGRN_HEAD)).T
        results.append((o * lax.rsqrt(jnp.mean(o * o, axis=-1, keepdims=True) + EPS) * og, decay * st + upd))
    n_h = len(seqs[0][4])
    return [([r[0] for r in results[i * n_h:(i + 1) * n_h]], [r[1] for r in results[i * n_h:(i + 1) * n_h]])
            for i in range(len(seqs))]


def _hgrn_prompt_kernel(sq_ref, g_ref, kf_ref, iv_ref, tri_ref, masks_ref, signs_ref, og_ref, o_ref, s_ref,
                        *, c, n_heads):
    @pl.when(pl.program_id(0) == 0)
    def _():
        s_ref[...] = jnp.zeros_like(s_ref)

    def body(ci, carry):
        rows = pl.ds(pl.multiple_of(ci * c, c), c)
        seqs = [(sq_ref[b, rows, :], g_ref[b, rows, :], kf_ref[b, rows, :], iv_ref[b, rows, :],
                 [s_ref[b, h] for h in range(n_heads)]) for b in range(sq_ref.shape[0])]
        res = _hgrn_chunks(seqs, tri_ref, masks_ref, signs_ref, og_ref, c)
        for b, (outs, new_state) in enumerate(res):
            for h in range(n_heads):
                s_ref[b, h] = new_state[h]
                o_ref[b, rows, h * HGRN_HEAD:(h + 1) * HGRN_HEAD] = outs[h].astype(o_ref.dtype)
        return carry

    lax.fori_loop(0, sq_ref.shape[1] // c, body, 0)


def _hgrn_prompt_call(sq, g, kf, iv, og, n, t, n_heads):
    width = n_heads * HGRN_HEAD
    c = min(HGRN_CHUNK, t)
    tt = min(HGRN_STEP_TOKENS, t)
    assert t % tt == 0 and tt % c == 0 and n <= HGRN_MAX_SEQS
    tri, masks, signs = _hgrn_tables(c)
    tok = pl.BlockSpec((n, tt, width), lambda i: (0, i, 0))
    o, s = pl.pallas_call(
        functools.partial(_hgrn_prompt_kernel, c=c, n_heads=n_heads),
        grid=(t // tt,),
        in_specs=[tok, tok, tok, tok, _resident(tri.shape), _resident(masks.shape), _resident(signs.shape),
                  _resident((1, HGRN_HEAD))],
        out_specs=[tok, pl.BlockSpec((n, n_heads, HGRN_HEAD, HGRN_HEAD), lambda i: (0, 0, 0, 0))],
        out_shape=[jax.ShapeDtypeStruct((n, t, width), BF16),
                   jax.ShapeDtypeStruct((n, n_heads, HGRN_HEAD, HGRN_HEAD), F32)],
        compiler_params=pltpu.CompilerParams(dimension_semantics=("arbitrary",), vmem_limit_bytes=V7X_VMEM_LIMIT),
        name="hgrn_prompt",
    )(sq.reshape(n, t, width), g.reshape(n, t, width), kf.reshape(n, t, width), iv.reshape(n, t, width),
      tri, masks, signs, og.reshape(1, HGRN_HEAD))
    return o.reshape(n * t, width), s


def _hgrn_sample_kernel(sq_ref, g_ref, kf_ref, iv_ref, s0_ref, tri_ref, masks_ref, signs_ref, og_ref, o_ref, s_ref,
                        *, c, n_heads):
    seqs = [(sq_ref[b], g_ref[b], kf_ref[b], iv_ref[b], [s0_ref[b, h] for h in range(n_heads)])
            for b in range(sq_ref.shape[0])]
    res = _hgrn_chunks(seqs, tri_ref, masks_ref, signs_ref, og_ref, c)
    for b, (outs, new_state) in enumerate(res):
        for h in range(n_heads):
            s_ref[b, h] = new_state[h]
            o_ref[b, :, h * HGRN_HEAD:(h + 1) * HGRN_HEAD] = outs[h].astype(o_ref.dtype)


def _hgrn_sample_call(sq, g, kf, iv, s0, og, n, t, n_heads):
    width = n_heads * HGRN_HEAD
    bn = min(HGRN_SAMPLE_SEQS, n)
    assert n % bn == 0 and t % 8 == 0
    tri, masks, signs = _hgrn_tables(t)
    tok = pl.BlockSpec((bn, t, width), lambda i: (i, 0, 0))
    st = pl.BlockSpec((bn, n_heads, HGRN_HEAD, HGRN_HEAD), lambda i: (i, 0, 0, 0))
    o, s = pl.pallas_call(
        functools.partial(_hgrn_sample_kernel, c=t, n_heads=n_heads),
        grid=(n // bn,),
        in_specs=[tok, tok, tok, tok, st, _resident(tri.shape), _resident(masks.shape), _resident(signs.shape),
                  _resident((1, HGRN_HEAD))],
        out_specs=[tok, st],
        out_shape=[jax.ShapeDtypeStruct((n, t, width), BF16),
                   jax.ShapeDtypeStruct((n, n_heads, HGRN_HEAD, HGRN_HEAD), F32)],
        compiler_params=pltpu.CompilerParams(dimension_semantics=("arbitrary",), vmem_limit_bytes=V7X_VMEM_LIMIT),
        name="hgrn_sample",
    )(sq.reshape(n, t, width), g.reshape(n, t, width), kf.reshape(n, t, width), iv.reshape(n, t, width),
      s0, tri, masks, signs, og.reshape(1, HGRN_HEAD))
    return o.reshape(n * t, width), s


def _load_tokens(ref, scr):
    bs, dil, rows, width = ref.shape
    if dil == 1:
        return ref[...].reshape(bs * rows, width).astype(F32)
    for r in range(dil):
        for c in range(width // LANES):
            scr[c, pl.ds(r, rows, stride=dil), :] = ref[0, r, :, c * LANES:(c + 1) * LANES].astype(F32)
    return jnp.concatenate([scr[c] for c in range(width // LANES)], axis=1)


def _out_kernel(*refs, n_att):
    att = refs[:2 * n_att] if n_att > 1 else refs[:1]
    n_in = len(att)
    sza_ref, ob_ref, szb_ref, sga_ref, sgb_ref, x_ref, gate_ref, wa_ref, wb_ref, wo_ref, y_ref, scr = refs[n_in:]
    if n_att > 1:
        os_ = [_load_tokens(r, scr) for r in att[:n_att]]
        ls_ = [_load_tokens(r, scr) for r in att[n_att:]]
        m = functools.reduce(jnp.maximum, ls_)
        es = [jnp.exp2(l - m) for l in ls_]
        den = functools.reduce(lambda a, b: a + b, es)
        num = functools.reduce(lambda a, b: a + b, [e * o for e, o in zip(es, os_)])
        o_a = num / den
    else:
        o_a = _load_tokens(att[0], scr)
    branch_a = _dot((o_a * sza_ref[...].astype(F32)).astype(BF16), wa_ref[...])
    branch_b = _dot((ob_ref[...].astype(F32) * szb_ref[...].astype(F32)).astype(BF16), wb_ref[...])
    merged = sga_ref[...].astype(F32) * branch_a + sgb_ref[...].astype(F32) * branch_b
    upd = _dot(merged.astype(BF16), wo_ref[...])
    x = x_ref[...]
    y_ref[...] = x + gate_ref[...] * upd.reshape(x.shape)


def _out_call(att_inputs, sza, ob, szb, sga, sgb, x, gate, wa_bf, wb_bf, wo_bf):
    s, t, d = x.shape
    bs, bt = _token_tiling(s, t)
    tm = bs * bt
    n_t = t // bt

    def tok_spec(a):
        return pl.BlockSpec((tm, a.shape[1]), lambda i, j: (i * n_t + j, 0))

    def att_spec(a):
        dil = a.shape[1]
        assert dil == 1 or bs == 1
        return pl.BlockSpec((bs, dil, bt // dil, a.shape[3]), lambda i, j: (i, 0, j, 0))

    n_att = (len(att_inputs) + 1) // 2
    toks = [sza, ob, szb, sga, sgb]
    width = att_inputs[0].shape[3]
    return pl.pallas_call(
        functools.partial(_out_kernel, n_att=n_att),
        grid=(s // bs, n_t),
        in_specs=[att_spec(a) for a in att_inputs] + [tok_spec(a) for a in toks]
        + [pl.BlockSpec((bs, bt, d), lambda i, j: (i, j, 0)),
           pl.BlockSpec((bs, 1, d), lambda i, j: (i, 0, 0)),
           _resident(wa_bf.shape), _resident(wb_bf.shape), _resident(wo_bf.shape)],
        out_specs=pl.BlockSpec((bs, bt, d), lambda i, j: (i, j, 0)),
        out_shape=jax.ShapeDtypeStruct((s, t, d), F32),
        scratch_shapes=[pltpu.VMEM((width // LANES, tm, LANES), F32)],
        compiler_params=pltpu.CompilerParams(dimension_semantics=("arbitrary", "arbitrary"),
                                             vmem_limit_bytes=V7X_VMEM_LIMIT),
        name="out",
    )(*att_inputs, *toks, x, gate, wa_bf, wb_bf, wo_bf)


def _kvwin_kernel(k_ref, v_ref, o_ref, scr):
    o_ref[0, 0] = _load_tokens(k_ref, scr).T
    o_ref[0, 1] = _load_tokens(v_ref, scr).T


def _kvwin_call(k, v, w):
    n, dil, sub, width = k.shape
    blk = min(w, PROJ_TOKENS)
    assert w % blk == 0 and (sub * dil - w) % blk == 0 and blk % dil == 0
    first = (sub * dil - w) // blk
    src = pl.BlockSpec((1, dil, blk // dil, width), lambda b, i: (b, 0, first + i, 0))
    return pl.pallas_call(
        _kvwin_kernel,
        grid=(n, w // blk),
        in_specs=[src, src],
        out_specs=pl.BlockSpec((1, 2, width, blk), lambda b, i: (b, 0, 0, i)),
        out_shape=jax.ShapeDtypeStruct((n, 2, width, w), F32),
        scratch_shapes=[pltpu.VMEM((width // LANES, blk, LANES), F32)],
        compiler_params=pltpu.CompilerParams(dimension_semantics=("arbitrary", "arbitrary")),
        name="kvwin",
    )(k, v)


def _kvnew_kernel(*refs):
    n_g = len(refs) // 3
    for g in range(n_g):
        for kv in range(2):
            src = refs[kv * n_g + g]
            for t in range(src.shape[1]):
                refs[2 * n_g + g][t, kv] = src[:, t, :].T


def _kvnew_call(ks, vs):
    n, t, width = ks[0].shape
    return pl.pallas_call(
        _kvnew_kernel,
        out_shape=[jax.ShapeDtypeStruct((t, 2, width, n), F32) for _ in ks],
        compiler_params=pltpu.CompilerParams(vmem_limit_bytes=V7X_VMEM_LIMIT),
        name="kvnew",
    )(*ks, *vs)


def kernel(x_prompt, x_sample, cache_kv_g0, cache_kv_g1, cache_kv_g2, state_hgrn, c_prompt, c_sample, norm_gain, w_ada, b_ada, w_in, q_norm_gain, k_norm_gain, hgrn_lb_logits, hgrn_out_norm_gain, w_branch_a, w_branch_b, w_out):
    depth = w_in.shape[0]
    assert depth == 1, "single-layer step"
    n_p, t_p, d = x_prompt.shape
    n_s, t_s, _ = x_sample.shape
    a_out = w_branch_a.shape[1]
    n_heads = a_out // HEAD_DIM
    b_f = w_branch_b.shape[1]
    b_heads = b_f // HGRN_HEAD
    slopes = _alibi_slopes(n_heads)

    n_c = n_p + n_s
    pad = (-n_c) % 8
    c_all = jnp.concatenate([c_prompt, c_sample, jnp.zeros((pad, d), F32)], axis=0)
    ada = _ada_call(c_all, w_ada[0], b_ada[0])
    shift, scale, gate = ada[:, :d], ada[:, d:2 * d], ada[:, 2 * d:]

    w_in_bf = w_in[0].astype(BF16)
    wa_bf, wb_bf, wo_bf = w_branch_a[0].astype(BF16), w_branch_b[0].astype(BF16), w_out[0].astype(BF16)
    head_of = np.arange(MXU_WIDTH) // HEAD_DIM
    bd = jnp.asarray(head_of[:, None] == head_of[None, :], BF16)
    qg = (jnp.tile(q_norm_gain[0], n_heads) * (ATTN_SCALE * LOG2E)).reshape(1, a_out)
    kg = jnp.tile(k_norm_gain[0], n_heads).reshape(1, a_out)
    og = hgrn_out_norm_gain[0]

    def cond(lo, n):
        return [a[lo:lo + n].reshape(n, 1, d) for a in (shift, scale, gate)]

    sh, sc, gt = cond(0, n_p)
    outs = _proj_call(x_prompt, sh, sc, norm_gain[0], w_in_bf, bd, qg, kg, hgrn_lb_logits, n_groups=N_GROUPS,
                      dils=DILATIONS, a_out=a_out, b_f=b_f, kv_dtype=BF16)
    qs, ks, vs = outs[0:N_GROUPS], outs[N_GROUPS:2 * N_GROUPS], outs[2 * N_GROUPS:3 * N_GROUPS]
    sza, sqb, g, kf, ib, szb, sga, sgb = outs[3 * N_GROUPS:]
    att = [_attn_call(qs[gi], ks[gi], vs[gi], gi, n_heads, slopes[gi]) for gi in range(N_GROUPS)]
    ob, hgrn_p = _hgrn_prompt_call(sqb, g, kf, ib, og, n_p, t_p, b_heads)
    y_p = _out_call([a[0] for a in att] + [a[1] for a in att], sza, ob, szb, sga, sgb, x_prompt, gt,
                    wa_bf, wb_bf, wo_bf)
    kv_p = []
    for gi in range(N_GROUPS):
        w = min(WINDOWS[gi], t_p)
        rows = _kvwin_call(ks[gi], vs[gi], w).reshape(n_p, 2, n_heads, HEAD_DIM, w)
        kv_p.append(jnp.transpose(rows, (0, 4, 1, 2, 3))[None])

    sh, sc, gt = cond(n_p, n_s)
    outs = _proj_call(x_sample, sh, sc, norm_gain[0], w_in_bf, bd, qg, kg, hgrn_lb_logits, n_groups=N_GROUPS,
                      dils=(1,) * N_GROUPS, a_out=a_out, b_f=b_f, kv_dtype=F32)
    qs, ks, vs = [[a.reshape(n_s, t_s, a_out) for a in outs[i * N_GROUPS:(i + 1) * N_GROUPS]] for i in range(3)]
    sza, sqb, g, kf, ib, szb, sga, sgb = outs[3 * N_GROUPS:]
    o_a = _sattn_call(qs, ks, vs, (cache_kv_g0[0], cache_kv_g1[0], cache_kv_g2[0]), n_heads, slopes)
    ob, hgrn_s = _hgrn_sample_call(sqb, g, kf, ib, state_hgrn[0], og, n_s, t_s, b_heads)
    y_s = _out_call([o_a.reshape(n_s, 1, t_s, a_out)], sza, ob, szb, sga, sgb, x_sample, gt, wa_bf, wb_bf, wo_bf)
    kv_s = [jnp.transpose(a.reshape(t_s, 2, n_heads, HEAD_DIM, n_s), (4, 0, 1, 2, 3))[None]
            for a in _kvnew_call(ks, vs)]

    return (y_p, y_s, kv_p[0], kv_p[1], kv_p[2], hgrn_p[None], kv_s[0], kv_s[1], kv_s[2], hgrn_s[None])
```

```python
import functools

import numpy as np
import jax
import jax.numpy as jnp
from jax import lax
from jax.experimental import pallas as pl
from jax.experimental.pallas import tpu as pltpu

F32 = jnp.float32
BF16 = jnp.bfloat16

HEAD_DIM = 64
N_GROUPS = 3
WINDOWS = (128, 512, 2048)
DILATIONS = (1, 4, 16)
N_BACK = 128
ATTN_BLOCK = 128
ATTN_BLOCKS_PER_STEP = 4
ATTN_SCALE = HEAD_DIM ** -0.5
HGRN_HEAD = 128
LANES = 128
MXU_WIDTH = 256
LOG2E = 1.4426950408889634
EPS = 1e-6
NEG = -1e30

V7X_VMEM_LIMIT = 56 * 1024 * 1024
PROJ_TOKENS = 512
HGRN_CHUNK = 64
HGRN_STEP_TOKENS = 512
HGRN_MAX_SEQS = 4
HGRN_SAMPLE_SEQS = 8


def _sigmoid(x):
    return 1.0 / (1.0 + jnp.exp(-x))


def _silu(x):
    return x * _sigmoid(x)


def _dot(a, b):
    return jnp.dot(a, b, preferred_element_type=F32)


def _dot_nt(a, b):
    return lax.dot_general(a, b, (((1,), (1,)), ((), ())), preferred_element_type=F32)


def _dot_tn(a, b):
    return lax.dot_general(a, b, (((0,), (0,)), ((), ())), preferred_element_type=F32)


def _alibi_slopes(n_heads_per_group):
    a_heads = N_GROUPS * n_heads_per_group
    s = 2.0 ** (-8.0 * (np.arange(a_heads) + 1) / a_heads)
    return s.astype(np.float32).reshape(N_GROUPS, n_heads_per_group)


def _resident(shape):
    nd = len(shape)
    return pl.BlockSpec(shape, lambda *_: (0,) * nd, pipeline_mode=pl.Buffered(1))


def _ada_kernel(c_ref, w_ref, b_ref, o_ref):
    o_ref[...] = _dot(_silu(c_ref[...]).astype(BF16), w_ref[...]) + b_ref[...]


def _ada_call(c, w_ada, b_ada):
    n, d = c.shape
    cols = w_ada.shape[1]
    blk = d
    return pl.pallas_call(
        _ada_kernel,
        grid=(cols // blk,),
        in_specs=[pl.BlockSpec((n, d), lambda j: (0, 0)),
                  pl.BlockSpec((d, blk), lambda j: (0, j)),
                  pl.BlockSpec((1, blk), lambda j: (0, j))],
        out_specs=pl.BlockSpec((n, blk), lambda j: (0, j)),
        out_shape=jax.ShapeDtypeStruct((n, cols), F32),
        name="ada",
    )(c, w_ada.astype(BF16), b_ada.reshape(1, cols))


def _store_dilated(out_ref, val, scr, dil):
    bs, d, rows, width = out_ref.shape
    if dil == 1:
        out_ref[...] = val.reshape(bs, 1, rows, width).astype(out_ref.dtype)
        return
    for c in range(width // LANES):
        scr[c] = val[:, c * LANES:(c + 1) * LANES]
    for r in range(dil):
        for c in range(width // LANES):
            out_ref[0, r, :, c * LANES:(c + 1) * LANES] = scr[c, pl.ds(r, rows, stride=dil), :].astype(out_ref.dtype)


def _proj_kernel(x_ref, shift_ref, scale_ref, ng_ref, w_ref, bd_ref, qg_ref, kg_ref, lbl_ref, *refs,
                 n_groups, dils, a_out, d_model):
    qkv_refs = refs[:3 * n_groups]
    sza_ref, sqb_ref, g_ref, kf_ref, ib_ref, szb_ref, sga_ref, sgb_ref, scr = refs[3 * n_groups:]
    x = x_ref[...]
    ms = jnp.mean(x * x, axis=-1, keepdims=True)
    h = x * lax.rsqrt(ms + EPS) * ng_ref[...]
    h = h * (1.0 + scale_ref[...]) + shift_ref[...]
    tm = h.shape[0] * h.shape[1]
    hb = h.reshape(tm, d_model).astype(BF16)
    tile = a_out

    def proj(c0):
        return _dot(hb, w_ref[:, c0:c0 + tile])

    def head_norm(u, gain):
        u2 = (u * u).astype(BF16)
        nb = bd_ref.shape[0]
        ss = jnp.concatenate([_dot(u2[:, c:c + nb], bd_ref[...]) for c in range(0, tile, nb)], axis=1)
        return u * lax.rsqrt(ss * (1.0 / HEAD_DIM) + EPS) * gain

    c0 = 0
    for kind in range(3):
        for g in range(n_groups):
            u = proj(c0)
            if kind == 0:
                u = head_norm(u, qg_ref[...])
            elif kind == 1:
                u = head_norm(u, kg_ref[...])
            _store_dilated(qkv_refs[kind * n_groups + g], u, scr, dils[g])
            c0 += tile
    sza_ref[...] = _silu(proj(c0)).astype(BF16)
    c0 += tile
    sqb_ref[...] = _silu(proj(c0)).astype(BF16)
    c0 += tile
    lbl = lbl_ref[...]
    lbe = jnp.exp(lbl - jnp.max(lbl, axis=0, keepdims=True))
    lb = lbe[0:1] / jnp.sum(lbe, axis=0, keepdims=True)
    fr = proj(c0)
    g_ref[...] = jnp.log2(lb + (1.0 - lb) * _sigmoid(fr))
    kf_ref[...] = ((1.0 - lb) * _sigmoid(-fr)).astype(BF16)
    c0 += tile
    ib_ref[...] = proj(c0).astype(BF16)
    c0 += tile
    szb_ref[...] = _silu(proj(c0)).astype(BF16)
    c0 += tile
    for j in range(d_model // tile):
        sga_ref[:, j * tile:(j + 1) * tile] = _sigmoid(proj(c0)).astype(BF16)
        c0 += tile
    for j in range(d_model // tile):
        sgb_ref[:, j * tile:(j + 1) * tile] = _sigmoid(proj(c0)).astype(BF16)
        c0 += tile


def _token_tiling(s, t):
    if t >= PROJ_TOKENS:
        assert t % PROJ_TOKENS == 0
        return 1, PROJ_TOKENS
    bs = min(s, PROJ_TOKENS // t)
    assert s % bs == 0 and t % 8 == 0
    return bs, t


def _proj_call(x, shift, scale, norm_gain, w_in_bf, bd, qg, kg, lb_logits, *, n_groups, dils, a_out, b_f, kv_dtype):
    s, t, d = x.shape
    bs, bt = _token_tiling(s, t)
    tm = bs * bt
    n_t = t // bt
    ntok = s * t
    grid = (s // bs, n_t)
    assert all(dl == 1 or (bs == 1 and bt % (dl * 16) == 0) for dl in dils)

    def tok_spec(cols):
        return pl.BlockSpec((tm, cols), lambda i, j: (i * n_t + j, 0))

    qkv_dtypes = [BF16] * n_groups + [kv_dtype] * (2 * n_groups)
    qkv_shapes = [jax.ShapeDtypeStruct((s, dils[g], t // dils[g], a_out), dt)
                  for g, dt in zip(list(range(n_groups)) * 3, qkv_dtypes)]
    qkv_specs = [pl.BlockSpec((bs, dils[g], bt // dils[g], a_out), lambda i, j: (i, 0, j, 0))
                 for g in list(range(n_groups)) * 3]
    out_cols = [(a_out, BF16), (b_f, BF16), (b_f, F32), (b_f, BF16), (b_f, BF16), (b_f, BF16), (d, BF16), (d, BF16)]
    kern = functools.partial(_proj_kernel, n_groups=n_groups, dils=tuple(dils), a_out=a_out, d_model=d)
    return pl.pallas_call(
        kern,
        grid=grid,
        in_specs=[pl.BlockSpec((bs, bt, d), lambda i, j: (i, j, 0)),
                  pl.BlockSpec((bs, 1, d), lambda i, j: (i, 0, 0)),
                  pl.BlockSpec((bs, 1, d), lambda i, j: (i, 0, 0)),
                  _resident((1, d)),
                  _resident(w_in_bf.shape),
                  _resident(bd.shape),
                  _resident(qg.shape),
                  _resident(kg.shape),
                  _resident(lb_logits.shape)],
        out_specs=qkv_specs + [tok_spec(c) for c, _ in out_cols],
        out_shape=qkv_shapes + [jax.ShapeDtypeStruct((ntok, c), dt) for c, dt in out_cols],
        scratch_shapes=[pltpu.VMEM((a_out // LANES, tm, LANES), F32)],
        compiler_params=pltpu.CompilerParams(dimension_semantics=("arbitrary", "arbitrary"),
                                             vmem_limit_bytes=V7X_VMEM_LIMIT),
        name="proj",
    )(x, shift, scale, norm_gain.reshape(1, d), w_in_bf, bd, qg, kg, lb_logits)


def _prompt_attn(q_ref, kp_ref, kc_ref, vp_ref, vc_ref, bias_ref, o_ref, lse_ref, *, n_heads, n_blk):
    first = (pl.program_id(2) == 0).astype(jnp.int32)
    lane = lax.broadcasted_iota(jnp.int32, (ATTN_BLOCK, 2 * HEAD_DIM), 1)
    low = lane < HEAD_DIM
    for blk in range(n_blk):
        rows = slice(blk * ATTN_BLOCK, (blk + 1) * ATTN_BLOCK)
        before = slice((blk - 1) * ATTN_BLOCK, blk * ATTN_BLOCK)
        table = first if blk == 0 else 0
        scores, values = [], []
        for hp in range(n_heads // 2):
            sl = slice(hp * 2 * HEAD_DIM, (hp + 1) * 2 * HEAD_DIM)
            q2 = q_ref[rows, sl]
            k_prev = kp_ref[:, sl] if blk == 0 else kc_ref[before, sl]
            v_prev = vp_ref[:, sl] if blk == 0 else vc_ref[before, sl]
            k2 = jnp.concatenate([k_prev, kc_ref[rows, sl]], axis=0)
            values.append(jnp.concatenate([v_prev, vc_ref[rows, sl]], axis=0))
            for par in range(2):
                keep = low if par == 0 else jnp.logical_not(low)
                qm = jnp.where(keep, q2, jnp.zeros_like(q2))
                scores.append(_dot_nt(qm, k2) + bias_ref[table, 2 * hp + par])
        soft = []
        for s in scores:
            m = jnp.max(s, axis=-1, keepdims=True)
            p = jnp.exp2(s - m)
            soft.append((p.astype(BF16), m, jnp.sum(p, axis=-1, keepdims=True)))
        outs = [_dot(p, values[i // 2]) for i, (p, m, l) in enumerate(soft)]
        lse = jnp.zeros((ATTN_BLOCK, LANES), F32)
        for hp in range(n_heads // 2):
            sl = slice(hp * 2 * HEAD_DIM, (hp + 1) * 2 * HEAD_DIM)
            (_, m0, l0), (_, m1, l1) = soft[2 * hp], soft[2 * hp + 1]
            o_ref[rows, sl] = jnp.where(low, outs[2 * hp] / l0, outs[2 * hp + 1] / l1).astype(o_ref.dtype)
            lse = jnp.where(lane == 2 * hp, m0 + jnp.log2(l0), lse)
            lse = jnp.where(lane == 2 * hp + 1, m1 + jnp.log2(l1), lse)
        lse_ref[rows, :] = lse


def _sample_attn(qs, ks, vs, caches, biases, bn_ref, o_ref, seq, *, n_heads):
    per_head = []
    for h in range(n_heads):
        scores, values = [], []
        hs = slice(h * HEAD_DIM, (h + 1) * HEAD_DIM)
        for g in range(N_GROUPS):
            qh = qs[g][seq, :, hs]
            scores.append(_dot(qh, caches[g][seq, 0, h].astype(BF16)) + biases[g][h])
            values.append(caches[g][seq, 1, h].astype(BF16))
            scores.append(_dot_nt(qh, ks[g][seq, :, hs].astype(BF16)) + bn_ref[g, h])
            values.append(vs[g][seq, :, hs].astype(BF16))
        per_head.append((scores, values))
    soft = []
    for scores, values in per_head:
        m = functools.reduce(jnp.maximum, [jnp.max(s, axis=-1, keepdims=True) for s in scores])
        ps = [jnp.exp2(s - m) for s in scores]
        l = functools.reduce(lambda a, b: a + b, [jnp.sum(p, axis=-1, keepdims=True) for p in ps])
        soft.append(([p.astype(BF16) for p in ps], l))
    for h, ((ps, l), (_, values)) in enumerate(zip(soft, per_head)):
        parts = [(_dot_nt(p, v) if i % 2 == 0 else _dot(p, v)) for i, (p, v) in enumerate(zip(ps, values))]
        acc = functools.reduce(lambda a, b: a + b, parts)
        o_ref[seq, :, h * HEAD_DIM:(h + 1) * HEAD_DIM] = (acc / l).astype(o_ref.dtype)


def _attn_kernel(*refs, n_heads, n_blk, n_seq):
    prompt_in, refs = refs[:6], refs[6:]
    qs, ks, vs = refs[0:N_GROUPS], refs[N_GROUPS:2 * N_GROUPS], refs[2 * N_GROUPS:3 * N_GROUPS]
    caches = refs[3 * N_GROUPS:4 * N_GROUPS]
    biases = refs[4 * N_GROUPS:5 * N_GROUPS]
    bn_ref, o_ref, lse_ref, os_ref = refs[5 * N_GROUPS:]
    for seq in range(n_seq):
        _sample_attn(qs, ks, vs, caches, biases, bn_ref, os_ref, seq, n_heads=n_heads)
    _prompt_attn(*prompt_in, o_ref, lse_ref, n_heads=n_heads, n_blk=n_blk)


def _attn_tables(dil, slopes_g):
    i = np.arange(ATTN_BLOCK)[:, None]
    j = np.arange(2 * ATTN_BLOCK)[None, :]
    delta = i + ATTN_BLOCK - j
    valid = (delta >= 0) & (delta <= N_BACK)
    dist = (delta * dil).astype(np.float32)
    bias = np.where(valid[None], -slopes_g[:, None, None] * dist[None] * LOG2E, NEG).astype(np.float32)
    first = np.where((j >= ATTN_BLOCK)[None], bias, NEG).astype(np.float32)
    return jnp.asarray(np.stack([bias, first]))


def _sattn_tables(n_heads, t_new, slopes, wbs):
    t = np.arange(t_new)[:, None]
    biases, bn = [], []
    for g in range(N_GROUPS):
        d, wb = DILATIONS[g], wbs[g]
        dist = wb + t - np.arange(wb)[None, :]
        valid = (dist % d == 0) & (dist // d <= N_BACK)
        biases.append(np.where(valid[None], -slopes[g][:, None, None] * dist[None] * LOG2E, NEG).astype(np.float32))
        dist = t - np.arange(t_new)[None, :]
        valid = (dist >= 0) & (dist % d == 0) & (dist // d <= N_BACK)
        bn.append(np.where(valid[None], -slopes[g][:, None, None] * dist[None] * LOG2E, NEG).astype(np.float32))
    return [jnp.asarray(b) for b in biases], jnp.asarray(np.stack(bn))


def _attn_steps(q):
    n, dil, sub, _ = q.shape
    nb = sub // ATTN_BLOCK
    n_blk = min(ATTN_BLOCKS_PER_STEP, nb)
    assert sub % ATTN_BLOCK == 0 and nb % n_blk == 0
    return n_blk, nb // n_blk, n * dil * (nb // n_blk)


def _attn_call(q, k, v, g, n_heads, slopes, sample, first_block, n_seq):
    n, dil, sub, width = q.shape
    assert dil == DILATIONS[g] and width == n_heads * HEAD_DIM and n_heads <= LANES
    n_blk, steps_i, steps = _attn_steps(q)
    cur = pl.BlockSpec((None, None, n_blk * ATTN_BLOCK, width), lambda b, r, i: (b, r, i, 0))
    prev = pl.BlockSpec((None, None, ATTN_BLOCK, width), lambda b, r, i: (b, r, jnp.maximum(i * n_blk - 1, 0), 0))
    lse = pl.BlockSpec((None, None, n_blk * ATTN_BLOCK, LANES), lambda b, r, i: (b, r, i, 0))
    bias = _attn_tables(dil, slopes[g])

    qs, ks, vs, caches = sample
    t_new = qs[0].shape[1]
    wbs = [c.shape[1] for c in caches]
    sbiases, bn = _sattn_tables(n_heads, t_new, slopes, wbs)
    views = [jnp.transpose(c, (0, 2, 3, 4, 1)) for c in caches]

    def step(b, r, i):
        return (b * dil + r) * steps_i + i

    new = pl.BlockSpec((n_seq, t_new, width), lambda b, r, i: (first_block + step(b, r, i), 0, 0))
    cache_specs = [pl.BlockSpec((n_seq, 2, n_heads, HEAD_DIM, wb), lambda b, r, i: (first_block + step(b, r, i), 0, 0, 0, 0))
                   for wb in wbs]
    return pl.pallas_call(
        functools.partial(_attn_kernel, n_heads=n_heads, n_blk=n_blk, n_seq=n_seq),
        grid=(n, dil, steps_i),
        in_specs=[cur, prev, cur, prev, cur, _resident(bias.shape)] + [new] * (3 * N_GROUPS) + cache_specs
        + [_resident(b.shape) for b in sbiases] + [_resident(bn.shape)],
        out_specs=[cur, lse, pl.BlockSpec((n_seq, t_new, width), lambda b, r, i: (step(b, r, i), 0, 0))],
        out_shape=[jax.ShapeDtypeStruct(q.shape, BF16), jax.ShapeDtypeStruct((n, dil, sub, LANES), F32),
                   jax.ShapeDtypeStruct((steps * n_seq, t_new, width), BF16)],
        compiler_params=pltpu.CompilerParams(dimension_semantics=("arbitrary", "arbitrary", "arbitrary"),
                                             vmem_limit_bytes=V7X_VMEM_LIMIT),
        name=f"attn_g{g}",
    )(q, k, k, v, v, bias, *qs, *ks, *vs, *views, *sbiases, bn)


def _hgrn_tables(c):
    t = np.arange(c)
    tri = (t[None, :] <= t[:, None]).astype(np.float32)
    masks = [np.eye(c, dtype=bool)]
    signs = []
    w = c // 2
    while w >= 1:
        blk = t // (2 * w)
        late = (t // w) % 2 == 1
        masks.append((blk[:, None] == blk[None, :]) & late[:, None] & ~late[None, :])
        if w < 8:
            signs.append(np.where(late, 1.0, -1.0))
        w //= 2
    masks = np.stack(masks).astype(np.float32)
    signs = np.repeat(np.stack(signs).reshape(-1, 1), HGRN_HEAD, axis=1).astype(np.float32)
    return jnp.asarray(tri, BF16), jnp.asarray(masks), jnp.asarray(signs)


def _level_exponents(b, g, c):
    width = b.shape[1]
    coarse, fine = [], []
    w = c // 2
    while w >= 8:
        parts = []
        for k in range(c // (2 * w)):
            lo, mid, hi = k * 2 * w, k * 2 * w + w, (k + 1) * 2 * w
            ref = b[mid - 1:mid]
            parts.append((slice(lo, mid), False, ref - b[lo:mid]))
            parts.append((slice(mid, hi), True, b[mid:hi] - ref))
        coarse.append(parts)
        w //= 2
    b3 = b.reshape(c // 8, 8, width)
    sub = lax.broadcasted_iota(jnp.int32, b3.shape, 1)
    if c >= 8:
        fine.append((b3 - jnp.broadcast_to(b3[:, 3:4], b3.shape)).reshape(c, width))
    ref = jnp.where(sub < 4, jnp.broadcast_to(b3[:, 1:2], b3.shape), jnp.broadcast_to(b3[:, 5:6], b3.shape))
    fine.append((b3 - ref).reshape(c, width))
    row = lax.broadcasted_iota(jnp.int32, b.shape, 0)
    fine.append(jnp.where(row % 2 == 1, g, 0.0))
    return coarse, fine


def _hgrn_chunks(seqs, tri, masks, signs, og, c):
    n_lev = masks.shape[0] - 1
    tri_v = tri[...]
    og = og[...]
    pre = []
    for sq, g, kf, iv, state in seqs:
        g1 = g.astype(BF16)
        r1 = g - g1.astype(F32)
        g2 = r1.astype(BF16)
        g3 = (r1 - g2.astype(F32)).astype(BF16)
        b = _dot(tri_v, g1) + _dot(tri_v, g2) + _dot(tri_v, g3)
        pre.append((b, _level_exponents(b, g, c)))
    heads = []
    for (sq, g, kf, iv, state), (b, (coarse, fine)) in zip(seqs, pre):
        e_cum = jnp.exp2(b)
        e_tail = jnp.exp2(b[c - 1:c] - b)
        for h in range(len(state)):
            hs = slice(h * HGRN_HEAD, (h + 1) * HGRN_HEAD)
            sqh, kfh, ivh = sq[:, hs], kf[:, hs], iv[:, hs]
            sqf, kff = sqh.astype(F32), kfh.astype(F32)
            eb = e_cum[:, hs]
            prods = [_dot_nt(sqh, kfh)]
            for parts in coarse:
                x = jnp.concatenate([(sqf if late else kff)[rows] * jnp.exp2(e[:, hs]) for rows, late, e in parts],
                                    axis=0).astype(BF16)
                prods.append(_dot_nt(x, x))
            for lv, e in enumerate(fine):
                sgn = signs[lv * c:(lv + 1) * c]
                x = (jnp.where(sgn > 0.0, sqf, kff) * jnp.exp2(e[:, hs] * sgn)).astype(BF16)
                prods.append(_dot_nt(x, x))
            o_inter = _dot((sqf * eb).astype(BF16), state[h].astype(BF16))
            kt = (kff * e_tail[:, hs]).astype(BF16)
            heads.append((prods, o_inter, _dot_tn(kt, ivh), eb, ivh, state[h]))
    results = []
    for prods, o_inter, upd, eb, ivh, st in heads:
        att = masks[0] * prods[0]
        for lv in range(n_lev):
            att = att + masks[1 + lv] * prods[1 + lv]
        o = o_inter + _dot(att.astype(BF16), ivh)
        decay = jnp.broadcast_to(eb[c - 1:c, :], (HGRN_HEAD, HGRN_HEAD)).T
        results.append((o * lax.rsqrt(jnp.mean(o * o, axis=-1, keepdims=True) + EPS) * og, decay * st + upd))
    n_h = len(seqs[0][4])
    return [([r[0] for r in results[i * n_h:(i + 1) * n_h]], [r[1] for r in results[i * n_h:(i + 1) * n_h]])
            for i in range(len(seqs))]


def _hgrn_prompt_kernel(sq_ref, g_ref, kf_ref, iv_ref, tri_ref, masks_ref, signs_ref, og_ref, o_ref, s_ref,
                        *, c, n_heads):
    @pl.when(pl.program_id(0) == 0)
    def _():
        s_ref[...] = jnp.zeros_like(s_ref)

    def body(ci, carry):
        rows = pl.ds(pl.multiple_of(ci * c, c), c)
        seqs = [(sq_ref[b, rows, :], g_ref[b, rows, :], kf_ref[b, rows, :], iv_ref[b, rows, :],
                 [s_ref[b, h] for h in range(n_heads)]) for b in range(sq_ref.shape[0])]
        res = _hgrn_chunks(seqs, tri_ref, masks_ref, signs_ref, og_ref, c)
        for b, (outs, new_state) in enumerate(res):
            for h in range(n_heads):
                s_ref[b, h] = new_state[h]
                o_ref[b, rows, h * HGRN_HEAD:(h + 1) * HGRN_HEAD] = outs[h].astype(o_ref.dtype)
        return carry

    lax.fori_loop(0, sq_ref.shape[1] // c, body, 0)


def _hgrn_prompt_call(sq, g, kf, iv, og, n, t, n_heads):
    width = n_heads * HGRN_HEAD
    c = min(HGRN_CHUNK, t)
    tt = min(HGRN_STEP_TOKENS, t)
    assert t % tt == 0 and tt % c == 0 and n <= HGRN_MAX_SEQS
    tri, masks, signs = _hgrn_tables(c)
    tok = pl.BlockSpec((n, tt, width), lambda i: (0, i, 0))
    o, s = pl.pallas_call(
        functools.partial(_hgrn_prompt_kernel, c=c, n_heads=n_heads),
        grid=(t // tt,),
        in_specs=[tok, tok, tok, tok, _resident(tri.shape), _resident(masks.shape), _resident(signs.shape),
                  _resident((1, HGRN_HEAD))],
        out_specs=[tok, pl.BlockSpec((n, n_heads, HGRN_HEAD, HGRN_HEAD), lambda i: (0, 0, 0, 0))],
        out_shape=[jax.ShapeDtypeStruct((n, t, width), BF16),
                   jax.ShapeDtypeStruct((n, n_heads, HGRN_HEAD, HGRN_HEAD), F32)],
        compiler_params=pltpu.CompilerParams(dimension_semantics=("arbitrary",), vmem_limit_bytes=V7X_VMEM_LIMIT),
        name="hgrn_prompt",
    )(sq.reshape(n, t, width), g.reshape(n, t, width), kf.reshape(n, t, width), iv.reshape(n, t, width),
      tri, masks, signs, og.reshape(1, HGRN_HEAD))
    return o.reshape(n * t, width), s


def _hgrn_sample_kernel(sq_ref, g_ref, kf_ref, iv_ref, s0_ref, tri_ref, masks_ref, signs_ref, og_ref, o_ref, s_ref,
                        *, c, n_heads):
    seqs = [(sq_ref[b], g_ref[b], kf_ref[b], iv_ref[b], [s0_ref[b, h] for h in range(n_heads)])
            for b in range(sq_ref.shape[0])]
    res = _hgrn_chunks(seqs, tri_ref, masks_ref, signs_ref, og_ref, c)
    for b, (outs, new_state) in enumerate(res):
        for h in range(n_heads):
            s_ref[b, h] = new_state[h]
            o_ref[b, :, h * HGRN_HEAD:(h + 1) * HGRN_HEAD] = outs[h].astype(o_ref.dtype)


def _hgrn_sample_call(sq, g, kf, iv, s0, og, n, t, n_heads):
    width = n_heads * HGRN_HEAD
    bn = min(HGRN_SAMPLE_SEQS, n)
    assert n % bn == 0 and t % 8 == 0
    tri, masks, signs = _hgrn_tables(t)
    tok = pl.BlockSpec((bn, t, width), lambda i: (i, 0, 0))
    st = pl.BlockSpec((bn, n_heads, HGRN_HEAD, HGRN_HEAD), lambda i: (i, 0, 0, 0))
    o, s = pl.pallas_call(
        functools.partial(_hgrn_sample_kernel, c=t, n_heads=n_heads),
        grid=(n // bn,),
        in_specs=[tok, tok, tok, tok, st, _resident(tri.shape), _resident(masks.shape), _resident(signs.shape),
                  _resident((1, HGRN_HEAD))],
        out_specs=[tok, st],
        out_shape=[jax.ShapeDtypeStruct((n, t, width), BF16),
                   jax.ShapeDtypeStruct((n, n_heads, HGRN_HEAD, HGRN_HEAD), F32)],
        compiler_params=pltpu.CompilerParams(dimension_semantics=("arbitrary",), vmem_limit_bytes=V7X_VMEM_LIMIT),
        name="hgrn_sample",
    )(sq.reshape(n, t, width), g.reshape(n, t, width), kf.reshape(n, t, width), iv.reshape(n, t, width),
      s0, tri, masks, signs, og.reshape(1, HGRN_HEAD))
    return o.reshape(n * t, width), s


def _load_tokens(ref, scr):
    bs, dil, rows, width = ref.shape
    if dil == 1:
        return ref[...].reshape(bs * rows, width).astype(F32)
    for r in range(dil):
        for c in range(width // LANES):
            scr[c, pl.ds(r, rows, stride=dil), :] = ref[0, r, :, c * LANES:(c + 1) * LANES].astype(F32)
    return jnp.concatenate([scr[c] for c in range(width // LANES)], axis=1)


def _out_kernel(*refs, n_att):
    att = refs[:2 * n_att] if n_att > 1 else refs[:1]
    n_in = len(att)
    sza_ref, ob_ref, szb_ref, sga_ref, sgb_ref, x_ref, gate_ref, ex_ref, wa_ref, wb_ref, wo_ref, y_ref, scr = refs[n_in:]
    if n_att > 1:
        ls_ = [_load_tokens(r, scr) for r in att[n_att:]]
        m = functools.reduce(jnp.maximum, ls_)
        es = [jnp.exp2(l - m) for l in ls_]
        inv = 1.0 / functools.reduce(lambda a, b: a + b, es)
        alphas = [_dot((e * inv).astype(BF16), ex_ref[...]) for e in es]
        o_a = functools.reduce(lambda a, b: a + b, [al * _load_tokens(r, scr) for al, r in zip(alphas, att[:n_att])])
    else:
        o_a = _load_tokens(att[0], scr)
    branch_a = _dot((o_a * sza_ref[...].astype(F32)).astype(BF16), wa_ref[...])
    branch_b = _dot((ob_ref[...].astype(F32) * szb_ref[...].astype(F32)).astype(BF16), wb_ref[...])
    merged = sga_ref[...].astype(F32) * branch_a + sgb_ref[...].astype(F32) * branch_b
    upd = _dot(merged.astype(BF16), wo_ref[...])
    x = x_ref[...]
    y_ref[...] = x + gate_ref[...] * upd.reshape(x.shape)


def _out_call(att_inputs, sza, ob, szb, sga, sgb, x, gate, wa_bf, wb_bf, wo_bf):
    s, t, d = x.shape
    bs, bt = _token_tiling(s, t)
    tm = bs * bt
    n_t = t // bt

    def tok_spec(a):
        return pl.BlockSpec((tm, a.shape[1]), lambda i, j: (i * n_t + j, 0))

    def att_spec(a):
        dil = a.shape[1]
        assert dil == 1 or bs == 1
        return pl.BlockSpec((bs, dil, bt // dil, a.shape[3]), lambda i, j: (i, 0, j, 0))

    n_att = (len(att_inputs) + 1) // 2
    toks = [sza, ob, szb, sga, sgb]
    width = att_inputs[0].shape[3]
    lane_head = np.arange(LANES)[:, None] == (np.arange(width) // HEAD_DIM)[None, :]
    expand = jnp.asarray(lane_head, BF16)
    return pl.pallas_call(
        functools.partial(_out_kernel, n_att=n_att),
        grid=(s // bs, n_t),
        in_specs=[att_spec(a) for a in att_inputs] + [tok_spec(a) for a in toks]
        + [pl.BlockSpec((bs, bt, d), lambda i, j: (i, j, 0)),
           pl.BlockSpec((bs, 1, d), lambda i, j: (i, 0, 0)),
           _resident(expand.shape), _resident(wa_bf.shape), _resident(wb_bf.shape), _resident(wo_bf.shape)],
        out_specs=pl.BlockSpec((bs, bt, d), lambda i, j: (i, j, 0)),
        out_shape=jax.ShapeDtypeStruct((s, t, d), F32),
        scratch_shapes=[pltpu.VMEM((width // LANES, tm, LANES), F32)],
        compiler_params=pltpu.CompilerParams(dimension_semantics=("arbitrary", "arbitrary"),
                                             vmem_limit_bytes=V7X_VMEM_LIMIT),
        name="out",
    )(*att_inputs, *toks, x, gate, expand, wa_bf, wb_bf, wo_bf)


def _kvwin_kernel(k_ref, v_ref, o_ref, scr):
    o_ref[0, 0] = _load_tokens(k_ref, scr).T
    o_ref[0, 1] = _load_tokens(v_ref, scr).T


def _kvwin_call(k, v, w):
    n, dil, sub, width = k.shape
    blk = min(w, PROJ_TOKENS)
    assert w % blk == 0 and (sub * dil - w) % blk == 0 and blk % dil == 0
    first = (sub * dil - w) // blk
    src = pl.BlockSpec((1, dil, blk // dil, width), lambda b, i: (b, 0, first + i, 0))
    return pl.pallas_call(
        _kvwin_kernel,
        grid=(n, w // blk),
        in_specs=[src, src],
        out_specs=pl.BlockSpec((1, 2, width, blk), lambda b, i: (b, 0, 0, i)),
        out_shape=jax.ShapeDtypeStruct((n, 2, width, w), F32),
        scratch_shapes=[pltpu.VMEM((width // LANES, blk, LANES), F32)],
        compiler_params=pltpu.CompilerParams(dimension_semantics=("arbitrary", "arbitrary")),
        name="kvwin",
    )(k, v)


def _kvnew_kernel(*refs):
    n_g = len(refs) // 3
    for g in range(n_g):
        for kv in range(2):
            src = refs[kv * n_g + g]
            for t in range(src.shape[1]):
                refs[2 * n_g + g][t, kv] = src[:, t, :].T


def _kvnew_call(ks, vs):
    n, t, width = ks[0].shape
    return pl.pallas_call(
        _kvnew_kernel,
        out_shape=[jax.ShapeDtypeStruct((t, 2, width, n), F32) for _ in ks],
        compiler_params=pltpu.CompilerParams(vmem_limit_bytes=V7X_VMEM_LIMIT),
        name="kvnew",
    )(*ks, *vs)


def kernel(x_prompt, x_sample, cache_kv_g0, cache_kv_g1, cache_kv_g2, state_hgrn, c_prompt, c_sample, norm_gain, w_ada, b_ada, w_in, q_norm_gain, k_norm_gain, hgrn_lb_logits, hgrn_out_norm_gain, w_branch_a, w_branch_b, w_out):
    depth = w_in.shape[0]
    assert depth == 1, "single-layer step"
    n_p, t_p, d = x_prompt.shape
    n_s, t_s, _ = x_sample.shape
    a_out = w_branch_a.shape[1]
    n_heads = a_out // HEAD_DIM
    b_f = w_branch_b.shape[1]
    b_heads = b_f // HGRN_HEAD
    slopes = _alibi_slopes(n_heads)

    n_c = n_p + n_s
    pad = (-n_c) % 8
    c_all = jnp.concatenate([c_prompt, c_sample, jnp.zeros((pad, d), F32)], axis=0)
    ada = _ada_call(c_all, w_ada[0], b_ada[0])
    shift, scale, gate = ada[:, :d], ada[:, d:2 * d], ada[:, 2 * d:]

    w_in_bf = w_in[0].astype(BF16)
    wa_bf, wb_bf, wo_bf = w_branch_a[0].astype(BF16), w_branch_b[0].astype(BF16), w_out[0].astype(BF16)
    head_of = np.arange(MXU_WIDTH) // HEAD_DIM
    bd = jnp.asarray(head_of[:, None] == head_of[None, :], BF16)
    qg = (jnp.tile(q_norm_gain[0], n_heads) * (ATTN_SCALE * LOG2E)).reshape(1, a_out)
    kg = jnp.tile(k_norm_gain[0], n_heads).reshape(1, a_out)
    og = hgrn_out_norm_gain[0]

    def cond(lo, n):
        return [a[lo:lo + n].reshape(n, 1, d) for a in (shift, scale, gate)]

    sh, sc, gt_p = cond(0, n_p)
    outs = _proj_call(x_prompt, sh, sc, norm_gain[0], w_in_bf, bd, qg, kg, hgrn_lb_logits, n_groups=N_GROUPS,
                      dils=DILATIONS, a_out=a_out, b_f=b_f, kv_dtype=BF16)
    qs, ks, vs = outs[0:N_GROUPS], outs[N_GROUPS:2 * N_GROUPS], outs[2 * N_GROUPS:3 * N_GROUPS]
    act_p = outs[3 * N_GROUPS:]
    sh, sc, gt_s = cond(n_p, n_s)
    outs = _proj_call(x_sample, sh, sc, norm_gain[0], w_in_bf, bd, qg, kg, hgrn_lb_logits, n_groups=N_GROUPS,
                      dils=(1,) * N_GROUPS, a_out=a_out, b_f=b_f, kv_dtype=F32)
    qn, kn, vn = [[a.reshape(n_s, t_s, a_out) for a in outs[i * N_GROUPS:(i + 1) * N_GROUPS]] for i in range(3)]
    act_s = outs[3 * N_GROUPS:]

    steps = [_attn_steps(qs[gi])[2] for gi in range(N_GROUPS)]
    assert n_s % sum(steps) == 0, "sample sequences must spread evenly over the prompt attention grid steps"
    n_seq = n_s // sum(steps)
    sample = (qn, kn, vn, (cache_kv_g0[0], cache_kv_g1[0], cache_kv_g2[0]))
    att = [_attn_call(qs[gi], ks[gi], vs[gi], gi, n_heads, slopes, sample, sum(steps[:gi]), n_seq)
           for gi in range(N_GROUPS)]
    o_a = jnp.concatenate([a[2] for a in att], axis=0)

    sza, sqb, g, kf, ib, szb, sga, sgb = act_p
    ob, hgrn_p = _hgrn_prompt_call(sqb, g, kf, ib, og, n_p, t_p, b_heads)
    y_p = _out_call([a[0] for a in att] + [a[1] for a in att], sza, ob, szb, sga, sgb, x_prompt, gt_p,
                    wa_bf, wb_bf, wo_bf)
    kv_p = []
    for gi in range(N_GROUPS):
        w = min(WINDOWS[gi], t_p)
        rows = _kvwin_call(ks[gi], vs[gi], w).reshape(n_p, 2, n_heads, HEAD_DIM, w)
        kv_p.append(jnp.transpose(rows, (0, 4, 1, 2, 3))[None])

    sza, sqb, g, kf, ib, szb, sga, sgb = act_s
    ob, hgrn_s = _hgrn_sample_call(sqb, g, kf, ib, state_hgrn[0], og, n_s, t_s, b_heads)
    y_s = _out_call([o_a.reshape(n_s, 1, t_s, a_out)], sza, ob, szb, sga, sgb, x_sample, gt_s, wa_bf, wb_bf, wo_bf)
    kv_s = [jnp.transpose(a.reshape(t_s, 2, n_heads, HEAD_DIM, n_s), (4, 0, 1, 2, 3))[None]
            for a in _kvnew_call(kn, vn)]

    return (y_p, y_s, kv_p[0], kv_p[1], kv_p[2], hgrn_p[None], kv_s[0], kv_s[1], kv_s[2], hgrn_s[None])
```

```python
import functools

import numpy as np
import jax
import jax.numpy as jnp
from jax import lax
from jax.experimental import pallas as pl
from jax.experimental.pallas import tpu as pltpu

F32 = jnp.float32
BF16 = jnp.bfloat16

HEAD_DIM = 64
N_GROUPS = 3
WINDOWS = (128, 512, 2048)
DILATIONS = (1, 4, 16)
N_BACK = 128
ATTN_BLOCK = 128
ATTN_BLOCKS_PER_STEP = 4
ATTN_SCALE = HEAD_DIM ** -0.5
HGRN_HEAD = 128
LANES = 128
LOG2E = 1.4426950408889634
EPS = 1e-6
NEG = -1e30

V7X_VMEM_LIMIT = 56 * 1024 * 1024
PROJ_TOKENS = 512
HGRN_CHUNK = 64
HGRN_STEP_TOKENS = 512
HGRN_MAX_SEQS = 4
HGRN_SAMPLE_SEQS = 8


def _sigmoid(x):
    return 1.0 / (1.0 + jnp.exp(-x))


def _silu(x):
    return x * _sigmoid(x)


def _dot(a, b):
    return jnp.dot(a, b, preferred_element_type=F32)


def _dot_nt(a, b):
    return lax.dot_general(a, b, (((1,), (1,)), ((), ())), preferred_element_type=F32)


def _dot_tn(a, b):
    return lax.dot_general(a, b, (((0,), (0,)), ((), ())), preferred_element_type=F32)


def _alibi_slopes(n_heads_per_group):
    a_heads = N_GROUPS * n_heads_per_group
    s = 2.0 ** (-8.0 * (np.arange(a_heads) + 1) / a_heads)
    return s.astype(np.float32).reshape(N_GROUPS, n_heads_per_group)


def _resident(shape):
    nd = len(shape)
    return pl.BlockSpec(shape, lambda *_: (0,) * nd, pipeline_mode=pl.Buffered(1))


def _ada_kernel(c_ref, w_ref, b_ref, o_ref):
    o_ref[...] = _dot(_silu(c_ref[...]).astype(BF16), w_ref[...]) + b_ref[...]


def _ada_call(c, w_ada, b_ada):
    n, d = c.shape
    cols = w_ada.shape[1]
    blk = d
    return pl.pallas_call(
        _ada_kernel,
        grid=(cols // blk,),
        in_specs=[pl.BlockSpec((n, d), lambda j: (0, 0)),
                  pl.BlockSpec((d, blk), lambda j: (0, j)),
                  pl.BlockSpec((1, blk), lambda j: (0, j))],
        out_specs=pl.BlockSpec((n, blk), lambda j: (0, j)),
        out_shape=jax.ShapeDtypeStruct((n, cols), F32),
        name="ada",
    )(c, w_ada.astype(BF16), b_ada.reshape(1, cols))


def _store_dilated(out_ref, val, scr, dil):
    bs, d, rows, width = out_ref.shape
    if dil == 1:
        out_ref[...] = val.reshape(bs, 1, rows, width).astype(out_ref.dtype)
        return
    for c in range(width // LANES):
        scr[c] = val[:, c * LANES:(c + 1) * LANES]
    for r in range(dil):
        for c in range(width // LANES):
            out_ref[0, r, :, c * LANES:(c + 1) * LANES] = scr[c, pl.ds(r, rows, stride=dil), :].astype(out_ref.dtype)


def _proj_kernel(x_ref, shift_ref, scale_ref, ng_ref, w_ref, qg_ref, kg_ref, lbl_ref, *refs,
                 n_groups, dils, a_out, d_model):
    qkv_refs = refs[:3 * n_groups]
    sza_ref, sqb_ref, g_ref, kf_ref, ib_ref, szb_ref, sga_ref, sgb_ref, scr = refs[3 * n_groups:]
    x = x_ref[...]
    ms = jnp.mean(x * x, axis=-1, keepdims=True)
    h = x * lax.rsqrt(ms + EPS) * ng_ref[...]
    h = h * (1.0 + scale_ref[...]) + shift_ref[...]
    tm = h.shape[0] * h.shape[1]
    hb = h.reshape(tm, d_model).astype(BF16)
    tile = a_out

    def proj(c0):
        return _dot(hb, w_ref[:, c0:c0 + tile])

    def head_norm(u, gain):
        u2 = u * u
        low = lax.broadcasted_iota(jnp.int32, (tm, LANES), 1) < HEAD_DIM
        outs = []
        for c in range(tile // LANES):
            s = u2[:, c * LANES:(c + 1) * LANES]
            lo = jnp.sum(jnp.where(low, s, 0.0), axis=-1, keepdims=True)
            hi = jnp.sum(jnp.where(low, 0.0, s), axis=-1, keepdims=True)
            r_lo = lax.rsqrt(lo * (1.0 / HEAD_DIM) + EPS)
            r_hi = lax.rsqrt(hi * (1.0 / HEAD_DIM) + EPS)
            outs.append(u[:, c * LANES:(c + 1) * LANES] * jnp.where(low, r_lo, r_hi))
        return jnp.concatenate(outs, axis=1) * gain

    c0 = 0
    for kind in range(3):
        for g in range(n_groups):
            u = proj(c0)
            if kind == 0:
                u = head_norm(u, qg_ref[...])
            elif kind == 1:
                u = head_norm(u, kg_ref[...])
            _store_dilated(qkv_refs[kind * n_groups + g], u, scr, dils[g])
            c0 += tile
    sza_ref[...] = _silu(proj(c0)).astype(BF16)
    c0 += tile
    sqb_ref[...] = _silu(proj(c0)).astype(BF16)
    c0 += tile
    lbl = lbl_ref[...]
    lbe = jnp.exp(lbl - jnp.max(lbl, axis=0, keepdims=True))
    lb = lbe[0:1] / jnp.sum(lbe, axis=0, keepdims=True)
    fr = proj(c0)
    g_ref[...] = jnp.log2(lb + (1.0 - lb) * _sigmoid(fr))
    kf_ref[...] = ((1.0 - lb) * _sigmoid(-fr)).astype(BF16)
    c0 += tile
    ib_ref[...] = proj(c0).astype(BF16)
    c0 += tile
    szb_ref[...] = _silu(proj(c0)).astype(BF16)
    c0 += tile
    for j in range(d_model // tile):
        sga_ref[:, j * tile:(j + 1) * tile] = _sigmoid(proj(c0)).astype(BF16)
        c0 += tile
    for j in range(d_model // tile):
        sgb_ref[:, j * tile:(j + 1) * tile] = _sigmoid(proj(c0)).astype(BF16)
        c0 += tile


def _token_tiling(s, t):
    if t >= PROJ_TOKENS:
        assert t % PROJ_TOKENS == 0
        return 1, PROJ_TOKENS
    bs = min(s, PROJ_TOKENS // t)
    assert s % bs == 0 and t % 8 == 0
    return bs, t


def _proj_call(x, shift, scale, norm_gain, w_in_bf, qg, kg, lb_logits, *, n_groups, dils, a_out, b_f, kv_dtype):
    s, t, d = x.shape
    bs, bt = _token_tiling(s, t)
    tm = bs * bt
    n_t = t // bt
    ntok = s * t
    grid = (s // bs, n_t)
    assert all(dl == 1 or (bs == 1 and bt % (dl * 16) == 0) for dl in dils)

    def tok_spec(cols):
        return pl.BlockSpec((tm, cols), lambda i, j: (i * n_t + j, 0))

    qkv_dtypes = [BF16] * n_groups + [kv_dtype] * (2 * n_groups)
    qkv_shapes = [jax.ShapeDtypeStruct((s, dils[g], t // dils[g], a_out), dt)
                  for g, dt in zip(list(range(n_groups)) * 3, qkv_dtypes)]
    qkv_specs = [pl.BlockSpec((bs, dils[g], bt // dils[g], a_out), lambda i, j: (i, 0, j, 0))
                 for g in list(range(n_groups)) * 3]
    out_cols = [(a_out, BF16), (b_f, BF16), (b_f, F32), (b_f, BF16), (b_f, BF16), (b_f, BF16), (d, BF16), (d, BF16)]
    kern = functools.partial(_proj_kernel, n_groups=n_groups, dils=tuple(dils), a_out=a_out, d_model=d)
    return pl.pallas_call(
        kern,
        grid=grid,
        in_specs=[pl.BlockSpec((bs, bt, d), lambda i, j: (i, j, 0)),
                  pl.BlockSpec((bs, 1, d), lambda i, j: (i, 0, 0)),
                  pl.BlockSpec((bs, 1, d), lambda i, j: (i, 0, 0)),
                  _resident((1, d)),
                  _resident(w_in_bf.shape),
                  _resident(qg.shape),
                  _resident(kg.shape),
                  _resident(lb_logits.shape)],
        out_specs=qkv_specs + [tok_spec(c) for c, _ in out_cols],
        out_shape=qkv_shapes + [jax.ShapeDtypeStruct((ntok, c), dt) for c, dt in out_cols],
        scratch_shapes=[pltpu.VMEM((a_out // LANES, tm, LANES), F32)],
        compiler_params=pltpu.CompilerParams(dimension_semantics=("arbitrary", "arbitrary"),
                                             vmem_limit_bytes=V7X_VMEM_LIMIT),
        name="proj",
    )(x, shift, scale, norm_gain.reshape(1, d), w_in_bf, qg, kg, lb_logits)


def _prompt_attn(q_ref, kp_ref, kc_ref, vp_ref, vc_ref, bias_ref, o_ref, lse_ref, *, n_heads, n_blk):
    first = (pl.program_id(2) == 0).astype(jnp.int32)
    lane = lax.broadcasted_iota(jnp.int32, (ATTN_BLOCK, 2 * HEAD_DIM), 1)
    low = lane < HEAD_DIM
    for blk in range(n_blk):
        rows = slice(blk * ATTN_BLOCK, (blk + 1) * ATTN_BLOCK)
        before = slice((blk - 1) * ATTN_BLOCK, blk * ATTN_BLOCK)
        table = first if blk == 0 else 0
        scores, values = [], []
        for hp in range(n_heads // 2):
            sl = slice(hp * 2 * HEAD_DIM, (hp + 1) * 2 * HEAD_DIM)
            q2 = q_ref[rows, sl]
            k_prev = kp_ref[:, sl] if blk == 0 else kc_ref[before, sl]
            v_prev = vp_ref[:, sl] if blk == 0 else vc_ref[before, sl]
            k2 = jnp.concatenate([k_prev, kc_ref[rows, sl]], axis=0)
            values.append(jnp.concatenate([v_prev, vc_ref[rows, sl]], axis=0))
            for par in range(2):
                keep = low if par == 0 else jnp.logical_not(low)
                qm = jnp.where(keep, q2, jnp.zeros_like(q2))
                scores.append(_dot_nt(qm, k2) + bias_ref[table, 2 * hp + par])
        soft = []
        for s in scores:
            m = jnp.max(s, axis=-1, keepdims=True)
            p = jnp.exp2(s - m)
            soft.append((p.astype(BF16), m, jnp.sum(p, axis=-1, keepdims=True)))
        outs = [_dot(p, values[i // 2]) for i, (p, m, l) in enumerate(soft)]
        lse = jnp.zeros((ATTN_BLOCK, LANES), F32)
        for hp in range(n_heads // 2):
            sl = slice(hp * 2 * HEAD_DIM, (hp + 1) * 2 * HEAD_DIM)
            (_, m0, l0), (_, m1, l1) = soft[2 * hp], soft[2 * hp + 1]
            o_ref[rows, sl] = jnp.where(low, outs[2 * hp] / l0, outs[2 * hp + 1] / l1).astype(o_ref.dtype)
            lse = jnp.where(lane == 2 * hp, m0 + jnp.log2(l0), lse)
            lse = jnp.where(lane == 2 * hp + 1, m1 + jnp.log2(l1), lse)
        lse_ref[rows, :] = lse


def _sample_attn(qs, ks, vs, caches, biases, bn_ref, o_ref, seq, *, n_heads):
    per_head = []
    for h in range(n_heads):
        scores, values = [], []
        hs = slice(h * HEAD_DIM, (h + 1) * HEAD_DIM)
        for g in range(N_GROUPS):
            qh = qs[g][seq, :, hs]
            scores.append(_dot(qh, caches[g][seq, 0, h].astype(BF16)) + biases[g][h])
            values.append(caches[g][seq, 1, h].astype(BF16))
            scores.append(_dot_nt(qh, ks[g][seq, :, hs].astype(BF16)) + bn_ref[g, h])
            values.append(vs[g][seq, :, hs].astype(BF16))
        per_head.append((scores, values))
    soft = []
    for scores, values in per_head:
        m = functools.reduce(jnp.maximum, [jnp.max(s, axis=-1, keepdims=True) for s in scores])
        ps = [jnp.exp2(s - m) for s in scores]
        l = functools.reduce(lambda a, b: a + b, [jnp.sum(p, axis=-1, keepdims=True) for p in ps])
        soft.append(([p.astype(BF16) for p in ps], l))
    for h, ((ps, l), (_, values)) in enumerate(zip(soft, per_head)):
        parts = [(_dot_nt(p, v) if i % 2 == 0 else _dot(p, v)) for i, (p, v) in enumerate(zip(ps, values))]
        acc = functools.reduce(lambda a, b: a + b, parts)
        o_ref[seq, :, h * HEAD_DIM:(h + 1) * HEAD_DIM] = (acc / l).astype(o_ref.dtype)


def _attn_kernel(*refs, n_heads, n_blk, n_seq):
    prompt_in, refs = refs[:6], refs[6:]
    qs, ks, vs = refs[0:N_GROUPS], refs[N_GROUPS:2 * N_GROUPS], refs[2 * N_GROUPS:3 * N_GROUPS]
    caches = refs[3 * N_GROUPS:4 * N_GROUPS]
    biases = refs[4 * N_GROUPS:5 * N_GROUPS]
    bn_ref, o_ref, lse_ref, os_ref = refs[5 * N_GROUPS:]
    for seq in range(n_seq):
        _sample_attn(qs, ks, vs, caches, biases, bn_ref, os_ref, seq, n_heads=n_heads)
    _prompt_attn(*prompt_in, o_ref, lse_ref, n_heads=n_heads, n_blk=n_blk)


def _attn_tables(dil, slopes_g):
    i = np.arange(ATTN_BLOCK)[:, None]
    j = np.arange(2 * ATTN_BLOCK)[None, :]
    delta = i + ATTN_BLOCK - j
    valid = (delta >= 0) & (delta <= N_BACK)
    dist = (delta * dil).astype(np.float32)
    bias = np.where(valid[None], -slopes_g[:, None, None] * dist[None] * LOG2E, NEG).astype(np.float32)
    first = np.where((j >= ATTN_BLOCK)[None], bias, NEG).astype(np.float32)
    return jnp.asarray(np.stack([bias, first]))


def _sattn_tables(n_heads, t_new, slopes, wbs):
    t = np.arange(t_new)[:, None]
    biases, bn = [], []
    for g in range(N_GROUPS):
        d, wb = DILATIONS[g], wbs[g]
        dist = wb + t - np.arange(wb)[None, :]
        valid = (dist % d == 0) & (dist // d <= N_BACK)
        biases.append(np.where(valid[None], -slopes[g][:, None, None] * dist[None] * LOG2E, NEG).astype(np.float32))
        dist = t - np.arange(t_new)[None, :]
        valid = (dist >= 0) & (dist % d == 0) & (dist // d <= N_BACK)
        bn.append(np.where(valid[None], -slopes[g][:, None, None] * dist[None] * LOG2E, NEG).astype(np.float32))
    return [jnp.asarray(b) for b in biases], jnp.asarray(np.stack(bn))


def _attn_steps(q):
    n, dil, sub, _ = q.shape
    nb = sub // ATTN_BLOCK
    n_blk = min(ATTN_BLOCKS_PER_STEP, nb)
    assert sub % ATTN_BLOCK == 0 and nb % n_blk == 0
    return n_blk, nb // n_blk, n * dil * (nb // n_blk)


def _attn_call(q, k, v, g, n_heads, slopes, sample, first_block, n_seq):
    n, dil, sub, width = q.shape
    assert dil == DILATIONS[g] and width == n_heads * HEAD_DIM and n_heads <= LANES
    n_blk, steps_i, steps = _attn_steps(q)
    cur = pl.BlockSpec((None, None, n_blk * ATTN_BLOCK, width), lambda b, r, i: (b, r, i, 0))
    prev = pl.BlockSpec((None, None, ATTN_BLOCK, width), lambda b, r, i: (b, r, jnp.maximum(i * n_blk - 1, 0), 0))
    lse = pl.BlockSpec((None, None, n_blk * ATTN_BLOCK, LANES), lambda b, r, i: (b, r, i, 0))
    bias = _attn_tables(dil, slopes[g])

    qs, ks, vs, caches = sample
    t_new = qs[0].shape[1]
    wbs = [c.shape[1] for c in caches]
    sbiases, bn = _sattn_tables(n_heads, t_new, slopes, wbs)
    views = [jnp.transpose(c, (0, 2, 3, 4, 1)) for c in caches]

    def step(b, r, i):
        return (b * dil + r) * steps_i + i

    new = pl.BlockSpec((n_seq, t_new, width), lambda b, r, i: (first_block + step(b, r, i), 0, 0))
    cache_specs = [pl.BlockSpec((n_seq, 2, n_heads, HEAD_DIM, wb), lambda b, r, i: (first_block + step(b, r, i), 0, 0, 0, 0))
                   for wb in wbs]
    return pl.pallas_call(
        functools.partial(_attn_kernel, n_heads=n_heads, n_blk=n_blk, n_seq=n_seq),
        grid=(n, dil, steps_i),
        in_specs=[cur, prev, cur, prev, cur, _resident(bias.shape)] + [new] * (3 * N_GROUPS) + cache_specs
        + [_resident(b.shape) for b in sbiases] + [_resident(bn.shape)],
        out_specs=[cur, lse, pl.BlockSpec((n_seq, t_new, width), lambda b, r, i: (step(b, r, i), 0, 0))],
        out_shape=[jax.ShapeDtypeStruct(q.shape, BF16), jax.ShapeDtypeStruct((n, dil, sub, LANES), F32),
                   jax.ShapeDtypeStruct((steps * n_seq, t_new, width), BF16)],
        compiler_params=pltpu.CompilerParams(dimension_semantics=("arbitrary", "arbitrary", "arbitrary"),
                                             vmem_limit_bytes=V7X_VMEM_LIMIT),
        name=f"attn_g{g}",
    )(q, k, k, v, v, bias, *qs, *ks, *vs, *views, *sbiases, bn)


def _hgrn_tables(c):
    t = np.arange(c)
    tri = (t[None, :] <= t[:, None]).astype(np.float32)
    masks = [np.eye(c, dtype=bool)]
    signs = []
    w = c // 2
    while w >= 1:
        blk = t // (2 * w)
        late = (t // w) % 2 == 1
        masks.append((blk[:, None] == blk[None, :]) & late[:, None] & ~late[None, :])
        if w < 8:
            signs.append(np.where(late, 1.0, -1.0))
        w //= 2
    masks = np.stack(masks).astype(np.float32)
    signs = np.repeat(np.stack(signs).reshape(-1, 1), HGRN_HEAD, axis=1).astype(np.float32)
    return jnp.asarray(tri, BF16), jnp.asarray(masks), jnp.asarray(signs)


def _level_exponents(b, g, c):
    width = b.shape[1]
    coarse, fine = [], []
    w = c // 2
    while w >= 8:
        parts = []
        for k in range(c // (2 * w)):
            lo, mid, hi = k * 2 * w, k * 2 * w + w, (k + 1) * 2 * w
            ref = b[mid - 1:mid]
            parts.append((slice(lo, mid), False, ref - b[lo:mid]))
            parts.append((slice(mid, hi), True, b[mid:hi] - ref))
        coarse.append(parts)
        w //= 2
    b3 = b.reshape(c // 8, 8, width)
    sub = lax.broadcasted_iota(jnp.int32, b3.shape, 1)
    if c >= 8:
        fine.append((b3 - jnp.broadcast_to(b3[:, 3:4], b3.shape)).reshape(c, width))
    ref = jnp.where(sub < 4, jnp.broadcast_to(b3[:, 1:2], b3.shape), jnp.broadcast_to(b3[:, 5:6], b3.shape))
    fine.append((b3 - ref).reshape(c, width))
    row = lax.broadcasted_iota(jnp.int32, b.shape, 0)
    fine.append(jnp.where(row % 2 == 1, g, 0.0))
    return coarse, fine


def _hgrn_chunks(seqs, tri, masks, signs, og, c):
    n_lev = masks.shape[0] - 1
    tri_v = tri[...]
    og = og[...]
    pre = []
    for sq, g, kf, iv, state in seqs:
        g1 = g.astype(BF16)
        r1 = g - g1.astype(F32)
        g2 = r1.astype(BF16)
        g3 = (r1 - g2.astype(F32)).astype(BF16)
        b = _dot(tri_v, g1) + _dot(tri_v, g2) + _dot(tri_v, g3)
        pre.append((b, _level_exponents(b, g, c)))
    heads = []
    for (sq, g, kf, iv, state), (b, (coarse, fine)) in zip(seqs, pre):
        e_cum = jnp.exp2(b)
        e_tail = jnp.exp2(b[c - 1:c] - b)
        for h in range(len(state)):
            hs = slice(h * HGRN_HEAD, (h + 1) * HGRN_HEAD)
            sqh, kfh, ivh = sq[:, hs], kf[:, hs], iv[:, hs]
            sqf, kff = sqh.astype(F32), kfh.astype(F32)
            eb = e_cum[:, hs]
            prods = [_dot_nt(sqh, kfh)]
            for parts in coarse:
                x = jnp.concatenate([(sqf if late else kff)[rows] * jnp.exp2(e[:, hs]) for rows, late, e in parts],
                                    axis=0).astype(BF16)
                prods.append(_dot_nt(x, x))
            for lv, e in enumerate(fine):
                sgn = signs[lv * c:(lv + 1) * c]
                x = (jnp.where(sgn > 0.0, sqf, kff) * jnp.exp2(e[:, hs] * sgn)).astype(BF16)
                prods.append(_dot_nt(x, x))
            o_inter = _dot((sqf * eb).astype(BF16), state[h].astype(BF16))
            kt = (kff * e_tail[:, hs]).astype(BF16)
            heads.append((prods, o_inter, _dot_tn(kt, ivh), eb, ivh, state[h]))
    results = []
    for prods, o_inter, upd, eb, ivh, st in heads:
        att = masks[0] * prods[0]
        for lv in range(n_lev):
            att = att + masks[1 + lv] * prods[1 + lv]
        o = o_inter + _dot(att.astype(BF16), ivh)
        decay = jnp.broadcast_to(eb[c - 1:c, :], (HGRN_HEAD, HGRN_HEAD)).T
        results.append((o * lax.rsqrt(jnp.mean(o * o, axis=-1, keepdims=True) + EPS) * og, decay * st + upd))
    n_h = len(seqs[0][4])
    return [([r[0] for r in results[i * n_h:(i + 1) * n_h]], [r[1] for r in results[i * n_h:(i + 1) * n_h]])
            for i in range(len(seqs))]


def _hgrn_prompt_kernel(sq_ref, g_ref, kf_ref, iv_ref, tri_ref, masks_ref, signs_ref, og_ref, o_ref, s_ref,
                        *, c, n_heads):
    @pl.when(pl.program_id(0) == 0)
    def _():
        s_ref[...] = jnp.zeros_like(s_ref)

    def body(ci, carry):
        rows = pl.ds(pl.multiple_of(ci * c, c), c)
        seqs = [(sq_ref[b, rows, :], g_ref[b, rows, :], kf_ref[b, rows, :], iv_ref[b, rows, :],
                 [s_ref[b, h] for h in range(n_heads)]) for b in range(sq_ref.shape[0])]
        res = _hgrn_chunks(seqs, tri_ref, masks_ref, signs_ref, og_ref, c)
        for b, (outs, new_state) in enumerate(res):
            for h in range(n_heads):
                s_ref[b, h] = new_state[h]
                o_ref[b, rows, h * HGRN_HEAD:(h + 1) * HGRN_HEAD] = outs[h].astype(o_ref.dtype)
        return carry

    lax.fori_loop(0, sq_ref.shape[1] // c, body, 0)


def _hgrn_prompt_call(sq, g, kf, iv, og, n, t, n_heads):
    width = n_heads * HGRN_HEAD
    c = min(HGRN_CHUNK, t)
    tt = min(HGRN_STEP_TOKENS, t)
    assert t % tt == 0 and tt % c == 0 and n <= HGRN_MAX_SEQS
    tri, masks, signs = _hgrn_tables(c)
    tok = pl.BlockSpec((n, tt, width), lambda i: (0, i, 0))
    o, s = pl.pallas_call(
        functools.partial(_hgrn_prompt_kernel, c=c, n_heads=n_heads),
        grid=(t // tt,),
        in_specs=[tok, tok, tok, tok, _resident(tri.shape), _resident(masks.shape), _resident(signs.shape),
                  _resident((1, HGRN_HEAD))],
        out_specs=[tok, pl.BlockSpec((n, n_heads, HGRN_HEAD, HGRN_HEAD), lambda i: (0, 0, 0, 0))],
        out_shape=[jax.ShapeDtypeStruct((n, t, width), BF16),
                   jax.ShapeDtypeStruct((n, n_heads, HGRN_HEAD, HGRN_HEAD), F32)],
        compiler_params=pltpu.CompilerParams(dimension_semantics=("arbitrary",), vmem_limit_bytes=V7X_VMEM_LIMIT),
        name="hgrn_prompt",
    )(sq.reshape(n, t, width), g.reshape(n, t, width), kf.reshape(n, t, width), iv.reshape(n, t, width),
      tri, masks, signs, og.reshape(1, HGRN_HEAD))
    return o.reshape(n * t, width), s


def _hgrn_sample_kernel(sq_ref, g_ref, kf_ref, iv_ref, s0_ref, tri_ref, masks_ref, signs_ref, og_ref, o_ref, s_ref,
                        *, c, n_heads):
    seqs = [(sq_ref[b], g_ref[b], kf_ref[b], iv_ref[b], [s0_ref[b, h] for h in range(n_heads)])
            for b in range(sq_ref.shape[0])]
    res = _hgrn_chunks(seqs, tri_ref, masks_ref, signs_ref, og_ref, c)
    for b, (outs, new_state) in enumerate(res):
        for h in range(n_heads):
            s_ref[b, h] = new_state[h]
            o_ref[b, :, h * HGRN_HEAD:(h + 1) * HGRN_HEAD] = outs[h].astype(o_ref.dtype)


def _hgrn_sample_call(sq, g, kf, iv, s0, og, n, t, n_heads):
    width = n_heads * HGRN_HEAD
    bn = min(HGRN_SAMPLE_SEQS, n)
    assert n % bn == 0 and t % 8 == 0
    tri, masks, signs = _hgrn_tables(t)
    tok = pl.BlockSpec((bn, t, width), lambda i: (i, 0, 0))
    st = pl.BlockSpec((bn, n_heads, HGRN_HEAD, HGRN_HEAD), lambda i: (i, 0, 0, 0))
    o, s = pl.pallas_call(
        functools.partial(_hgrn_sample_kernel, c=t, n_heads=n_heads),
        grid=(n // bn,),
        in_specs=[tok, tok, tok, tok, st, _resident(tri.shape), _resident(masks.shape), _resident(signs.shape),
                  _resident((1, HGRN_HEAD))],
        out_specs=[tok, st],
        out_shape=[jax.ShapeDtypeStruct((n, t, width), BF16),
                   jax.ShapeDtypeStruct((n, n_heads, HGRN_HEAD, HGRN_HEAD), F32)],
        compiler_params=pltpu.CompilerParams(dimension_semantics=("arbitrary",), vmem_limit_bytes=V7X_VMEM_LIMIT),
        name="hgrn_sample",
    )(sq.reshape(n, t, width), g.reshape(n, t, width), kf.reshape(n, t, width), iv.reshape(n, t, width),
      s0, tri, masks, signs, og.reshape(1, HGRN_HEAD))
    return o.reshape(n * t, width), s


def _load_tokens(ref, scr):
    bs, dil, rows, width = ref.shape
    if dil == 1:
        return ref[...].reshape(bs * rows, width).astype(F32)
    for r in range(dil):
        for c in range(width // LANES):
            scr[c, pl.ds(r, rows, stride=dil), :] = ref[0, r, :, c * LANES:(c + 1) * LANES].astype(F32)
    return jnp.concatenate([scr[c] for c in range(width // LANES)], axis=1)


def _out_kernel(*refs, n_att):
    att = refs[:2 * n_att] if n_att > 1 else refs[:1]
    n_in = len(att)
    sza_ref, ob_ref, szb_ref, sga_ref, sgb_ref, x_ref, gate_ref, ex_ref, wa_ref, wb_ref, wo_ref, y_ref, scr = refs[n_in:]
    if n_att > 1:
        ls_ = [_load_tokens(r, scr) for r in att[n_att:]]
        m = functools.reduce(jnp.maximum, ls_)
        es = [jnp.exp2(l - m) for l in ls_]
        inv = 1.0 / functools.reduce(lambda a, b: a + b, es)
        alphas = [_dot((e * inv).astype(BF16), ex_ref[...]) for e in es]
        o_a = functools.reduce(lambda a, b: a + b, [al * _load_tokens(r, scr) for al, r in zip(alphas, att[:n_att])])
    else:
        o_a = _load_tokens(att[0], scr)
    branch_a = _dot((o_a * sza_ref[...].astype(F32)).astype(BF16), wa_ref[...])
    branch_b = _dot((ob_ref[...].astype(F32) * szb_ref[...].astype(F32)).astype(BF16), wb_ref[...])
    merged = sga_ref[...].astype(F32) * branch_a + sgb_ref[...].astype(F32) * branch_b
    upd = _dot(merged.astype(BF16), wo_ref[...])
    x = x_ref[...]
    y_ref[...] = x + gate_ref[...] * upd.reshape(x.shape)


def _out_call(att_inputs, sza, ob, szb, sga, sgb, x, gate, wa_bf, wb_bf, wo_bf):
    s, t, d = x.shape
    bs, bt = _token_tiling(s, t)
    tm = bs * bt
    n_t = t // bt

    def tok_spec(a):
        return pl.BlockSpec((tm, a.shape[1]), lambda i, j: (i * n_t + j, 0))

    def att_spec(a):
        dil = a.shape[1]
        assert dil == 1 or bs == 1
        return pl.BlockSpec((bs, dil, bt // dil, a.shape[3]), lambda i, j: (i, 0, j, 0))

    n_att = (len(att_inputs) + 1) // 2
    toks = [sza, ob, szb, sga, sgb]
    width = att_inputs[0].shape[3]
    lane_head = np.arange(LANES)[:, None] == (np.arange(width) // HEAD_DIM)[None, :]
    expand = jnp.asarray(lane_head, BF16)
    return pl.pallas_call(
        functools.partial(_out_kernel, n_att=n_att),
        grid=(s // bs, n_t),
        in_specs=[att_spec(a) for a in att_inputs] + [tok_spec(a) for a in toks]
        + [pl.BlockSpec((bs, bt, d), lambda i, j: (i, j, 0)),
           pl.BlockSpec((bs, 1, d), lambda i, j: (i, 0, 0)),
           _resident(expand.shape), _resident(wa_bf.shape), _resident(wb_bf.shape), _resident(wo_bf.shape)],
        out_specs=pl.BlockSpec((bs, bt, d), lambda i, j: (i, j, 0)),
        out_shape=jax.ShapeDtypeStruct((s, t, d), F32),
        scratch_shapes=[pltpu.VMEM((width // LANES, tm, LANES), F32)],
        compiler_params=pltpu.CompilerParams(dimension_semantics=("arbitrary", "arbitrary"),
                                             vmem_limit_bytes=V7X_VMEM_LIMIT),
        name="out",
    )(*att_inputs, *toks, x, gate, expand, wa_bf, wb_bf, wo_bf)


def _kvwin_kernel(k_ref, v_ref, o_ref, scr):
    o_ref[0, 0] = _load_tokens(k_ref, scr).T
    o_ref[0, 1] = _load_tokens(v_ref, scr).T


def _kvwin_call(k, v, w):
    n, dil, sub, width = k.shape
    blk = min(w, PROJ_TOKENS)
    assert w % blk == 0 and (sub * dil - w) % blk == 0 and blk % dil == 0
    first = (sub * dil - w) // blk
    src = pl.BlockSpec((1, dil, blk // dil, width), lambda b, i: (b, 0, first + i, 0))
    return pl.pallas_call(
        _kvwin_kernel,
        grid=(n, w // blk),
        in_specs=[src, src],
        out_specs=pl.BlockSpec((1, 2, width, blk), lambda b, i: (b, 0, 0, i)),
        out_shape=jax.ShapeDtypeStruct((n, 2, width, w), F32),
        scratch_shapes=[pltpu.VMEM((width // LANES, blk, LANES), F32)],
        compiler_params=pltpu.CompilerParams(dimension_semantics=("arbitrary", "arbitrary")),
        name="kvwin",
    )(k, v)


def _kvnew_kernel(*refs):
    n_g = len(refs) // 3
    for g in range(n_g):
        for kv in range(2):
            src = refs[kv * n_g + g]
            for t in range(src.shape[1]):
                refs[2 * n_g + g][t, kv] = src[:, t, :].T


def _kvnew_call(ks, vs):
    n, t, width = ks[0].shape
    return pl.pallas_call(
        _kvnew_kernel,
        out_shape=[jax.ShapeDtypeStruct((t, 2, width, n), F32) for _ in ks],
        compiler_params=pltpu.CompilerParams(vmem_limit_bytes=V7X_VMEM_LIMIT),
        name="kvnew",
    )(*ks, *vs)


def kernel(x_prompt, x_sample, cache_kv_g0, cache_kv_g1, cache_kv_g2, state_hgrn, c_prompt, c_sample, norm_gain, w_ada, b_ada, w_in, q_norm_gain, k_norm_gain, hgrn_lb_logits, hgrn_out_norm_gain, w_branch_a, w_branch_b, w_out):
    depth = w_in.shape[0]
    assert depth == 1, "single-layer step"
    n_p, t_p, d = x_prompt.shape
    n_s, t_s, _ = x_sample.shape
    a_out = w_branch_a.shape[1]
    n_heads = a_out // HEAD_DIM
    b_f = w_branch_b.shape[1]
    b_heads = b_f // HGRN_HEAD
    slopes = _alibi_slopes(n_heads)

    n_c = n_p + n_s
    pad = (-n_c) % 8
    c_all = jnp.concatenate([c_prompt, c_sample, jnp.zeros((pad, d), F32)], axis=0)
    ada = _ada_call(c_all, w_ada[0], b_ada[0])
    shift, scale, gate = ada[:, :d], ada[:, d:2 * d], ada[:, 2 * d:]

    w_in_bf = w_in[0].astype(BF16)
    wa_bf, wb_bf, wo_bf = w_branch_a[0].astype(BF16), w_branch_b[0].astype(BF16), w_out[0].astype(BF16)
    qg = (jnp.tile(q_norm_gain[0], n_heads) * (ATTN_SCALE * LOG2E)).reshape(1, a_out)
    kg = jnp.tile(k_norm_gain[0], n_heads).reshape(1, a_out)
    og = hgrn_out_norm_gain[0]

    def cond(lo, n):
        return [a[lo:lo + n].reshape(n, 1, d) for a in (shift, scale, gate)]

    sh, sc, gt_p = cond(0, n_p)
    outs = _proj_call(x_prompt, sh, sc, norm_gain[0], w_in_bf, qg, kg, hgrn_lb_logits, n_groups=N_GROUPS,
                      dils=DILATIONS, a_out=a_out, b_f=b_f, kv_dtype=BF16)
    qs, ks, vs = outs[0:N_GROUPS], outs[N_GROUPS:2 * N_GROUPS], outs[2 * N_GROUPS:3 * N_GROUPS]
    act_p = outs[3 * N_GROUPS:]
    sh, sc, gt_s = cond(n_p, n_s)
    outs = _proj_call(x_sample, sh, sc, norm_gain[0], w_in_bf, qg, kg, hgrn_lb_logits, n_groups=N_GROUPS,
                      dils=(1,) * N_GROUPS, a_out=a_out, b_f=b_f, kv_dtype=F32)
    qn, kn, vn = [[a.reshape(n_s, t_s, a_out) for a in outs[i * N_GROUPS:(i + 1) * N_GROUPS]] for i in range(3)]
    act_s = outs[3 * N_GROUPS:]

    steps = [_attn_steps(qs[gi])[2] for gi in range(N_GROUPS)]
    assert n_s % sum(steps) == 0, "sample sequences must spread evenly over the prompt attention grid steps"
    n_seq = n_s // sum(steps)
    sample = (qn, kn, vn, (cache_kv_g0[0], cache_kv_g1[0], cache_kv_g2[0]))
    att = [_attn_call(qs[gi], ks[gi], vs[gi], gi, n_heads, slopes, sample, sum(steps[:gi]), n_seq)
           for gi in range(N_GROUPS)]
    o_a = jnp.concatenate([a[2] for a in att], axis=0)

    sza, sqb, g, kf, ib, szb, sga, sgb = act_p
    ob, hgrn_p = _hgrn_prompt_call(sqb, g, kf, ib, og, n_p, t_p, b_heads)
    y_p = _out_call([a[0] for a in att] + [a[1] for a in att], sza, ob, szb, sga, sgb, x_prompt, gt_p,
                    wa_bf, wb_bf, wo_bf)
    kv_p = []
    for gi in range(N_GROUPS):
        w = min(WINDOWS[gi], t_p)
        rows = _kvwin_call(ks[gi], vs[gi], w).reshape(n_p, 2, n_heads, HEAD_DIM, w)
        kv_p.append(jnp.transpose(rows, (0, 4, 1, 2, 3))[None])

    sza, sqb, g, kf, ib, szb, sga, sgb = act_s
    ob, hgrn_s = _hgrn_sample_call(sqb, g, kf, ib, state_hgrn[0], og, n_s, t_s, b_heads)
    y_s = _out_call([o_a.reshape(n_s, 1, t_s, a_out)], sza, ob, szb, sga, sgb, x_sample, gt_s, wa_bf, wb_bf, wo_bf)
    kv_s = [jnp.transpose(a.reshape(t_s, 2, n_heads, HEAD_DIM, n_s), (4, 0, 1, 2, 3))[None]
            for a in _kvnew_call(kn, vn)]

    return (y_p, y_s, kv_p[0], kv_p[1], kv_p[2], hgrn_p[None], kv_s[0], kv_s[1], kv_s[2], hgrn_s[None])
```

```python
import functools

import numpy as np
import jax
import jax.numpy as jnp
from jax import lax
from jax.experimental import pallas as pl
from jax.experimental.pallas import tpu as pltpu

F32 = jnp.float32
BF16 = jnp.bfloat16

HEAD_DIM = 64
N_GROUPS = 3
WINDOWS = (128, 512, 2048)
DILATIONS = (1, 4, 16)
N_BACK = 128
ATTN_BLOCK = 128
ATTN_BLOCKS_PER_STEP = 4
ATTN_SCALE = HEAD_DIM ** -0.5
HGRN_HEAD = 128
LANES = 128
LOG2E = 1.4426950408889634
EPS = 1e-6
NEG = -1e30

V7X_VMEM_LIMIT = 56 * 1024 * 1024
PROJ_TOKENS = 512
HGRN_CHUNK = 64
HGRN_MAX_SEQS = 4
HGRN_SAMPLE_SEQS = 8


def _sigmoid(x):
    return 1.0 / (1.0 + jnp.exp(-x))


def _silu(x):
    return x * _sigmoid(x)


def _dot(a, b):
    return jnp.dot(a, b, preferred_element_type=F32)


def _dot_nt(a, b):
    return lax.dot_general(a, b, (((1,), (1,)), ((), ())), preferred_element_type=F32)


def _dot_tn(a, b):
    return lax.dot_general(a, b, (((0,), (0,)), ((), ())), preferred_element_type=F32)


def _alibi_slopes(n_heads_per_group):
    a_heads = N_GROUPS * n_heads_per_group
    s = 2.0 ** (-8.0 * (np.arange(a_heads) + 1) / a_heads)
    return s.astype(np.float32).reshape(N_GROUPS, n_heads_per_group)


def _resident(shape):
    nd = len(shape)
    return pl.BlockSpec(shape, lambda *_: (0,) * nd, pipeline_mode=pl.Buffered(1))


def _ada_kernel(c_ref, w_ref, b_ref, o_ref):
    o_ref[...] = _dot(_silu(c_ref[...]).astype(BF16), w_ref[...]) + b_ref[...]


def _ada_call(c, w_ada, b_ada):
    n, d = c.shape
    cols = w_ada.shape[1]
    blk = d
    return pl.pallas_call(
        _ada_kernel,
        grid=(cols // blk,),
        in_specs=[pl.BlockSpec((n, d), lambda j: (0, 0)),
                  pl.BlockSpec((d, blk), lambda j: (0, j)),
                  pl.BlockSpec((1, blk), lambda j: (0, j))],
        out_specs=pl.BlockSpec((n, blk), lambda j: (0, j)),
        out_shape=jax.ShapeDtypeStruct((n, cols), F32),
        name="ada",
    )(c, w_ada.astype(BF16), b_ada.reshape(1, cols))


def _store_dilated(out_ref, val, scr, dil):
    bs, d, rows, width = out_ref.shape
    if dil == 1:
        out_ref[...] = val.reshape(bs, 1, rows, width).astype(out_ref.dtype)
        return
    for c in range(width // LANES):
        scr[c] = val[:, c * LANES:(c + 1) * LANES]
    for r in range(dil):
        for c in range(width // LANES):
            out_ref[0, r, :, c * LANES:(c + 1) * LANES] = scr[c, pl.ds(r, rows, stride=dil), :].astype(out_ref.dtype)


def _proj_kernel(x_ref, shift_ref, scale_ref, ng_ref, w_ref, qg_ref, kg_ref, lbl_ref, *refs,
                 n_groups, dils, a_out, d_model):
    qkv_refs = refs[:3 * n_groups]
    sza_ref, sqb_ref, g_ref, kf_ref, ib_ref, szb_ref, sga_ref, sgb_ref, scr = refs[3 * n_groups:]
    x = x_ref[...]
    ms = jnp.mean(x * x, axis=-1, keepdims=True)
    h = x * lax.rsqrt(ms + EPS) * ng_ref[...]
    h = h * (1.0 + scale_ref[...]) + shift_ref[...]
    tm = h.shape[0] * h.shape[1]
    hb = h.reshape(tm, d_model).astype(BF16)
    tile = a_out

    def proj(c0):
        return _dot(hb, w_ref[:, c0:c0 + tile])

    def head_norm(u, gain):
        u2 = u * u
        low = lax.broadcasted_iota(jnp.int32, (tm, LANES), 1) < HEAD_DIM
        outs = []
        for c in range(tile // LANES):
            s = u2[:, c * LANES:(c + 1) * LANES]
            lo = jnp.sum(jnp.where(low, s, 0.0), axis=-1, keepdims=True)
            hi = jnp.sum(jnp.where(low, 0.0, s), axis=-1, keepdims=True)
            r_lo = lax.rsqrt(lo * (1.0 / HEAD_DIM) + EPS)
            r_hi = lax.rsqrt(hi * (1.0 / HEAD_DIM) + EPS)
            outs.append(u[:, c * LANES:(c + 1) * LANES] * jnp.where(low, r_lo, r_hi))
        return jnp.concatenate(outs, axis=1) * gain

    c0 = 0
    for kind in range(3):
        for g in range(n_groups):
            u = proj(c0)
            if kind == 0:
                u = head_norm(u, qg_ref[...])
            elif kind == 1:
                u = head_norm(u, kg_ref[...])
            _store_dilated(qkv_refs[kind * n_groups + g], u, scr, dils[g])
            c0 += tile
    sza_ref[...] = _silu(proj(c0)).astype(BF16)
    c0 += tile
    sqb_ref[...] = _silu(proj(c0)).astype(BF16)
    c0 += tile
    lbl = lbl_ref[...]
    lbe = jnp.exp(lbl - jnp.max(lbl, axis=0, keepdims=True))
    lb = lbe[0:1] / jnp.sum(lbe, axis=0, keepdims=True)
    fr = proj(c0)
    g_ref[...] = jnp.log2(lb + (1.0 - lb) * _sigmoid(fr))
    kf_ref[...] = ((1.0 - lb) * _sigmoid(-fr)).astype(BF16)
    c0 += tile
    ib_ref[...] = proj(c0).astype(BF16)
    c0 += tile
    szb_ref[...] = _silu(proj(c0)).astype(BF16)
    c0 += tile
    for j in range(d_model // tile):
        sga_ref[:, j * tile:(j + 1) * tile] = _sigmoid(proj(c0)).astype(BF16)
        c0 += tile
    for j in range(d_model // tile):
        sgb_ref[:, j * tile:(j + 1) * tile] = _sigmoid(proj(c0)).astype(BF16)
        c0 += tile


def _token_tiling(s, t):
    if t >= PROJ_TOKENS:
        assert t % PROJ_TOKENS == 0
        return 1, PROJ_TOKENS
    bs = min(s, PROJ_TOKENS // t)
    assert s % bs == 0 and t % 8 == 0
    return bs, t


def _proj_call(x, shift, scale, norm_gain, w_in_bf, qg, kg, lb_logits, *, n_groups, dils, a_out, b_f, kv_dtype):
    s, t, d = x.shape
    bs, bt = _token_tiling(s, t)
    tm = bs * bt
    n_t = t // bt
    ntok = s * t
    grid = (s // bs, n_t)
    assert all(dl == 1 or (bs == 1 and bt % (dl * 16) == 0) for dl in dils)

    def tok_spec(cols):
        return pl.BlockSpec((tm, cols), lambda i, j: (i * n_t + j, 0))

    qkv_dtypes = [BF16] * n_groups + [kv_dtype] * (2 * n_groups)
    qkv_shapes = [jax.ShapeDtypeStruct((s, dils[g], t // dils[g], a_out), dt)
                  for g, dt in zip(list(range(n_groups)) * 3, qkv_dtypes)]
    qkv_specs = [pl.BlockSpec((bs, dils[g], bt // dils[g], a_out), lambda i, j: (i, 0, j, 0))
                 for g in list(range(n_groups)) * 3]
    out_cols = [(a_out, BF16), (b_f, BF16), (b_f, F32), (b_f, BF16), (b_f, BF16), (b_f, BF16), (d, BF16), (d, BF16)]
    kern = functools.partial(_proj_kernel, n_groups=n_groups, dils=tuple(dils), a_out=a_out, d_model=d)
    return pl.pallas_call(
        kern,
        grid=grid,
        in_specs=[pl.BlockSpec((bs, bt, d), lambda i, j: (i, j, 0)),
                  pl.BlockSpec((bs, 1, d), lambda i, j: (i, 0, 0)),
                  pl.BlockSpec((bs, 1, d), lambda i, j: (i, 0, 0)),
                  _resident((1, d)),
                  _resident(w_in_bf.shape),
                  _resident(qg.shape),
                  _resident(kg.shape),
                  _resident(lb_logits.shape)],
        out_specs=qkv_specs + [tok_spec(c) for c, _ in out_cols],
        out_shape=qkv_shapes + [jax.ShapeDtypeStruct((ntok, c), dt) for c, dt in out_cols],
        scratch_shapes=[pltpu.VMEM((a_out // LANES, tm, LANES), F32)],
        compiler_params=pltpu.CompilerParams(dimension_semantics=("arbitrary", "arbitrary"),
                                             vmem_limit_bytes=V7X_VMEM_LIMIT),
        name="proj",
    )(x, shift, scale, norm_gain.reshape(1, d), w_in_bf, qg, kg, lb_logits)


def _prompt_attn(q_ref, kp_ref, kc_ref, vp_ref, vc_ref, bias_ref, o_ref, lse_ref, *, n_heads, n_blk):
    first = (pl.program_id(2) == 0).astype(jnp.int32)
    lane = lax.broadcasted_iota(jnp.int32, (ATTN_BLOCK, 2 * HEAD_DIM), 1)
    low = lane < HEAD_DIM
    for blk in range(n_blk):
        rows = slice(blk * ATTN_BLOCK, (blk + 1) * ATTN_BLOCK)
        before = slice((blk - 1) * ATTN_BLOCK, blk * ATTN_BLOCK)
        table = first if blk == 0 else 0
        scores, values = [], []
        for hp in range(n_heads // 2):
            sl = slice(hp * 2 * HEAD_DIM, (hp + 1) * 2 * HEAD_DIM)
            q2 = q_ref[rows, sl]
            k_prev = kp_ref[:, sl] if blk == 0 else kc_ref[before, sl]
            v_prev = vp_ref[:, sl] if blk == 0 else vc_ref[before, sl]
            k2 = jnp.concatenate([k_prev, kc_ref[rows, sl]], axis=0)
            values.append(jnp.concatenate([v_prev, vc_ref[rows, sl]], axis=0))
            for par in range(2):
                keep = low if par == 0 else jnp.logical_not(low)
                qm = jnp.where(keep, q2, jnp.zeros_like(q2))
                scores.append(_dot_nt(qm, k2) + bias_ref[table, 2 * hp + par])
        soft = []
        for s in scores:
            m = jnp.max(s, axis=-1, keepdims=True)
            p = jnp.exp2(s - m)
            soft.append((p.astype(BF16), m, jnp.sum(p, axis=-1, keepdims=True)))
        outs = [_dot(p, values[i // 2]) for i, (p, m, l) in enumerate(soft)]
        lse = jnp.zeros((ATTN_BLOCK, LANES), F32)
        for hp in range(n_heads // 2):
            sl = slice(hp * 2 * HEAD_DIM, (hp + 1) * 2 * HEAD_DIM)
            (_, m0, l0), (_, m1, l1) = soft[2 * hp], soft[2 * hp + 1]
            o_ref[rows, sl] = jnp.where(low, outs[2 * hp] / l0, outs[2 * hp + 1] / l1).astype(o_ref.dtype)
            lse = jnp.where(lane == 2 * hp, m0 + jnp.log2(l0), lse)
            lse = jnp.where(lane == 2 * hp + 1, m1 + jnp.log2(l1), lse)
        lse_ref[rows, :] = lse


def _sample_attn(qs, ks, vs, caches, biases, bn_ref, o_ref, seq, *, n_heads):
    per_head = []
    for h in range(n_heads):
        scores, values = [], []
        hs = slice(h * HEAD_DIM, (h + 1) * HEAD_DIM)
        for g in range(N_GROUPS):
            qh = qs[g][seq, :, hs]
            scores.append(_dot(qh, caches[g][seq, 0, h].astype(BF16)) + biases[g][h])
            values.append(caches[g][seq, 1, h].astype(BF16))
            scores.append(_dot_nt(qh, ks[g][seq, :, hs].astype(BF16)) + bn_ref[g, h])
            values.append(vs[g][seq, :, hs].astype(BF16))
        per_head.append((scores, values))
    soft = []
    for scores, values in per_head:
        m = functools.reduce(jnp.maximum, [jnp.max(s, axis=-1, keepdims=True) for s in scores])
        ps = [jnp.exp2(s - m) for s in scores]
        l = functools.reduce(lambda a, b: a + b, [jnp.sum(p, axis=-1, keepdims=True) for p in ps])
        soft.append(([p.astype(BF16) for p in ps], l))
    for h, ((ps, l), (_, values)) in enumerate(zip(soft, per_head)):
        parts = [(_dot_nt(p, v) if i % 2 == 0 else _dot(p, v)) for i, (p, v) in enumerate(zip(ps, values))]
        acc = functools.reduce(lambda a, b: a + b, parts)
        o_ref[seq, :, h * HEAD_DIM:(h + 1) * HEAD_DIM] = (acc / l).astype(o_ref.dtype)


def _attn_kernel(*refs, n_heads, n_blk, n_seq, hgrn):
    prompt_in, refs = refs[:6], refs[6:]
    qs, ks, vs = refs[0:N_GROUPS], refs[N_GROUPS:2 * N_GROUPS], refs[2 * N_GROUPS:3 * N_GROUPS]
    caches = refs[3 * N_GROUPS:4 * N_GROUPS]
    biases = refs[4 * N_GROUPS:5 * N_GROUPS]
    bn_ref, refs = refs[5 * N_GROUPS], refs[5 * N_GROUPS + 1:]
    if hgrn is not None:
        rec_in, refs = refs[:8], refs[8:]
        o_ref, lse_ref, os_ref, ob_ref, st_ref = refs
    else:
        o_ref, lse_ref, os_ref = refs
    for seq in range(n_seq):
        _sample_attn(qs, ks, vs, caches, biases, bn_ref, os_ref, seq, n_heads=n_heads)
    if hgrn is not None:
        first = (pl.program_id(0) == 0) & (pl.program_id(1) == 0) & (pl.program_id(2) == 0)
        _hgrn_prompt_part(*rec_in, ob_ref, st_ref, first, c=hgrn[0], n_heads=hgrn[1])
    _prompt_attn(*prompt_in, o_ref, lse_ref, n_heads=n_heads, n_blk=n_blk)


def _attn_tables(dil, slopes_g):
    i = np.arange(ATTN_BLOCK)[:, None]
    j = np.arange(2 * ATTN_BLOCK)[None, :]
    delta = i + ATTN_BLOCK - j
    valid = (delta >= 0) & (delta <= N_BACK)
    dist = (delta * dil).astype(np.float32)
    bias = np.where(valid[None], -slopes_g[:, None, None] * dist[None] * LOG2E, NEG).astype(np.float32)
    first = np.where((j >= ATTN_BLOCK)[None], bias, NEG).astype(np.float32)
    return jnp.asarray(np.stack([bias, first]))


def _sattn_tables(n_heads, t_new, slopes, wbs):
    t = np.arange(t_new)[:, None]
    biases, bn = [], []
    for g in range(N_GROUPS):
        d, wb = DILATIONS[g], wbs[g]
        dist = wb + t - np.arange(wb)[None, :]
        valid = (dist % d == 0) & (dist // d <= N_BACK)
        biases.append(np.where(valid[None], -slopes[g][:, None, None] * dist[None] * LOG2E, NEG).astype(np.float32))
        dist = t - np.arange(t_new)[None, :]
        valid = (dist >= 0) & (dist % d == 0) & (dist // d <= N_BACK)
        bn.append(np.where(valid[None], -slopes[g][:, None, None] * dist[None] * LOG2E, NEG).astype(np.float32))
    return [jnp.asarray(b) for b in biases], jnp.asarray(np.stack(bn))


def _attn_steps(q):
    n, dil, sub, _ = q.shape
    nb = sub // ATTN_BLOCK
    n_blk = min(ATTN_BLOCKS_PER_STEP, nb)
    assert sub % ATTN_BLOCK == 0 and nb % n_blk == 0
    return n_blk, nb // n_blk, n * dil * (nb // n_blk)


def _attn_call(q, k, v, g, n_heads, slopes, sample, first_block, n_seq, rec=None):
    n, dil, sub, width = q.shape
    assert dil == DILATIONS[g] and width == n_heads * HEAD_DIM and n_heads <= LANES
    n_blk, steps_i, steps = _attn_steps(q)
    cur = pl.BlockSpec((None, None, n_blk * ATTN_BLOCK, width), lambda b, r, i: (b, r, i, 0))
    prev = pl.BlockSpec((None, None, ATTN_BLOCK, width), lambda b, r, i: (b, r, jnp.maximum(i * n_blk - 1, 0), 0))
    lse = pl.BlockSpec((None, None, n_blk * ATTN_BLOCK, LANES), lambda b, r, i: (b, r, i, 0))
    bias = _attn_tables(dil, slopes[g])

    qs, ks, vs, caches = sample
    t_new = qs[0].shape[1]
    wbs = [c.shape[1] for c in caches]
    sbiases, bn = _sattn_tables(n_heads, t_new, slopes, wbs)
    views = [jnp.transpose(c, (0, 2, 3, 4, 1)) for c in caches]

    def step(b, r, i):
        return (b * dil + r) * steps_i + i

    new = pl.BlockSpec((n_seq, t_new, width), lambda b, r, i: (first_block + step(b, r, i), 0, 0))
    cache_specs = [pl.BlockSpec((n_seq, 2, n_heads, HEAD_DIM, wb), lambda b, r, i: (first_block + step(b, r, i), 0, 0, 0, 0))
                   for wb in wbs]
    in_specs = ([cur, prev, cur, prev, cur, _resident(bias.shape)] + [new] * (3 * N_GROUPS) + cache_specs
                + [_resident(b.shape) for b in sbiases] + [_resident(bn.shape)])
    out_specs = [cur, lse, pl.BlockSpec((n_seq, t_new, width), lambda b, r, i: (step(b, r, i), 0, 0))]
    out_shape = [jax.ShapeDtypeStruct(q.shape, BF16), jax.ShapeDtypeStruct((n, dil, sub, LANES), F32),
                 jax.ShapeDtypeStruct((steps * n_seq, t_new, width), BF16)]
    operands = [q, k, k, v, v, bias, *qs, *ks, *vs, *views, *sbiases, bn]
    hgrn = None
    if rec is not None:
        sq, g_log, kf, iv, og, rec_heads = rec
        n_r, t_r, width_r = sq.shape
        c = min(HGRN_CHUNK, t_r)
        assert t_r % steps == 0 and (t_r // steps) % c == 0 and n_r <= HGRN_MAX_SEQS
        tri, masks, signs = _hgrn_tables(c)
        tok = pl.BlockSpec((n_r, t_r // steps, width_r), lambda b, r, i: (0, step(b, r, i), 0))
        in_specs += [tok] * 4 + [_resident(tri.shape), _resident(masks.shape), _resident(signs.shape),
                                 _resident((1, HGRN_HEAD))]
        out_specs += [tok, pl.BlockSpec((n_r, rec_heads, HGRN_HEAD, HGRN_HEAD), lambda b, r, i: (0, 0, 0, 0))]
        out_shape += [jax.ShapeDtypeStruct((n_r, t_r, width_r), BF16),
                      jax.ShapeDtypeStruct((n_r, rec_heads, HGRN_HEAD, HGRN_HEAD), F32)]
        operands += [sq, g_log, kf, iv, tri, masks, signs, og.reshape(1, HGRN_HEAD)]
        hgrn = (c, rec_heads)
    return pl.pallas_call(
        functools.partial(_attn_kernel, n_heads=n_heads, n_blk=n_blk, n_seq=n_seq, hgrn=hgrn),
        grid=(n, dil, steps_i),
        in_specs=in_specs,
        out_specs=out_specs,
        out_shape=out_shape,
        compiler_params=pltpu.CompilerParams(dimension_semantics=("arbitrary", "arbitrary", "arbitrary"),
                                             vmem_limit_bytes=V7X_VMEM_LIMIT),
        name=f"attn_g{g}",
    )(*operands)


def _hgrn_tables(c):
    t = np.arange(c)
    tri = (t[None, :] <= t[:, None]).astype(np.float32)
    masks = [np.eye(c, dtype=bool)]
    signs = []
    w = c // 2
    while w >= 1:
        blk = t // (2 * w)
        late = (t // w) % 2 == 1
        masks.append((blk[:, None] == blk[None, :]) & late[:, None] & ~late[None, :])
        if w < 8:
            signs.append(np.where(late, 1.0, -1.0))
        w //= 2
    masks = np.stack(masks).astype(np.float32)
    signs = np.repeat(np.stack(signs).reshape(-1, 1), HGRN_HEAD, axis=1).astype(np.float32)
    return jnp.asarray(tri, BF16), jnp.asarray(masks), jnp.asarray(signs)


def _level_exponents(b, g, c):
    width = b.shape[1]
    coarse, fine = [], []
    w = c // 2
    while w >= 8:
        parts = []
        for k in range(c // (2 * w)):
            lo, mid, hi = k * 2 * w, k * 2 * w + w, (k + 1) * 2 * w
            ref = b[mid - 1:mid]
            parts.append((slice(lo, mid), False, ref - b[lo:mid]))
            parts.append((slice(mid, hi), True, b[mid:hi] - ref))
        coarse.append(parts)
        w //= 2
    b3 = b.reshape(c // 8, 8, width)
    sub = lax.broadcasted_iota(jnp.int32, b3.shape, 1)
    if c >= 8:
        fine.append((b3 - jnp.broadcast_to(b3[:, 3:4], b3.shape)).reshape(c, width))
    ref = jnp.where(sub < 4, jnp.broadcast_to(b3[:, 1:2], b3.shape), jnp.broadcast_to(b3[:, 5:6], b3.shape))
    fine.append((b3 - ref).reshape(c, width))
    row = lax.broadcasted_iota(jnp.int32, b.shape, 0)
    fine.append(jnp.where(row % 2 == 1, g, 0.0))
    return coarse, fine


def _hgrn_chunks(seqs, tri, masks, signs, og, c):
    n_lev = masks.shape[0] - 1
    tri_v = tri[...]
    og = og[...]
    pre = []
    for sq, g, kf, iv, state in seqs:
        g1 = g.astype(BF16)
        r1 = g - g1.astype(F32)
        g2 = r1.astype(BF16)
        g3 = (r1 - g2.astype(F32)).astype(BF16)
        b = _dot(tri_v, g1) + _dot(tri_v, g2) + _dot(tri_v, g3)
        pre.append((b, _level_exponents(b, g, c)))
    heads = []
    for (sq, g, kf, iv, state), (b, (coarse, fine)) in zip(seqs, pre):
        e_cum = jnp.exp2(b)
        e_tail = jnp.exp2(b[c - 1:c] - b)
        for h in range(len(state)):
            hs = slice(h * HGRN_HEAD, (h + 1) * HGRN_HEAD)
            sqh, kfh, ivh = sq[:, hs], kf[:, hs], iv[:, hs]
            sqf, kff = sqh.astype(F32), kfh.astype(F32)
            eb = e_cum[:, hs]
            prods = [_dot_nt(sqh, kfh)]
            for parts in coarse:
                x = jnp.concatenate([(sqf if late else kff)[rows] * jnp.exp2(e[:, hs]) for rows, late, e in parts],
                                    axis=0).astype(BF16)
                prods.append(_dot_nt(x, x))
            for lv, e in enumerate(fine):
                sgn = signs[lv * c:(lv + 1) * c]
                x = (jnp.where(sgn > 0.0, sqf, kff) * jnp.exp2(e[:, hs] * sgn)).astype(BF16)
                prods.append(_dot_nt(x, x))
            o_inter = _dot((sqf * eb).astype(BF16), state[h].astype(BF16))
            kt = (kff * e_tail[:, hs]).astype(BF16)
            heads.append((prods, o_inter, _dot_tn(kt, ivh), eb, ivh, state[h]))
    results = []
    for prods, o_inter, upd, eb, ivh, st in heads:
        att = masks[0] * prods[0]
        for lv in range(n_lev):
            att = att + masks[1 + lv] * prods[1 + lv]
        o = o_inter + _dot(att.astype(BF16), ivh)
        decay = jnp.broadcast_to(eb[c - 1:c, :], (HGRN_HEAD, HGRN_HEAD)).T
        results.append((o * lax.rsqrt(jnp.mean(o * o, axis=-1, keepdims=True) + EPS) * og, decay * st + upd))
    n_h = len(seqs[0][4])
    return [([r[0] for r in results[i * n_h:(i + 1) * n_h]], [r[1] for r in results[i * n_h:(i + 1) * n_h]])
            for i in range(len(seqs))]


def _hgrn_prompt_part(sq_ref, g_ref, kf_ref, iv_ref, tri_ref, masks_ref, signs_ref, og_ref, o_ref, s_ref, first,
                      *, c, n_heads):
    @pl.when(first)
    def _():
        s_ref[...] = jnp.zeros_like(s_ref)

    def body(ci, carry):
        rows = pl.ds(pl.multiple_of(ci * c, c), c)
        seqs = [(sq_ref[b, rows, :], g_ref[b, rows, :], kf_ref[b, rows, :], iv_ref[b, rows, :],
                 [s_ref[b, h] for h in range(n_heads)]) for b in range(sq_ref.shape[0])]
        res = _hgrn_chunks(seqs, tri_ref, masks_ref, signs_ref, og_ref, c)
        for b, (outs, new_state) in enumerate(res):
            for h in range(n_heads):
                s_ref[b, h] = new_state[h]
                o_ref[b, rows, h * HGRN_HEAD:(h + 1) * HGRN_HEAD] = outs[h].astype(o_ref.dtype)
        return carry

    lax.fori_loop(0, sq_ref.shape[1] // c, body, 0)


def _hgrn_sample_kernel(sq_ref, g_ref, kf_ref, iv_ref, s0_ref, tri_ref, masks_ref, signs_ref, og_ref, o_ref, s_ref,
                        *, c, n_heads):
    seqs = [(sq_ref[b], g_ref[b], kf_ref[b], iv_ref[b], [s0_ref[b, h] for h in range(n_heads)])
            for b in range(sq_ref.shape[0])]
    res = _hgrn_chunks(seqs, tri_ref, masks_ref, signs_ref, og_ref, c)
    for b, (outs, new_state) in enumerate(res):
        for h in range(n_heads):
            s_ref[b, h] = new_state[h]
            o_ref[b, :, h * HGRN_HEAD:(h + 1) * HGRN_HEAD] = outs[h].astype(o_ref.dtype)


def _hgrn_sample_call(sq, g, kf, iv, s0, og, n, t, n_heads):
    width = n_heads * HGRN_HEAD
    bn = min(HGRN_SAMPLE_SEQS, n)
    assert n % bn == 0 and t % 8 == 0
    tri, masks, signs = _hgrn_tables(t)
    tok = pl.BlockSpec((bn, t, width), lambda i: (i, 0, 0))
    st = pl.BlockSpec((bn, n_heads, HGRN_HEAD, HGRN_HEAD), lambda i: (i, 0, 0, 0))
    o, s = pl.pallas_call(
        functools.partial(_hgrn_sample_kernel, c=t, n_heads=n_heads),
        grid=(n // bn,),
        in_specs=[tok, tok, tok, tok, st, _resident(tri.shape), _resident(masks.shape), _resident(signs.shape),
                  _resident((1, HGRN_HEAD))],
        out_specs=[tok, st],
        out_shape=[jax.ShapeDtypeStruct((n, t, width), BF16),
                   jax.ShapeDtypeStruct((n, n_heads, HGRN_HEAD, HGRN_HEAD), F32)],
        compiler_params=pltpu.CompilerParams(dimension_semantics=("arbitrary",), vmem_limit_bytes=V7X_VMEM_LIMIT),
        name="hgrn_sample",
    )(sq.reshape(n, t, width), g.reshape(n, t, width), kf.reshape(n, t, width), iv.reshape(n, t, width),
      s0, tri, masks, signs, og.reshape(1, HGRN_HEAD))
    return o.reshape(n * t, width), s


def _load_tokens(ref, scr):
    bs, dil, rows, width = ref.shape
    if dil == 1:
        return ref[...].reshape(bs * rows, width).astype(F32)
    for r in range(dil):
        for c in range(width // LANES):
            scr[c, pl.ds(r, rows, stride=dil), :] = ref[0, r, :, c * LANES:(c + 1) * LANES].astype(F32)
    return jnp.concatenate([scr[c] for c in range(width // LANES)], axis=1)


def _out_kernel(*refs, n_att):
    att = refs[:2 * n_att] if n_att > 1 else refs[:1]
    n_in = len(att)
    sza_ref, ob_ref, szb_ref, sga_ref, sgb_ref, x_ref, gate_ref, ex_ref, wa_ref, wb_ref, wo_ref, y_ref, scr = refs[n_in:]
    if n_att > 1:
        ls_ = [_load_tokens(r, scr) for r in att[n_att:]]
        m = functools.reduce(jnp.maximum, ls_)
        es = [jnp.exp2(l - m) for l in ls_]
        inv = 1.0 / functools.reduce(lambda a, b: a + b, es)
        alphas = [_dot((e * inv).astype(BF16), ex_ref[...]) for e in es]
        o_a = functools.reduce(lambda a, b: a + b, [al * _load_tokens(r, scr) for al, r in zip(alphas, att[:n_att])])
    else:
        o_a = _load_tokens(att[0], scr)
    branch_a = _dot((o_a * sza_ref[...].astype(F32)).astype(BF16), wa_ref[...])
    branch_b = _dot((ob_ref[...].astype(F32) * szb_ref[...].astype(F32)).astype(BF16), wb_ref[...])
    merged = sga_ref[...].astype(F32) * branch_a + sgb_ref[...].astype(F32) * branch_b
    upd = _dot(merged.astype(BF16), wo_ref[...])
    x = x_ref[...]
    y_ref[...] = x + gate_ref[...] * upd.reshape(x.shape)


def _out_call(att_inputs, sza, ob, szb, sga, sgb, x, gate, wa_bf, wb_bf, wo_bf):
    s, t, d = x.shape
    bs, bt = _token_tiling(s, t)
    tm = bs * bt
    n_t = t // bt

    def tok_spec(a):
        return pl.BlockSpec((tm, a.shape[1]), lambda i, j: (i * n_t + j, 0))

    def att_spec(a):
        dil = a.shape[1]
        assert dil == 1 or bs == 1
        return pl.BlockSpec((bs, dil, bt // dil, a.shape[3]), lambda i, j: (i, 0, j, 0))

    n_att = (len(att_inputs) + 1) // 2
    toks = [sza, ob, szb, sga, sgb]
    width = att_inputs[0].shape[3]
    lane_head = np.arange(LANES)[:, None] == (np.arange(width) // HEAD_DIM)[None, :]
    expand = jnp.asarray(lane_head, BF16)
    return pl.pallas_call(
        functools.partial(_out_kernel, n_att=n_att),
        grid=(s // bs, n_t),
        in_specs=[att_spec(a) for a in att_inputs] + [tok_spec(a) for a in toks]
        + [pl.BlockSpec((bs, bt, d), lambda i, j: (i, j, 0)),
           pl.BlockSpec((bs, 1, d), lambda i, j: (i, 0, 0)),
           _resident(expand.shape), _resident(wa_bf.shape), _resident(wb_bf.shape), _resident(wo_bf.shape)],
        out_specs=pl.BlockSpec((bs, bt, d), lambda i, j: (i, j, 0)),
        out_shape=jax.ShapeDtypeStruct((s, t, d), F32),
        scratch_shapes=[pltpu.VMEM((width // LANES, tm, LANES), F32)],
        compiler_params=pltpu.CompilerParams(dimension_semantics=("arbitrary", "arbitrary"),
                                             vmem_limit_bytes=V7X_VMEM_LIMIT),
        name="out",
    )(*att_inputs, *toks, x, gate, expand, wa_bf, wb_bf, wo_bf)


def _kvwin_kernel(k_ref, v_ref, o_ref, scr):
    o_ref[0, 0] = _load_tokens(k_ref, scr).T
    o_ref[0, 1] = _load_tokens(v_ref, scr).T


def _kvwin_call(k, v, w):
    n, dil, sub, width = k.shape
    blk = min(w, PROJ_TOKENS)
    assert w % blk == 0 and (sub * dil - w) % blk == 0 and blk % dil == 0
    first = (sub * dil - w) // blk
    src = pl.BlockSpec((1, dil, blk // dil, width), lambda b, i: (b, 0, first + i, 0))
    return pl.pallas_call(
        _kvwin_kernel,
        grid=(n, w // blk),
        in_specs=[src, src],
        out_specs=pl.BlockSpec((1, 2, width, blk), lambda b, i: (b, 0, 0, i)),
        out_shape=jax.ShapeDtypeStruct((n, 2, width, w), F32),
        scratch_shapes=[pltpu.VMEM((width // LANES, blk, LANES), F32)],
        compiler_params=pltpu.CompilerParams(dimension_semantics=("arbitrary", "arbitrary")),
        name="kvwin",
    )(k, v)


def _kvnew_kernel(*refs):
    n_g = len(refs) // 3
    for g in range(n_g):
        for kv in range(2):
            src = refs[kv * n_g + g]
            for t in range(src.shape[1]):
                refs[2 * n_g + g][t, kv] = src[:, t, :].T


def _kvnew_call(ks, vs):
    n, t, width = ks[0].shape
    return pl.pallas_call(
        _kvnew_kernel,
        out_shape=[jax.ShapeDtypeStruct((t, 2, width, n), F32) for _ in ks],
        compiler_params=pltpu.CompilerParams(vmem_limit_bytes=V7X_VMEM_LIMIT),
        name="kvnew",
    )(*ks, *vs)


def kernel(x_prompt, x_sample, cache_kv_g0, cache_kv_g1, cache_kv_g2, state_hgrn, c_prompt, c_sample, norm_gain, w_ada, b_ada, w_in, q_norm_gain, k_norm_gain, hgrn_lb_logits, hgrn_out_norm_gain, w_branch_a, w_branch_b, w_out):
    depth = w_in.shape[0]
    assert depth == 1, "single-layer step"
    n_p, t_p, d = x_prompt.shape
    n_s, t_s, _ = x_sample.shape
    a_out = w_branch_a.shape[1]
    n_heads = a_out // HEAD_DIM
    b_f = w_branch_b.shape[1]
    b_heads = b_f // HGRN_HEAD
    slopes = _alibi_slopes(n_heads)

    n_c = n_p + n_s
    pad = (-n_c) % 8
    c_all = jnp.concatenate([c_prompt, c_sample, jnp.zeros((pad, d), F32)], axis=0)
    ada = _ada_call(c_all, w_ada[0], b_ada[0])
    shift, scale, gate = ada[:, :d], ada[:, d:2 * d], ada[:, 2 * d:]

    w_in_bf = w_in[0].astype(BF16)
    wa_bf, wb_bf, wo_bf = w_branch_a[0].astype(BF16), w_branch_b[0].astype(BF16), w_out[0].astype(BF16)
    qg = (jnp.tile(q_norm_gain[0], n_heads) * (ATTN_SCALE * LOG2E)).reshape(1, a_out)
    kg = jnp.tile(k_norm_gain[0], n_heads).reshape(1, a_out)
    og = hgrn_out_norm_gain[0]

    def cond(lo, n):
        return [a[lo:lo + n].reshape(n, 1, d) for a in (shift, scale, gate)]

    sh, sc, gt_p = cond(0, n_p)
    outs = _proj_call(x_prompt, sh, sc, norm_gain[0], w_in_bf, qg, kg, hgrn_lb_logits, n_groups=N_GROUPS,
                      dils=DILATIONS, a_out=a_out, b_f=b_f, kv_dtype=BF16)
    qs, ks, vs = outs[0:N_GROUPS], outs[N_GROUPS:2 * N_GROUPS], outs[2 * N_GROUPS:3 * N_GROUPS]
    act_p = outs[3 * N_GROUPS:]
    sh, sc, gt_s = cond(n_p, n_s)
    outs = _proj_call(x_sample, sh, sc, norm_gain[0], w_in_bf, qg, kg, hgrn_lb_logits, n_groups=N_GROUPS,
                      dils=(1,) * N_GROUPS, a_out=a_out, b_f=b_f, kv_dtype=F32)
    qn, kn, vn = [[a.reshape(n_s, t_s, a_out) for a in outs[i * N_GROUPS:(i + 1) * N_GROUPS]] for i in range(3)]
    act_s = outs[3 * N_GROUPS:]

    steps = [_attn_steps(qs[gi])[2] for gi in range(N_GROUPS)]
    assert n_s % sum(steps) == 0, "sample sequences must spread evenly over the prompt attention grid steps"
    n_seq = n_s // sum(steps)
    sample = (qn, kn, vn, (cache_kv_g0[0], cache_kv_g1[0], cache_kv_g2[0]))
    sza, sqb, g, kf, ib, szb, sga, sgb = act_p
    carrier = int(np.argmax(steps))
    rec = tuple(a.reshape(n_p, t_p, b_f) for a in (sqb, g, kf, ib)) + (og, b_heads)
    att = [_attn_call(qs[gi], ks[gi], vs[gi], gi, n_heads, slopes, sample, sum(steps[:gi]), n_seq,
                      rec if gi == carrier else None) for gi in range(N_GROUPS)]
    o_a = jnp.concatenate([a[2] for a in att], axis=0)
    ob, hgrn_p = att[carrier][3].reshape(n_p * t_p, b_f), att[carrier][4]

    y_p = _out_call([a[0] for a in att] + [a[1] for a in att], sza, ob, szb, sga, sgb, x_prompt, gt_p,
                    wa_bf, wb_bf, wo_bf)
    kv_p = []
    for gi in range(N_GROUPS):
        w = min(WINDOWS[gi], t_p)
        rows = _kvwin_call(ks[gi], vs[gi], w).reshape(n_p, 2, n_heads, HEAD_DIM, w)
        kv_p.append(jnp.transpose(rows, (0, 4, 1, 2, 3))[None])

    sza, sqb, g, kf, ib, szb, sga, sgb = act_s
    ob, hgrn_s = _hgrn_sample_call(sqb, g, kf, ib, state_hgrn[0], og, n_s, t_s, b_heads)
    y_s = _out_call([o_a.reshape(n_s, 1, t_s, a_out)], sza, ob, szb, sga, sgb, x_sample, gt_s, wa_bf, wb_bf, wo_bf)
    kv_s = [jnp.transpose(a.reshape(t_s, 2, n_heads, HEAD_DIM, n_s), (4, 0, 1, 2, 3))[None]
            for a in _kvnew_call(kn, vn)]

    return (y_p, y_s, kv_p[0], kv_p[1], kv_p[2], hgrn_p[None], kv_s[0], kv_s[1], kv_s[2], hgrn_s[None])
```

```python
import functools

import numpy as np
import jax
import jax.numpy as jnp
from jax import lax
from jax.experimental import pallas as pl
from jax.experimental.pallas import tpu as pltpu

F32 = jnp.float32
BF16 = jnp.bfloat16

HEAD_DIM = 64
N_GROUPS = 3
WINDOWS = (128, 512, 2048)
DILATIONS = (1, 4, 16)
N_BACK = 128
ATTN_BLOCK = 128
ATTN_BLOCKS_PER_STEP = 4
ATTN_SCALE = HEAD_DIM ** -0.5
HGRN_HEAD = 128
LANES = 128
LOG2E = 1.4426950408889634
EPS = 1e-6
NEG = -1e30

V7X_VMEM_LIMIT = 60 * 1024 * 1024
PROJ_TOKENS = 512
HGRN_CHUNK = 64
HGRN_MAX_SEQS = 4
HGRN_SAMPLE_SEQS = 8


def _sigmoid(x):
    return 1.0 / (1.0 + jnp.exp(-x))


def _silu(x):
    return x * _sigmoid(x)


def _dot(a, b):
    return jnp.dot(a, b, preferred_element_type=F32)


def _dot_nt(a, b):
    return lax.dot_general(a, b, (((1,), (1,)), ((), ())), preferred_element_type=F32)


def _dot_tn(a, b):
    return lax.dot_general(a, b, (((0,), (0,)), ((), ())), preferred_element_type=F32)


def _alibi_slopes(n_heads_per_group):
    a_heads = N_GROUPS * n_heads_per_group
    s = 2.0 ** (-8.0 * (np.arange(a_heads) + 1) / a_heads)
    return s.astype(np.float32).reshape(N_GROUPS, n_heads_per_group)


def _resident(shape):
    nd = len(shape)
    return pl.BlockSpec(shape, lambda *_: (0,) * nd, pipeline_mode=pl.Buffered(1))


def _ada_kernel(c_ref, w_ref, b_ref, o_ref):
    o_ref[...] = _dot(_silu(c_ref[...]).astype(BF16), w_ref[...]) + b_ref[...]


def _ada_call(c, w_ada, b_ada):
    n, d = c.shape
    cols = w_ada.shape[1]
    blk = d
    return pl.pallas_call(
        _ada_kernel,
        grid=(cols // blk,),
        in_specs=[pl.BlockSpec((n, d), lambda j: (0, 0)),
                  pl.BlockSpec((d, blk), lambda j: (0, j)),
                  pl.BlockSpec((1, blk), lambda j: (0, j))],
        out_specs=pl.BlockSpec((n, blk), lambda j: (0, j)),
        out_shape=jax.ShapeDtypeStruct((n, cols), F32),
        name="ada",
    )(c, w_ada.astype(BF16), b_ada.reshape(1, cols))


def _store_dilated(out_ref, val, scr, dil):
    bs, d, rows, width = out_ref.shape
    if dil == 1:
        out_ref[...] = val.reshape(bs, 1, rows, width).astype(out_ref.dtype)
        return
    for c in range(width // LANES):
        scr[c] = val[:, c * LANES:(c + 1) * LANES]
    for r in range(dil):
        for c in range(width // LANES):
            out_ref[0, r, :, c * LANES:(c + 1) * LANES] = scr[c, pl.ds(r, rows, stride=dil), :].astype(out_ref.dtype)


def _proj_kernel(x_ref, shift_ref, scale_ref, ng_ref, w_ref, qg_ref, kg_ref, lbl_ref, *refs,
                 n_groups, dils, a_out, d_model):
    qkv_refs = refs[:3 * n_groups]
    sza_ref, sqb_ref, g_ref, kf_ref, ib_ref, szb_ref, sga_ref, sgb_ref = refs[3 * n_groups:3 * n_groups + 8]
    win_refs, scr = refs[3 * n_groups + 8:-1], refs[-1]
    x = x_ref[...]
    ms = jnp.mean(x * x, axis=-1, keepdims=True)
    h = x * lax.rsqrt(ms + EPS) * ng_ref[...]
    h = h * (1.0 + scale_ref[...]) + shift_ref[...]
    tm = h.shape[0] * h.shape[1]
    hb = h.reshape(tm, d_model).astype(BF16)
    tile = a_out

    def proj(c0):
        return _dot(hb, w_ref[:, c0:c0 + tile])

    def head_norm(u, gain):
        u2 = u * u
        low = lax.broadcasted_iota(jnp.int32, (tm, LANES), 1) < HEAD_DIM
        outs = []
        for c in range(tile // LANES):
            s = u2[:, c * LANES:(c + 1) * LANES]
            lo = jnp.sum(jnp.where(low, s, 0.0), axis=-1, keepdims=True)
            hi = jnp.sum(jnp.where(low, 0.0, s), axis=-1, keepdims=True)
            r_lo = lax.rsqrt(lo * (1.0 / HEAD_DIM) + EPS)
            r_hi = lax.rsqrt(hi * (1.0 / HEAD_DIM) + EPS)
            outs.append(u[:, c * LANES:(c + 1) * LANES] * jnp.where(low, r_lo, r_hi))
        return jnp.concatenate(outs, axis=1) * gain

    c0 = 0
    for kind in range(3):
        for g in range(n_groups):
            u = proj(c0)
            if kind == 0:
                u = head_norm(u, qg_ref[...])
            elif kind == 1:
                u = head_norm(u, kg_ref[...])
            _store_dilated(qkv_refs[kind * n_groups + g], u, scr, dils[g])
            if win_refs and kind > 0:
                cols = win_refs[g].shape[3]
                win_refs[g][0, kind - 1] = u[tm - cols:, :].T
            c0 += tile
    sza_ref[...] = _silu(proj(c0)).astype(BF16)
    c0 += tile
    sqb_ref[...] = _silu(proj(c0)).astype(BF16)
    c0 += tile
    lbl = lbl_ref[...]
    lbe = jnp.exp(lbl - jnp.max(lbl, axis=0, keepdims=True))
    lb = lbe[0:1] / jnp.sum(lbe, axis=0, keepdims=True)
    fr = proj(c0)
    g_ref[...] = jnp.log2(lb + (1.0 - lb) * _sigmoid(fr))
    kf_ref[...] = ((1.0 - lb) * _sigmoid(-fr)).astype(BF16)
    c0 += tile
    ib_ref[...] = proj(c0).astype(BF16)
    c0 += tile
    szb_ref[...] = _silu(proj(c0)).astype(BF16)
    c0 += tile
    for j in range(d_model // tile):
        sga_ref[:, j * tile:(j + 1) * tile] = _sigmoid(proj(c0)).astype(BF16)
        c0 += tile
    for j in range(d_model // tile):
        sgb_ref[:, j * tile:(j + 1) * tile] = _sigmoid(proj(c0)).astype(BF16)
        c0 += tile


def _token_tiling(s, t):
    if t >= PROJ_TOKENS:
        assert t % PROJ_TOKENS == 0
        return 1, PROJ_TOKENS
    bs = min(s, PROJ_TOKENS // t)
    assert s % bs == 0 and t % 8 == 0
    return bs, t


def _proj_call(x, shift, scale, norm_gain, w_in_bf, qg, kg, lb_logits, *, n_groups, dils, a_out, b_f, kv_dtype,
               windows=()):
    s, t, d = x.shape
    bs, bt = _token_tiling(s, t)
    tm = bs * bt
    n_t = t // bt
    ntok = s * t
    grid = (s // bs, n_t)
    assert all(dl == 1 or (bs == 1 and bt % (dl * 16) == 0) for dl in dils)

    def tok_spec(cols):
        return pl.BlockSpec((tm, cols), lambda i, j: (i * n_t + j, 0))

    qkv_dtypes = [BF16] * n_groups + [kv_dtype] * (2 * n_groups)
    qkv_shapes = [jax.ShapeDtypeStruct((s, dils[g], t // dils[g], a_out), dt)
                  for g, dt in zip(list(range(n_groups)) * 3, qkv_dtypes)]
    qkv_specs = [pl.BlockSpec((bs, dils[g], bt // dils[g], a_out), lambda i, j: (i, 0, j, 0))
                 for g in list(range(n_groups)) * 3]
    out_cols = [(a_out, BF16), (b_f, BF16), (b_f, F32), (b_f, BF16), (b_f, BF16), (b_f, BF16), (d, BF16), (d, BF16)]
    win_shapes, win_specs = [], []
    for w in windows:
        cols = min(w, bt)
        assert bs == 1 and w % cols == 0 and t % cols == 0
        first = (t - w) // cols
        per_step = bt // cols
        win_shapes.append(jax.ShapeDtypeStruct((s, 2, a_out, w), F32))
        win_specs.append(pl.BlockSpec(
            (1, 2, a_out, cols),
            lambda i, j, first=first, per_step=per_step: (i, 0, 0, jnp.maximum((j + 1) * per_step - 1 - first, 0))))
    kern = functools.partial(_proj_kernel, n_groups=n_groups, dils=tuple(dils), a_out=a_out, d_model=d)
    return pl.pallas_call(
        kern,
        grid=grid,
        in_specs=[pl.BlockSpec((bs, bt, d), lambda i, j: (i, j, 0)),
                  pl.BlockSpec((bs, 1, d), lambda i, j: (i, 0, 0)),
                  pl.BlockSpec((bs, 1, d), lambda i, j: (i, 0, 0)),
                  _resident((1, d)),
                  _resident(w_in_bf.shape),
                  _resident(qg.shape),
                  _resident(kg.shape),
                  _resident(lb_logits.shape)],
        out_specs=qkv_specs + [tok_spec(c) for c, _ in out_cols] + win_specs,
        out_shape=qkv_shapes + [jax.ShapeDtypeStruct((ntok, c), dt) for c, dt in out_cols] + win_shapes,
        scratch_shapes=[pltpu.VMEM((a_out // LANES, tm, LANES), F32)],
        compiler_params=pltpu.CompilerParams(dimension_semantics=("arbitrary", "arbitrary"),
                                             vmem_limit_bytes=V7X_VMEM_LIMIT),
        name="proj",
    )(x, shift, scale, norm_gain.reshape(1, d), w_in_bf, qg, kg, lb_logits)


def _prompt_attn(q_ref, kp_ref, kc_ref, vp_ref, vc_ref, bias_ref, o_ref, lse_ref, *, n_heads, n_blk):
    first = (pl.program_id(2) == 0).astype(jnp.int32)
    lane = lax.broadcasted_iota(jnp.int32, (ATTN_BLOCK, 2 * HEAD_DIM), 1)
    low = lane < HEAD_DIM
    for blk in range(n_blk):
        rows = slice(blk * ATTN_BLOCK, (blk + 1) * ATTN_BLOCK)
        before = slice((blk - 1) * ATTN_BLOCK, blk * ATTN_BLOCK)
        table = first if blk == 0 else 0
        scores, values = [], []
        for hp in range(n_heads // 2):
            sl = slice(hp * 2 * HEAD_DIM, (hp + 1) * 2 * HEAD_DIM)
            q2 = q_ref[rows, sl]
            k_prev = kp_ref[:, sl] if blk == 0 else kc_ref[before, sl]
            v_prev = vp_ref[:, sl] if blk == 0 else vc_ref[before, sl]
            k2 = jnp.concatenate([k_prev, kc_ref[rows, sl]], axis=0)
            values.append(jnp.concatenate([v_prev, vc_ref[rows, sl]], axis=0))
            for par in range(2):
                keep = low if par == 0 else jnp.logical_not(low)
                qm = jnp.where(keep, q2, jnp.zeros_like(q2))
                scores.append(_dot_nt(qm, k2) + bias_ref[table, 2 * hp + par])
        soft = []
        for s in scores:
            m = jnp.max(s, axis=-1, keepdims=True)
            p = jnp.exp2(s - m)
            soft.append((p.astype(BF16), m, jnp.sum(p, axis=-1, keepdims=True)))
        outs = [_dot(p, values[i // 2]) for i, (p, m, l) in enumerate(soft)]
        lse = jnp.zeros((ATTN_BLOCK, LANES), F32)
        for hp in range(n_heads // 2):
            sl = slice(hp * 2 * HEAD_DIM, (hp + 1) * 2 * HEAD_DIM)
            (_, m0, l0), (_, m1, l1) = soft[2 * hp], soft[2 * hp + 1]
            o_ref[rows, sl] = jnp.where(low, outs[2 * hp] / l0, outs[2 * hp + 1] / l1).astype(o_ref.dtype)
            lse = jnp.where(lane == 2 * hp, m0 + jnp.log2(l0), lse)
            lse = jnp.where(lane == 2 * hp + 1, m1 + jnp.log2(l1), lse)
        lse_ref[rows, :] = lse


def _sample_attn(qs, ks, vs, caches, biases, bn_ref, o_ref, seq, *, n_heads):
    per_head = []
    for h in range(n_heads):
        scores, values = [], []
        hs = slice(h * HEAD_DIM, (h + 1) * HEAD_DIM)
        for g in range(N_GROUPS):
            qh = qs[g][seq, :, hs]
            scores.append(_dot(qh, caches[g][seq, 0, h].astype(BF16)) + biases[g][h])
            values.append(caches[g][seq, 1, h].astype(BF16))
            scores.append(_dot_nt(qh, ks[g][seq, :, hs].astype(BF16)) + bn_ref[g, h])
            values.append(vs[g][seq, :, hs].astype(BF16))
        per_head.append((scores, values))
    soft = []
    for scores, values in per_head:
        m = functools.reduce(jnp.maximum, [jnp.max(s, axis=-1, keepdims=True) for s in scores])
        ps = [jnp.exp2(s - m) for s in scores]
        l = functools.reduce(lambda a, b: a + b, [jnp.sum(p, axis=-1, keepdims=True) for p in ps])
        soft.append(([p.astype(BF16) for p in ps], l))
    for h, ((ps, l), (_, values)) in enumerate(zip(soft, per_head)):
        parts = [(_dot_nt(p, v) if i % 2 == 0 else _dot(p, v)) for i, (p, v) in enumerate(zip(ps, values))]
        acc = functools.reduce(lambda a, b: a + b, parts)
        o_ref[seq, :, h * HEAD_DIM:(h + 1) * HEAD_DIM] = (acc / l).astype(o_ref.dtype)


def _attn_kernel(*refs, n_heads, n_blk, n_seq, hgrn):
    prompt_in, refs = refs[:6], refs[6:]
    qs, ks, vs = refs[0:N_GROUPS], refs[N_GROUPS:2 * N_GROUPS], refs[2 * N_GROUPS:3 * N_GROUPS]
    caches = refs[3 * N_GROUPS:4 * N_GROUPS]
    biases = refs[4 * N_GROUPS:5 * N_GROUPS]
    bn_ref, refs = refs[5 * N_GROUPS], refs[5 * N_GROUPS + 1:]
    if hgrn is not None:
        rec_in, refs = refs[:8], refs[8:]
        o_ref, lse_ref, os_ref, ob_ref, st_ref = refs
    else:
        o_ref, lse_ref, os_ref = refs
    for seq in range(n_seq):
        _sample_attn(qs, ks, vs, caches, biases, bn_ref, os_ref, seq, n_heads=n_heads)
    if hgrn is not None:
        first = (pl.program_id(0) == 0) & (pl.program_id(1) == 0) & (pl.program_id(2) == 0)
        _hgrn_prompt_part(*rec_in, ob_ref, st_ref, first, c=hgrn[0], n_heads=hgrn[1])
    _prompt_attn(*prompt_in, o_ref, lse_ref, n_heads=n_heads, n_blk=n_blk)


def _attn_tables(dil, slopes_g):
    i = np.arange(ATTN_BLOCK)[:, None]
    j = np.arange(2 * ATTN_BLOCK)[None, :]
    delta = i + ATTN_BLOCK - j
    valid = (delta >= 0) & (delta <= N_BACK)
    dist = (delta * dil).astype(np.float32)
    bias = np.where(valid[None], -slopes_g[:, None, None] * dist[None] * LOG2E, NEG).astype(np.float32)
    first = np.where((j >= ATTN_BLOCK)[None], bias, NEG).astype(np.float32)
    return jnp.asarray(np.stack([bias, first]))


def _sattn_tables(n_heads, t_new, slopes, wbs):
    t = np.arange(t_new)[:, None]
    biases, bn = [], []
    for g in range(N_GROUPS):
        d, wb = DILATIONS[g], wbs[g]
        dist = wb + t - np.arange(wb)[None, :]
        valid = (dist % d == 0) & (dist // d <= N_BACK)
        biases.append(np.where(valid[None], -slopes[g][:, None, None] * dist[None] * LOG2E, NEG).astype(np.float32))
        dist = t - np.arange(t_new)[None, :]
        valid = (dist >= 0) & (dist % d == 0) & (dist // d <= N_BACK)
        bn.append(np.where(valid[None], -slopes[g][:, None, None] * dist[None] * LOG2E, NEG).astype(np.float32))
    return [jnp.asarray(b) for b in biases], jnp.asarray(np.stack(bn))


def _attn_steps(q):
    n, dil, sub, _ = q.shape
    nb = sub // ATTN_BLOCK
    n_blk = min(ATTN_BLOCKS_PER_STEP, nb)
    assert sub % ATTN_BLOCK == 0 and nb % n_blk == 0
    return n_blk, nb // n_blk, n * dil * (nb // n_blk)


def _attn_call(q, k, v, g, n_heads, slopes, sample, first_block, n_seq, rec=None):
    n, dil, sub, width = q.shape
    assert dil == DILATIONS[g] and width == n_heads * HEAD_DIM and n_heads <= LANES
    n_blk, steps_i, steps = _attn_steps(q)
    cur = pl.BlockSpec((None, None, n_blk * ATTN_BLOCK, width), lambda b, r, i: (b, r, i, 0))
    prev = pl.BlockSpec((None, None, ATTN_BLOCK, width), lambda b, r, i: (b, r, jnp.maximum(i * n_blk - 1, 0), 0))
    lse = pl.BlockSpec((None, None, n_blk * ATTN_BLOCK, LANES), lambda b, r, i: (b, r, i, 0))
    bias = _attn_tables(dil, slopes[g])

    qs, ks, vs, caches = sample
    t_new = qs[0].shape[1]
    wbs = [c.shape[1] for c in caches]
    sbiases, bn = _sattn_tables(n_heads, t_new, slopes, wbs)
    views = [jnp.transpose(c, (0, 2, 3, 4, 1)) for c in caches]

    def step(b, r, i):
        return (b * dil + r) * steps_i + i

    new = pl.BlockSpec((n_seq, t_new, width), lambda b, r, i: (first_block + step(b, r, i), 0, 0))
    cache_specs = [pl.BlockSpec((n_seq, 2, n_heads, HEAD_DIM, wb), lambda b, r, i: (first_block + step(b, r, i), 0, 0, 0, 0))
                   for wb in wbs]
    in_specs = ([cur, prev, cur, prev, cur, _resident(bias.shape)] + [new] * (3 * N_GROUPS) + cache_specs
                + [_resident(b.shape) for b in sbiases] + [_resident(bn.shape)])
    out_specs = [cur, lse, pl.BlockSpec((n_seq, t_new, width), lambda b, r, i: (step(b, r, i), 0, 0))]
    out_shape = [jax.ShapeDtypeStruct(q.shape, BF16), jax.ShapeDtypeStruct((n, dil, sub, LANES), F32),
                 jax.ShapeDtypeStruct((steps * n_seq, t_new, width), BF16)]
    operands = [q, k, k, v, v, bias, *qs, *ks, *vs, *views, *sbiases, bn]
    hgrn = None
    if rec is not None:
        sq, g_log, kf, iv, og, rec_heads = rec
        n_r, t_r, width_r = sq.shape
        c = min(HGRN_CHUNK, t_r)
        assert t_r % steps == 0 and (t_r // steps) % c == 0 and n_r <= HGRN_MAX_SEQS
        tri, masks, signs = _hgrn_tables(c)
        tok = pl.BlockSpec((n_r, t_r // steps, width_r), lambda b, r, i: (0, step(b, r, i), 0))
        in_specs += [tok] * 4 + [_resident(tri.shape), _resident(masks.shape), _resident(signs.shape),
                                 _resident((1, HGRN_HEAD))]
        out_specs += [tok, pl.BlockSpec((n_r, rec_heads, HGRN_HEAD, HGRN_HEAD), lambda b, r, i: (0, 0, 0, 0))]
        out_shape += [jax.ShapeDtypeStruct((n_r, t_r, width_r), BF16),
                      jax.ShapeDtypeStruct((n_r, rec_heads, HGRN_HEAD, HGRN_HEAD), F32)]
        operands += [sq, g_log, kf, iv, tri, masks, signs, og.reshape(1, HGRN_HEAD)]
        hgrn = (c, rec_heads)
    return pl.pallas_call(
        functools.partial(_attn_kernel, n_heads=n_heads, n_blk=n_blk, n_seq=n_seq, hgrn=hgrn),
        grid=(n, dil, steps_i),
        in_specs=in_specs,
        out_specs=out_specs,
        out_shape=out_shape,
        compiler_params=pltpu.CompilerParams(dimension_semantics=("arbitrary", "arbitrary", "arbitrary"),
                                             vmem_limit_bytes=V7X_VMEM_LIMIT),
        name=f"attn_g{g}",
    )(*operands)


def _hgrn_tables(c):
    t = np.arange(c)
    tri = (t[None, :] <= t[:, None]).astype(np.float32)
    masks = [np.eye(c, dtype=bool)]
    signs = []
    w = c // 2
    while w >= 1:
        blk = t // (2 * w)
        late = (t // w) % 2 == 1
        masks.append((blk[:, None] == blk[None, :]) & late[:, None] & ~late[None, :])
        if w < 8:
            signs.append(np.where(late, 1.0, -1.0))
        w //= 2
    masks = np.stack(masks).astype(np.float32)
    signs = np.repeat(np.stack(signs).reshape(-1, 1), HGRN_HEAD, axis=1).astype(np.float32)
    return jnp.asarray(tri, BF16), jnp.asarray(masks), jnp.asarray(signs)


def _level_exponents(b, g, c):
    width = b.shape[1]
    coarse, fine = [], []
    w = c // 2
    while w >= 8:
        parts = []
        for k in range(c // (2 * w)):
            lo, mid, hi = k * 2 * w, k * 2 * w + w, (k + 1) * 2 * w
            ref = b[mid - 1:mid]
            parts.append((slice(lo, mid), False, ref - b[lo:mid]))
            parts.append((slice(mid, hi), True, b[mid:hi] - ref))
        coarse.append(parts)
        w //= 2
    b3 = b.reshape(c // 8, 8, width)
    sub = lax.broadcasted_iota(jnp.int32, b3.shape, 1)
    if c >= 8:
        fine.append((b3 - jnp.broadcast_to(b3[:, 3:4], b3.shape)).reshape(c, width))
    ref = jnp.where(sub < 4, jnp.broadcast_to(b3[:, 1:2], b3.shape), jnp.broadcast_to(b3[:, 5:6], b3.shape))
    fine.append((b3 - ref).reshape(c, width))
    row = lax.broadcasted_iota(jnp.int32, b.shape, 0)
    fine.append(jnp.where(row % 2 == 1, g, 0.0))
    return coarse, fine


def _hgrn_chunks(seqs, tri, masks, signs, og, c):
    n_lev = masks.shape[0] - 1
    tri_v = tri[...]
    og = og[...]
    pre = []
    for sq, g, kf, iv, state in seqs:
        g1 = g.astype(BF16)
        r1 = g - g1.astype(F32)
        g2 = r1.astype(BF16)
        g3 = (r1 - g2.astype(F32)).astype(BF16)
        b = _dot(tri_v, g1) + _dot(tri_v, g2) + _dot(tri_v, g3)
        pre.append((b, _level_exponents(b, g, c)))
    heads = []
    for (sq, g, kf, iv, state), (b, (coarse, fine)) in zip(seqs, pre):
        e_cum = jnp.exp2(b)
        e_tail = jnp.exp2(b[c - 1:c] - b)
        for h in range(len(state)):
            hs = slice(h * HGRN_HEAD, (h + 1) * HGRN_HEAD)
            sqh, kfh, ivh = sq[:, hs], kf[:, hs], iv[:, hs]
            sqf, kff = sqh.astype(F32), kfh.astype(F32)
            eb = e_cum[:, hs]
            prods = [_dot_nt(sqh, kfh)]
            for parts in coarse:
                x = jnp.concatenate([(sqf if late else kff)[rows] * jnp.exp2(e[:, hs]) for rows, late, e in parts],
                                    axis=0).astype(BF16)
                prods.append(_dot_nt(x, x))
            for lv, e in enumerate(fine):
                sgn = signs[lv * c:(lv + 1) * c]
                x = (jnp.where(sgn > 0.0, sqf, kff) * jnp.exp2(e[:, hs] * sgn)).astype(BF16)
                prods.append(_dot_nt(x, x))
            o_inter = _dot((sqf * eb).astype(BF16), state[h].astype(BF16))
            kt = (kff * e_tail[:, hs]).astype(BF16)
            heads.append((prods, o_inter, _dot_tn(kt, ivh), eb, ivh, state[h]))
    results = []
    for prods, o_inter, upd, eb, ivh, st in heads:
        att = masks[0] * prods[0]
        for lv in range(n_lev):
            att = att + masks[1 + lv] * prods[1 + lv]
        o = o_inter + _dot(att.astype(BF16), ivh)
        decay = jnp.broadcast_to(eb[c - 1:c, :], (HGRN_HEAD, HGRN_HEAD)).T
        results.append((o * lax.rsqrt(jnp.mean(o * o, axis=-1, keepdims=True) + EPS) * og, decay * st + upd))
    n_h = len(seqs[0][4])
    return [([r[0] for r in results[i * n_h:(i + 1) * n_h]], [r[1] for r in results[i * n_h:(i + 1) * n_h]])
            for i in range(len(seqs))]


def _hgrn_prompt_part(sq_ref, g_ref, kf_ref, iv_ref, tri_ref, masks_ref, signs_ref, og_ref, o_ref, s_ref, first,
                      *, c, n_heads):
    @pl.when(first)
    def _():
        s_ref[...] = jnp.zeros_like(s_ref)

    def body(ci, carry):
        rows = pl.ds(pl.multiple_of(ci * c, c), c)
        seqs = [(sq_ref[b, rows, :], g_ref[b, rows, :], kf_ref[b, rows, :], iv_ref[b, rows, :],
                 [s_ref[b, h] for h in range(n_heads)]) for b in range(sq_ref.shape[0])]
        res = _hgrn_chunks(seqs, tri_ref, masks_ref, signs_ref, og_ref, c)
        for b, (outs, new_state) in enumerate(res):
            for h in range(n_heads):
                s_ref[b, h] = new_state[h]
                o_ref[b, rows, h * HGRN_HEAD:(h + 1) * HGRN_HEAD] = outs[h].astype(o_ref.dtype)
        return carry

    lax.fori_loop(0, sq_ref.shape[1] // c, body, 0)


def _hgrn_sample_kernel(sq_ref, g_ref, kf_ref, iv_ref, s0_ref, tri_ref, masks_ref, signs_ref, og_ref, o_ref, s_ref,
                        *, c, n_heads):
    seqs = [(sq_ref[b], g_ref[b], kf_ref[b], iv_ref[b], [s0_ref[b, h] for h in range(n_heads)])
            for b in range(sq_ref.shape[0])]
    res = _hgrn_chunks(seqs, tri_ref, masks_ref, signs_ref, og_ref, c)
    for b, (outs, new_state) in enumerate(res):
        for h in range(n_heads):
            s_ref[b, h] = new_state[h]
            o_ref[b, :, h * HGRN_HEAD:(h + 1) * HGRN_HEAD] = outs[h].astype(o_ref.dtype)


def _hgrn_sample_call(sq, g, kf, iv, s0, og, n, t, n_heads):
    width = n_heads * HGRN_HEAD
    bn = min(HGRN_SAMPLE_SEQS, n)
    assert n % bn == 0 and t % 8 == 0
    tri, masks, signs = _hgrn_tables(t)
    tok = pl.BlockSpec((bn, t, width), lambda i: (i, 0, 0))
    st = pl.BlockSpec((bn, n_heads, HGRN_HEAD, HGRN_HEAD), lambda i: (i, 0, 0, 0))
    o, s = pl.pallas_call(
        functools.partial(_hgrn_sample_kernel, c=t, n_heads=n_heads),
        grid=(n // bn,),
        in_specs=[tok, tok, tok, tok, st, _resident(tri.shape), _resident(masks.shape), _resident(signs.shape),
                  _resident((1, HGRN_HEAD))],
        out_specs=[tok, st],
        out_shape=[jax.ShapeDtypeStruct((n, t, width), BF16),
                   jax.ShapeDtypeStruct((n, n_heads, HGRN_HEAD, HGRN_HEAD), F32)],
        compiler_params=pltpu.CompilerParams(dimension_semantics=("arbitrary",), vmem_limit_bytes=V7X_VMEM_LIMIT),
        name="hgrn_sample",
    )(sq.reshape(n, t, width), g.reshape(n, t, width), kf.reshape(n, t, width), iv.reshape(n, t, width),
      s0, tri, masks, signs, og.reshape(1, HGRN_HEAD))
    return o.reshape(n * t, width), s


def _load_tokens(ref, scr):
    bs, dil, rows, width = ref.shape
    if dil == 1:
        return ref[...].reshape(bs * rows, width).astype(F32)
    for r in range(dil):
        for c in range(width // LANES):
            scr[c, pl.ds(r, rows, stride=dil), :] = ref[0, r, :, c * LANES:(c + 1) * LANES].astype(F32)
    return jnp.concatenate([scr[c] for c in range(width // LANES)], axis=1)


def _out_kernel(*refs, n_att):
    att = refs[:2 * n_att] if n_att > 1 else refs[:1]
    n_in = len(att)
    sza_ref, ob_ref, szb_ref, sga_ref, sgb_ref, x_ref, gate_ref, ex_ref, wa_ref, wb_ref, wo_ref, y_ref, scr = refs[n_in:]
    if n_att > 1:
        ls_ = [_load_tokens(r, scr) for r in att[n_att:]]
        m = functools.reduce(jnp.maximum, ls_)
        es = [jnp.exp2(l - m) for l in ls_]
        inv = 1.0 / functools.reduce(lambda a, b: a + b, es)
        alphas = [_dot((e * inv).astype(BF16), ex_ref[...]) for e in es]
        o_a = functools.reduce(lambda a, b: a + b, [al * _load_tokens(r, scr) for al, r in zip(alphas, att[:n_att])])
    else:
        o_a = _load_tokens(att[0], scr)
    branch_a = _dot((o_a * sza_ref[...].astype(F32)).astype(BF16), wa_ref[...])
    branch_b = _dot((ob_ref[...].astype(F32) * szb_ref[...].astype(F32)).astype(BF16), wb_ref[...])
    merged = sga_ref[...].astype(F32) * branch_a + sgb_ref[...].astype(F32) * branch_b
    upd = _dot(merged.astype(BF16), wo_ref[...])
    x = x_ref[...]
    y_ref[...] = x + gate_ref[...] * upd.reshape(x.shape)


def _out_call(att_inputs, sza, ob, szb, sga, sgb, x, gate, wa_bf, wb_bf, wo_bf):
    s, t, d = x.shape
    bs, bt = _token_tiling(s, t)
    tm = bs * bt
    n_t = t // bt

    def tok_spec(a):
        return pl.BlockSpec((tm, a.shape[1]), lambda i, j: (i * n_t + j, 0))

    def att_spec(a):
        dil = a.shape[1]
        assert dil == 1 or bs == 1
        return pl.BlockSpec((bs, dil, bt // dil, a.shape[3]), lambda i, j: (i, 0, j, 0))

    n_att = (len(att_inputs) + 1) // 2
    toks = [sza, ob, szb, sga, sgb]
    width = att_inputs[0].shape[3]
    lane_head = np.arange(LANES)[:, None] == (np.arange(width) // HEAD_DIM)[None, :]
    expand = jnp.asarray(lane_head, BF16)
    return pl.pallas_call(
        functools.partial(_out_kernel, n_att=n_att),
        grid=(s // bs, n_t),
        in_specs=[att_spec(a) for a in att_inputs] + [tok_spec(a) for a in toks]
        + [pl.BlockSpec((bs, bt, d), lambda i, j: (i, j, 0)),
           pl.BlockSpec((bs, 1, d), lambda i, j: (i, 0, 0)),
           _resident(expand.shape), _resident(wa_bf.shape), _resident(wb_bf.shape), _resident(wo_bf.shape)],
        out_specs=pl.BlockSpec((bs, bt, d), lambda i, j: (i, j, 0)),
        out_shape=jax.ShapeDtypeStruct((s, t, d), F32),
        scratch_shapes=[pltpu.VMEM((width // LANES, tm, LANES), F32)],
        compiler_params=pltpu.CompilerParams(dimension_semantics=("arbitrary", "arbitrary"),
                                             vmem_limit_bytes=V7X_VMEM_LIMIT),
        name="out",
    )(*att_inputs, *toks, x, gate, expand, wa_bf, wb_bf, wo_bf)


def _kvnew_kernel(*refs):
    n_g = len(refs) // 3
    for g in range(n_g):
        for kv in range(2):
            src = refs[kv * n_g + g]
            for t in range(src.shape[1]):
                refs[2 * n_g + g][t, kv] = src[:, t, :].T


def _kvnew_call(ks, vs):
    n, t, width = ks[0].shape
    return pl.pallas_call(
        _kvnew_kernel,
        out_shape=[jax.ShapeDtypeStruct((t, 2, width, n), F32) for _ in ks],
        compiler_params=pltpu.CompilerParams(vmem_limit_bytes=V7X_VMEM_LIMIT),
        name="kvnew",
    )(*ks, *vs)


def kernel(x_prompt, x_sample, cache_kv_g0, cache_kv_g1, cache_kv_g2, state_hgrn, c_prompt, c_sample, norm_gain, w_ada, b_ada, w_in, q_norm_gain, k_norm_gain, hgrn_lb_logits, hgrn_out_norm_gain, w_branch_a, w_branch_b, w_out):
    depth = w_in.shape[0]
    assert depth == 1, "single-layer step"
    n_p, t_p, d = x_prompt.shape
    n_s, t_s, _ = x_sample.shape
    a_out = w_branch_a.shape[1]
    n_heads = a_out // HEAD_DIM
    b_f = w_branch_b.shape[1]
    b_heads = b_f // HGRN_HEAD
    slopes = _alibi_slopes(n_heads)

    n_c = n_p + n_s
    pad = (-n_c) % 8
    c_all = jnp.concatenate([c_prompt, c_sample, jnp.zeros((pad, d), F32)], axis=0)
    ada = _ada_call(c_all, w_ada[0], b_ada[0])
    shift, scale, gate = ada[:, :d], ada[:, d:2 * d], ada[:, 2 * d:]

    w_in_bf = w_in[0].astype(BF16)
    wa_bf, wb_bf, wo_bf = w_branch_a[0].astype(BF16), w_branch_b[0].astype(BF16), w_out[0].astype(BF16)
    qg = (jnp.tile(q_norm_gain[0], n_heads) * (ATTN_SCALE * LOG2E)).reshape(1, a_out)
    kg = jnp.tile(k_norm_gain[0], n_heads).reshape(1, a_out)
    og = hgrn_out_norm_gain[0]

    def cond(lo, n):
        return [a[lo:lo + n].reshape(n, 1, d) for a in (shift, scale, gate)]

    sh, sc, gt_p = cond(0, n_p)
    outs = _proj_call(x_prompt, sh, sc, norm_gain[0], w_in_bf, qg, kg, hgrn_lb_logits, n_groups=N_GROUPS,
                      dils=DILATIONS, a_out=a_out, b_f=b_f, kv_dtype=BF16,
                      windows=tuple(min(w, t_p) for w in WINDOWS))
    qs, ks, vs = outs[0:N_GROUPS], outs[N_GROUPS:2 * N_GROUPS], outs[2 * N_GROUPS:3 * N_GROUPS]
    act_p, kv_rows = outs[3 * N_GROUPS:3 * N_GROUPS + 8], outs[3 * N_GROUPS + 8:]
    sh, sc, gt_s = cond(n_p, n_s)
    outs = _proj_call(x_sample, sh, sc, norm_gain[0], w_in_bf, qg, kg, hgrn_lb_logits, n_groups=N_GROUPS,
                      dils=(1,) * N_GROUPS, a_out=a_out, b_f=b_f, kv_dtype=F32)
    qn, kn, vn = [[a.reshape(n_s, t_s, a_out) for a in outs[i * N_GROUPS:(i + 1) * N_GROUPS]] for i in range(3)]
    act_s = outs[3 * N_GROUPS:3 * N_GROUPS + 8]

    steps = [_attn_steps(qs[gi])[2] for gi in range(N_GROUPS)]
    assert n_s % sum(steps) == 0, "sample sequences must spread evenly over the prompt attention grid steps"
    n_seq = n_s // sum(steps)
    sample = (qn, kn, vn, (cache_kv_g0[0], cache_kv_g1[0], cache_kv_g2[0]))
    sza, sqb, g, kf, ib, szb, sga, sgb = act_p
    carrier = int(np.argmax(steps))
    rec = tuple(a.reshape(n_p, t_p, b_f) for a in (sqb, g, kf, ib)) + (og, b_heads)
    att = [_attn_call(qs[gi], ks[gi], vs[gi], gi, n_heads, slopes, sample, sum(steps[:gi]), n_seq,
                      rec if gi == carrier else None) for gi in range(N_GROUPS)]
    o_a = jnp.concatenate([a[2] for a in att], axis=0)
    ob, hgrn_p = att[carrier][3].reshape(n_p * t_p, b_f), att[carrier][4]

    y_p = _out_call([a[0] for a in att] + [a[1] for a in att], sza, ob, szb, sga, sgb, x_prompt, gt_p,
                    wa_bf, wb_bf, wo_bf)
    kv_p = [jnp.transpose(a.reshape(n_p, 2, n_heads, HEAD_DIM, a.shape[3]), (0, 4, 1, 2, 3))[None]
            for a in kv_rows]

    sza, sqb, g, kf, ib, szb, sga, sgb = act_s
    ob, hgrn_s = _hgrn_sample_call(sqb, g, kf, ib, state_hgrn[0], og, n_s, t_s, b_heads)
    y_s = _out_call([o_a.reshape(n_s, 1, t_s, a_out)], sza, ob, szb, sga, sgb, x_sample, gt_s, wa_bf, wb_bf, wo_bf)
    kv_s = [jnp.transpose(a.reshape(t_s, 2, n_heads, HEAD_DIM, n_s), (4, 0, 1, 2, 3))[None]
            for a in _kvnew_call(kn, vn)]

    return (y_p, y_s, kv_p[0], kv_p[1], kv_p[2], hgrn_p[None], kv_s[0], kv_s[1], kv_s[2], hgrn_s[None])
```

```python
import functools

import numpy as np
import jax
import jax.numpy as jnp
from jax import lax
from jax.experimental import pallas as pl
from jax.experimental.pallas import tpu as pltpu

F32 = jnp.float32
BF16 = jnp.bfloat16

HEAD_DIM = 64
N_GROUPS = 3
WINDOWS = (128, 512, 2048)
DILATIONS = (1, 4, 16)
N_BACK = 128
ATTN_BLOCK = 128
ATTN_BLOCKS_PER_STEP = 4
ATTN_SCALE = HEAD_DIM ** -0.5
HGRN_HEAD = 128
LANES = 128
LOG2E = 1.4426950408889634
EPS = 1e-6
NEG = -1e30

V7X_VMEM_LIMIT = 60 * 1024 * 1024
PROJ_TOKENS = 512
HGRN_CHUNK = 64
HGRN_MAX_SEQS = 4
HGRN_SAMPLE_SEQS = 8


def _sigmoid(x):
    return 1.0 / (1.0 + jnp.exp(-x))


def _silu(x):
    return x * _sigmoid(x)


def _dot(a, b):
    return jnp.dot(a, b, preferred_element_type=F32)


def _dot_nt(a, b):
    return lax.dot_general(a, b, (((1,), (1,)), ((), ())), preferred_element_type=F32)


def _dot_tn(a, b):
    return lax.dot_general(a, b, (((0,), (0,)), ((), ())), preferred_element_type=F32)


def _alibi_slopes(n_heads_per_group):
    a_heads = N_GROUPS * n_heads_per_group
    s = 2.0 ** (-8.0 * (np.arange(a_heads) + 1) / a_heads)
    return s.astype(np.float32).reshape(N_GROUPS, n_heads_per_group)


def _resident(shape):
    nd = len(shape)
    return pl.BlockSpec(shape, lambda *_: (0,) * nd, pipeline_mode=pl.Buffered(1))


def _ada_kernel(c_ref, w_ref, b_ref, o_ref):
    o_ref[...] = _dot(_silu(c_ref[...]).astype(BF16), w_ref[...].astype(BF16)) + b_ref[...]


def _ada_call(c, w_ada, b_ada):
    n, d = c.shape
    cols = w_ada.shape[1]
    blk = d
    return pl.pallas_call(
        _ada_kernel,
        grid=(cols // blk,),
        in_specs=[pl.BlockSpec((n, d), lambda j: (0, 0)),
                  pl.BlockSpec((d, blk), lambda j: (0, j)),
                  pl.BlockSpec((1, blk), lambda j: (0, j))],
        out_specs=pl.BlockSpec((n, blk), lambda j: (0, j)),
        out_shape=jax.ShapeDtypeStruct((n, cols), F32),
        name="ada",
    )(c, w_ada, b_ada.reshape(1, cols))


def _store_dilated(out_ref, val, scr, dil):
    bs, d, rows, width = out_ref.shape
    if dil == 1:
        out_ref[...] = val.reshape(bs, 1, rows, width).astype(out_ref.dtype)
        return
    for c in range(width // LANES):
        scr[c] = val[:, c * LANES:(c + 1) * LANES]
    for r in range(dil):
        for c in range(width // LANES):
            out_ref[0, r, :, c * LANES:(c + 1) * LANES] = scr[c, pl.ds(r, rows, stride=dil), :].astype(out_ref.dtype)


def _proj_kernel(x_ref, shift_ref, scale_ref, ng_ref, w_ref, qg_ref, kg_ref, lbl_ref, *refs,
                 n_groups, dils, a_out, d_model, merged):
    n_qkv = 3 if merged else 3 * n_groups
    qkv_refs = refs[:n_qkv]
    sza_ref, sqb_ref, g_ref, kf_ref, ib_ref, szb_ref, sga_ref, sgb_ref = refs[n_qkv:n_qkv + 8]
    win_refs, scr = refs[n_qkv + 8:-1], refs[-1]
    x = x_ref[...]
    ms = jnp.mean(x * x, axis=-1, keepdims=True)
    h = x * lax.rsqrt(ms + EPS) * ng_ref[...]
    h = h * (1.0 + scale_ref[...]) + shift_ref[...]
    tm = h.shape[0] * h.shape[1]
    hb = h.reshape(tm, d_model).astype(BF16)
    tile = a_out

    def proj(c0):
        return _dot(hb, w_ref[:, c0:c0 + tile])

    def head_norm(u, gain):
        u2 = u * u
        low = lax.broadcasted_iota(jnp.int32, (tm, LANES), 1) < HEAD_DIM
        outs = []
        for c in range(tile // LANES):
            s = u2[:, c * LANES:(c + 1) * LANES]
            lo = jnp.sum(jnp.where(low, s, 0.0), axis=-1, keepdims=True)
            hi = jnp.sum(jnp.where(low, 0.0, s), axis=-1, keepdims=True)
            r_lo = lax.rsqrt(lo * (1.0 / HEAD_DIM) + EPS)
            r_hi = lax.rsqrt(hi * (1.0 / HEAD_DIM) + EPS)
            outs.append(u[:, c * LANES:(c + 1) * LANES] * jnp.where(low, r_lo, r_hi))
        return jnp.concatenate(outs, axis=1) * gain

    c0 = 0
    for kind in range(3):
        for g in range(n_groups):
            u = proj(c0)
            if kind == 0:
                u = head_norm(u, qg_ref[...])
            elif kind == 1:
                u = head_norm(u, kg_ref[...])
            if merged:
                ref = qkv_refs[kind]
                ref[:, :, g * tile:(g + 1) * tile] = u.reshape(ref.shape[0], ref.shape[1], tile).astype(ref.dtype)
            else:
                _store_dilated(qkv_refs[kind * n_groups + g], u, scr, dils[g])
            if win_refs and kind > 0:
                cols = win_refs[g].shape[3]
                win_refs[g][0, kind - 1] = u[tm - cols:, :].T
            c0 += tile
    sza_ref[...] = _silu(proj(c0)).astype(BF16)
    c0 += tile
    sqb_ref[...] = _silu(proj(c0)).astype(BF16)
    c0 += tile
    lbl = lbl_ref[...]
    lbe = jnp.exp(lbl - jnp.max(lbl, axis=0, keepdims=True))
    lb = lbe[0:1] / jnp.sum(lbe, axis=0, keepdims=True)
    fr = proj(c0)
    g_ref[...] = jnp.log2(lb + (1.0 - lb) * _sigmoid(fr))
    kf_ref[...] = ((1.0 - lb) * _sigmoid(-fr)).astype(BF16)
    c0 += tile
    ib_ref[...] = proj(c0).astype(BF16)
    c0 += tile
    szb_ref[...] = _silu(proj(c0)).astype(BF16)
    c0 += tile
    for j in range(d_model // tile):
        sga_ref[:, j * tile:(j + 1) * tile] = _sigmoid(proj(c0)).astype(BF16)
        c0 += tile
    for j in range(d_model // tile):
        sgb_ref[:, j * tile:(j + 1) * tile] = _sigmoid(proj(c0)).astype(BF16)
        c0 += tile


def _token_tiling(s, t):
    if t >= PROJ_TOKENS:
        assert t % PROJ_TOKENS == 0
        return 1, PROJ_TOKENS
    bs = min(s, PROJ_TOKENS // t)
    assert s % bs == 0 and t % 8 == 0
    return bs, t


def _proj_call(x, shift, scale, norm_gain, w_in_bf, qg, kg, lb_logits, *, n_groups, dils, a_out, b_f, kv_dtype,
               windows=()):
    s, t, d = x.shape
    bs, bt = _token_tiling(s, t)
    tm = bs * bt
    n_t = t // bt
    ntok = s * t
    grid = (s // bs, n_t)
    assert all(dl == 1 or (bs == 1 and bt % (dl * 16) == 0) for dl in dils)

    def tok_spec(cols):
        return pl.BlockSpec((tm, cols), lambda i, j: (i * n_t + j, 0))

    merged = all(dl == 1 for dl in dils)
    if merged:
        qkv_shapes = [jax.ShapeDtypeStruct((s, t, n_groups * a_out), dt) for dt in (BF16, kv_dtype, kv_dtype)]
        qkv_specs = [pl.BlockSpec((bs, bt, n_groups * a_out), lambda i, j: (i, j, 0))] * 3
    else:
        qkv_dtypes = [BF16] * n_groups + [kv_dtype] * (2 * n_groups)
        qkv_shapes = [jax.ShapeDtypeStruct((s, dils[g], t // dils[g], a_out), dt)
                      for g, dt in zip(list(range(n_groups)) * 3, qkv_dtypes)]
        qkv_specs = [pl.BlockSpec((bs, dils[g], bt // dils[g], a_out), lambda i, j: (i, 0, j, 0))
                     for g in list(range(n_groups)) * 3]
    out_cols = [(a_out, BF16), (b_f, BF16), (b_f, F32), (b_f, BF16), (b_f, BF16), (b_f, BF16), (d, BF16), (d, BF16)]
    win_shapes, win_specs = [], []
    for w in windows:
        cols = min(w, bt)
        assert bs == 1 and w % cols == 0 and t % cols == 0
        first = (t - w) // cols
        per_step = bt // cols
        win_shapes.append(jax.ShapeDtypeStruct((s, 2, a_out, w), F32))
        win_specs.append(pl.BlockSpec(
            (1, 2, a_out, cols),
            lambda i, j, first=first, per_step=per_step: (i, 0, 0, jnp.maximum((j + 1) * per_step - 1 - first, 0))))
    kern = functools.partial(_proj_kernel, n_groups=n_groups, dils=tuple(dils), a_out=a_out, d_model=d, merged=merged)
    return pl.pallas_call(
        kern,
        grid=grid,
        in_specs=[pl.BlockSpec((bs, bt, d), lambda i, j: (i, j, 0)),
                  pl.BlockSpec((bs, 1, d), lambda i, j: (i, 0, 0)),
                  pl.BlockSpec((bs, 1, d), lambda i, j: (i, 0, 0)),
                  _resident((1, d)),
                  _resident(w_in_bf.shape),
                  _resident(qg.shape),
                  _resident(kg.shape),
                  _resident(lb_logits.shape)],
        out_specs=qkv_specs + [tok_spec(c) for c, _ in out_cols] + win_specs,
        out_shape=qkv_shapes + [jax.ShapeDtypeStruct((ntok, c), dt) for c, dt in out_cols] + win_shapes,
        scratch_shapes=[pltpu.VMEM((a_out // LANES, tm, LANES), F32)],
        compiler_params=pltpu.CompilerParams(dimension_semantics=("arbitrary", "arbitrary"),
                                             vmem_limit_bytes=V7X_VMEM_LIMIT),
        name="proj",
    )(x, shift, scale, norm_gain.reshape(1, d), w_in_bf, qg, kg, lb_logits)


def _prompt_attn(q_ref, kp_ref, kc_ref, vp_ref, vc_ref, bias_ref, o_ref, lse_ref, *, n_heads, n_blk):
    first = (pl.program_id(2) == 0).astype(jnp.int32)
    lane = lax.broadcasted_iota(jnp.int32, (ATTN_BLOCK, 2 * HEAD_DIM), 1)
    low = lane < HEAD_DIM
    for blk in range(n_blk):
        rows = slice(blk * ATTN_BLOCK, (blk + 1) * ATTN_BLOCK)
        before = slice((blk - 1) * ATTN_BLOCK, blk * ATTN_BLOCK)
        table = first if blk == 0 else 0
        scores, values = [], []
        for hp in range(n_heads // 2):
            sl = slice(hp * 2 * HEAD_DIM, (hp + 1) * 2 * HEAD_DIM)
            q2 = q_ref[rows, sl]
            k_prev = kp_ref[:, sl] if blk == 0 else kc_ref[before, sl]
            v_prev = vp_ref[:, sl] if blk == 0 else vc_ref[before, sl]
            k2 = jnp.concatenate([k_prev, kc_ref[rows, sl]], axis=0)
            values.append(jnp.concatenate([v_prev, vc_ref[rows, sl]], axis=0))
            for par in range(2):
                keep = low if par == 0 else jnp.logical_not(low)
                qm = jnp.where(keep, q2, jnp.zeros_like(q2))
                scores.append(_dot_nt(qm, k2) + bias_ref[table, 2 * hp + par])
        soft = []
        for s in scores:
            m = jnp.max(s, axis=-1, keepdims=True)
            p = jnp.exp2(s - m)
            soft.append((p.astype(BF16), m, jnp.sum(p, axis=-1, keepdims=True)))
        outs = [_dot(p, values[i // 2]) for i, (p, m, l) in enumerate(soft)]
        lse = jnp.zeros((ATTN_BLOCK, LANES), F32)
        for hp in range(n_heads // 2):
            sl = slice(hp * 2 * HEAD_DIM, (hp + 1) * 2 * HEAD_DIM)
            (_, m0, l0), (_, m1, l1) = soft[2 * hp], soft[2 * hp + 1]
            o_ref[rows, sl] = jnp.where(low, outs[2 * hp] / l0, outs[2 * hp + 1] / l1).astype(o_ref.dtype)
            lse = jnp.where(lane == 2 * hp, m0 + jnp.log2(l0), lse)
            lse = jnp.where(lane == 2 * hp + 1, m1 + jnp.log2(l1), lse)
        lse_ref[rows, :] = lse


def _sample_attn(q_ref, k_ref, v_ref, caches, biases, bn_ref, o_ref, seq, *, n_heads):
    width = n_heads * HEAD_DIM
    per_head = []
    for h in range(n_heads):
        scores, values = [], []
        for g in range(N_GROUPS):
            hs = slice(g * width + h * HEAD_DIM, g * width + (h + 1) * HEAD_DIM)
            qh = q_ref[seq, :, hs]
            scores.append(_dot(qh, caches[g][seq, 0, h].astype(BF16)) + biases[g][h])
            values.append(caches[g][seq, 1, h].astype(BF16))
            scores.append(_dot_nt(qh, k_ref[seq, :, hs].astype(BF16)) + bn_ref[g, h])
            values.append(v_ref[seq, :, hs].astype(BF16))
        per_head.append((scores, values))
    soft = []
    for scores, values in per_head:
        m = functools.reduce(jnp.maximum, [jnp.max(s, axis=-1, keepdims=True) for s in scores])
        ps = [jnp.exp2(s - m) for s in scores]
        l = functools.reduce(lambda a, b: a + b, [jnp.sum(p, axis=-1, keepdims=True) for p in ps])
        soft.append(([p.astype(BF16) for p in ps], l))
    for h, ((ps, l), (_, values)) in enumerate(zip(soft, per_head)):
        parts = [(_dot_nt(p, v) if i % 2 == 0 else _dot(p, v)) for i, (p, v) in enumerate(zip(ps, values))]
        acc = functools.reduce(lambda a, b: a + b, parts)
        o_ref[seq, :, h * HEAD_DIM:(h + 1) * HEAD_DIM] = (acc / l).astype(o_ref.dtype)


def _attn_kernel(*refs, n_heads, n_blk, n_seq, hgrn):
    prompt_in, (qn_ref, kn_ref, vn_ref), refs = refs[:6], refs[6:9], refs[9:]
    caches, biases = refs[:N_GROUPS], refs[N_GROUPS:2 * N_GROUPS]
    bn_ref, refs = refs[2 * N_GROUPS], refs[2 * N_GROUPS + 1:]
    if hgrn is not None:
        rec_in, refs = refs[:8], refs[8:]
        o_ref, lse_ref, os_ref, ob_ref, st_ref = refs
    else:
        o_ref, lse_ref, os_ref = refs
    for seq in range(n_seq):
        _sample_attn(qn_ref, kn_ref, vn_ref, caches, biases, bn_ref, os_ref, seq, n_heads=n_heads)
    if hgrn is not None:
        first = (pl.program_id(0) == 0) & (pl.program_id(1) == 0) & (pl.program_id(2) == 0)
        _hgrn_prompt_part(*rec_in, ob_ref, st_ref, first, c=hgrn[0], n_heads=hgrn[1])
    _prompt_attn(*prompt_in, o_ref, lse_ref, n_heads=n_heads, n_blk=n_blk)


def _attn_tables(dil, slopes_g):
    i = np.arange(ATTN_BLOCK)[:, None]
    j = np.arange(2 * ATTN_BLOCK)[None, :]
    delta = i + ATTN_BLOCK - j
    valid = (delta >= 0) & (delta <= N_BACK)
    dist = (delta * dil).astype(np.float32)
    bias = np.where(valid[None], -slopes_g[:, None, None] * dist[None] * LOG2E, NEG).astype(np.float32)
    first = np.where((j >= ATTN_BLOCK)[None], bias, NEG).astype(np.float32)
    return jnp.asarray(np.stack([bias, first]))


def _sattn_tables(n_heads, t_new, slopes, wbs):
    t = np.arange(t_new)[:, None]
    biases, bn = [], []
    for g in range(N_GROUPS):
        d, wb = DILATIONS[g], wbs[g]
        dist = wb + t - np.arange(wb)[None, :]
        valid = (dist % d == 0) & (dist // d <= N_BACK)
        biases.append(np.where(valid[None], -slopes[g][:, None, None] * dist[None] * LOG2E, NEG).astype(np.float32))
        dist = t - np.arange(t_new)[None, :]
        valid = (dist >= 0) & (dist % d == 0) & (dist // d <= N_BACK)
        bn.append(np.where(valid[None], -slopes[g][:, None, None] * dist[None] * LOG2E, NEG).astype(np.float32))
    return [jnp.asarray(b) for b in biases], jnp.asarray(np.stack(bn))


def _attn_steps(q):
    n, dil, sub, _ = q.shape
    nb = sub // ATTN_BLOCK
    n_blk = min(ATTN_BLOCKS_PER_STEP, nb)
    assert sub % ATTN_BLOCK == 0 and nb % n_blk == 0
    return n_blk, nb // n_blk, n * dil * (nb // n_blk)


def _attn_call(q, k, v, g, n_heads, slopes, sample, first_block, n_seq, rec=None):
    n, dil, sub, width = q.shape
    assert dil == DILATIONS[g] and width == n_heads * HEAD_DIM and n_heads <= LANES
    n_blk, steps_i, steps = _attn_steps(q)
    cur = pl.BlockSpec((None, None, n_blk * ATTN_BLOCK, width), lambda b, r, i: (b, r, i, 0))
    prev = pl.BlockSpec((None, None, ATTN_BLOCK, width), lambda b, r, i: (b, r, jnp.maximum(i * n_blk - 1, 0), 0))
    lse = pl.BlockSpec((None, None, n_blk * ATTN_BLOCK, LANES), lambda b, r, i: (b, r, i, 0))
    bias = _attn_tables(dil, slopes[g])

    qn, kn, vn, caches = sample
    t_new = qn.shape[1]
    wbs = [c.shape[1] for c in caches]
    sbiases, bn = _sattn_tables(n_heads, t_new, slopes, wbs)
    views = [jnp.transpose(c, (0, 2, 3, 4, 1)) for c in caches]

    def step(b, r, i):
        return (b * dil + r) * steps_i + i

    new = pl.BlockSpec((n_seq, t_new, N_GROUPS * width), lambda b, r, i: (first_block + step(b, r, i), 0, 0))
    cache_specs = [pl.BlockSpec((n_seq, 2, n_heads, HEAD_DIM, wb), lambda b, r, i: (first_block + step(b, r, i), 0, 0, 0, 0))
                   for wb in wbs]
    in_specs = ([cur, prev, cur, prev, cur, _resident(bias.shape)] + [new] * 3 + cache_specs
                + [_resident(b.shape) for b in sbiases] + [_resident(bn.shape)])
    out_specs = [cur, lse, pl.BlockSpec((n_seq, t_new, width), lambda b, r, i: (step(b, r, i), 0, 0))]
    out_shape = [jax.ShapeDtypeStruct(q.shape, BF16), jax.ShapeDtypeStruct((n, dil, sub, LANES), F32),
                 jax.ShapeDtypeStruct((steps * n_seq, t_new, width), BF16)]
    operands = [q, k, k, v, v, bias, qn, kn, vn, *views, *sbiases, bn]
    hgrn = None
    if rec is not None:
        sq, g_log, kf, iv, og, rec_heads = rec
        n_r, t_r, width_r = sq.shape
        c = min(HGRN_CHUNK, t_r)
        assert t_r % steps == 0 and (t_r // steps) % c == 0 and n_r <= HGRN_MAX_SEQS
        tri, masks, signs = _hgrn_tables(c)
        tok = pl.BlockSpec((n_r, t_r // steps, width_r), lambda b, r, i: (0, step(b, r, i), 0))
        in_specs += [tok] * 4 + [_resident(tri.shape), _resident(masks.shape), _resident(signs.shape),
                                 _resident((1, HGRN_HEAD))]
        out_specs += [tok, pl.BlockSpec((n_r, rec_heads, HGRN_HEAD, HGRN_HEAD), lambda b, r, i: (0, 0, 0, 0))]
        out_shape += [jax.ShapeDtypeStruct((n_r, t_r, width_r), BF16),
                      jax.ShapeDtypeStruct((n_r, rec_heads, HGRN_HEAD, HGRN_HEAD), F32)]
        operands += [sq, g_log, kf, iv, tri, masks, signs, og.reshape(1, HGRN_HEAD)]
        hgrn = (c, rec_heads)
    return pl.pallas_call(
        functools.partial(_attn_kernel, n_heads=n_heads, n_blk=n_blk, n_seq=n_seq, hgrn=hgrn),
        grid=(n, dil, steps_i),
        in_specs=in_specs,
        out_specs=out_specs,
        out_shape=out_shape,
        compiler_params=pltpu.CompilerParams(dimension_semantics=("arbitrary", "arbitrary", "arbitrary"),
                                             vmem_limit_bytes=V7X_VMEM_LIMIT),
        name=f"attn_g{g}",
    )(*operands)


def _hgrn_tables(c):
    t = np.arange(c)
    tri = (t[None, :] <= t[:, None]).astype(np.float32)
    masks = [np.eye(c, dtype=bool)]
    signs = []
    w = c // 2
    while w >= 1:
        blk = t // (2 * w)
        late = (t // w) % 2 == 1
        masks.append((blk[:, None] == blk[None, :]) & late[:, None] & ~late[None, :])
        if w < 8:
            signs.append(np.where(late, 1.0, -1.0))
        w //= 2
    masks = np.stack(masks).astype(np.float32)
    signs = np.repeat(np.stack(signs).reshape(-1, 1), HGRN_HEAD, axis=1).astype(np.float32)
    return jnp.asarray(tri, BF16), jnp.asarray(masks), jnp.asarray(signs)


def _level_exponents(b, g, c):
    width = b.shape[1]
    coarse, fine = [], []
    w = c // 2
    while w >= 8:
        parts = []
        for k in range(c // (2 * w)):
            lo, mid, hi = k * 2 * w, k * 2 * w + w, (k + 1) * 2 * w
            ref = b[mid - 1:mid]
            parts.append((slice(lo, mid), False, ref - b[lo:mid]))
            parts.append((slice(mid, hi), True, b[mid:hi] - ref))
        coarse.append(parts)
        w //= 2
    b3 = b.reshape(c // 8, 8, width)
    sub = lax.broadcasted_iota(jnp.int32, b3.shape, 1)
    if c >= 8:
        fine.append((b3 - jnp.broadcast_to(b3[:, 3:4], b3.shape)).reshape(c, width))
    ref = jnp.where(sub < 4, jnp.broadcast_to(b3[:, 1:2], b3.shape), jnp.broadcast_to(b3[:, 5:6], b3.shape))
    fine.append((b3 - ref).reshape(c, width))
    row = lax.broadcasted_iota(jnp.int32, b.shape, 0)
    fine.append(jnp.where(row % 2 == 1, g, 0.0))
    return coarse, fine


def _hgrn_chunks(seqs, tri, masks, signs, og, c):
    n_lev = masks.shape[0] - 1
    tri_v = tri[...]
    og = og[...]
    pre = []
    for sq, g, kf, iv, state in seqs:
        g1 = g.astype(BF16)
        r1 = g - g1.astype(F32)
        g2 = r1.astype(BF16)
        g3 = (r1 - g2.astype(F32)).astype(BF16)
        b = _dot(tri_v, g1) + _dot(tri_v, g2) + _dot(tri_v, g3)
        pre.append((b, _level_exponents(b, g, c)))
    heads = []
    for (sq, g, kf, iv, state), (b, (coarse, fine)) in zip(seqs, pre):
        e_cum = jnp.exp2(b)
        e_tail = jnp.exp2(b[c - 1:c] - b)
        for h in range(len(state)):
            hs = slice(h * HGRN_HEAD, (h + 1) * HGRN_HEAD)
            sqh, kfh, ivh = sq[:, hs], kf[:, hs], iv[:, hs]
            sqf, kff = sqh.astype(F32), kfh.astype(F32)
            eb = e_cum[:, hs]
            prods = [_dot_nt(sqh, kfh)]
            for parts in coarse:
                x = jnp.concatenate([(sqf if late else kff)[rows] * jnp.exp2(e[:, hs]) for rows, late, e in parts],
                                    axis=0).astype(BF16)
                prods.append(_dot_nt(x, x))
            for lv, e in enumerate(fine):
                sgn = signs[lv * c:(lv + 1) * c]
                x = (jnp.where(sgn > 0.0, sqf, kff) * jnp.exp2(e[:, hs] * sgn)).astype(BF16)
                prods.append(_dot_nt(x, x))
            o_inter = _dot((sqf * eb).astype(BF16), state[h].astype(BF16))
            kt = (kff * e_tail[:, hs]).astype(BF16)
            heads.append((prods, o_inter, _dot_tn(kt, ivh), eb, ivh, state[h]))
    results = []
    for prods, o_inter, upd, eb, ivh, st in heads:
        att = masks[0] * prods[0]
        for lv in range(n_lev):
            att = att + masks[1 + lv] * prods[1 + lv]
        o = o_inter + _dot(att.astype(BF16), ivh)
        decay = jnp.broadcast_to(eb[c - 1:c, :], (HGRN_HEAD, HGRN_HEAD)).T
        results.append((o * lax.rsqrt(jnp.mean(o * o, axis=-1, keepdims=True) + EPS) * og, decay * st + upd))
    n_h = len(seqs[0][4])
    return [([r[0] for r in results[i * n_h:(i + 1) * n_h]], [r[1] for r in results[i * n_h:(i + 1) * n_h]])
            for i in range(len(seqs))]


def _hgrn_prompt_part(sq_ref, g_ref, kf_ref, iv_ref, tri_ref, masks_ref, signs_ref, og_ref, o_ref, s_ref, first,
                      *, c, n_heads):
    @pl.when(first)
    def _():
        s_ref[...] = jnp.zeros_like(s_ref)

    def body(ci, carry):
        rows = pl.ds(pl.multiple_of(ci * c, c), c)
        seqs = [(sq_ref[b, rows, :], g_ref[b, rows, :], kf_ref[b, rows, :], iv_ref[b, rows, :],
                 [s_ref[b, h] for h in range(n_heads)]) for b in range(sq_ref.shape[0])]
        res = _hgrn_chunks(seqs, tri_ref, masks_ref, signs_ref, og_ref, c)
        for b, (outs, new_state) in enumerate(res):
            for h in range(n_heads):
                s_ref[b, h] = new_state[h]
                o_ref[b, rows, h * HGRN_HEAD:(h + 1) * HGRN_HEAD] = outs[h].astype(o_ref.dtype)
        return carry

    lax.fori_loop(0, sq_ref.shape[1] // c, body, 0)


def _hgrn_sample_kernel(sq_ref, g_ref, kf_ref, iv_ref, s0_ref, tri_ref, masks_ref, signs_ref, og_ref, o_ref, s_ref,
                        *, c, n_heads):
    seqs = [(sq_ref[b], g_ref[b], kf_ref[b], iv_ref[b], [s0_ref[b, h] for h in range(n_heads)])
            for b in range(sq_ref.shape[0])]
    res = _hgrn_chunks(seqs, tri_ref, masks_ref, signs_ref, og_ref, c)
    for b, (outs, new_state) in enumerate(res):
        for h in range(n_heads):
            s_ref[b, h] = new_state[h]
            o_ref[b, :, h * HGRN_HEAD:(h + 1) * HGRN_HEAD] = outs[h].astype(o_ref.dtype)


def _hgrn_sample_call(sq, g, kf, iv, s0, og, n, t, n_heads):
    width = n_heads * HGRN_HEAD
    bn = min(HGRN_SAMPLE_SEQS, n)
    assert n % bn == 0 and t % 8 == 0
    tri, masks, signs = _hgrn_tables(t)
    tok = pl.BlockSpec((bn, t, width), lambda i: (i, 0, 0))
    st = pl.BlockSpec((bn, n_heads, HGRN_HEAD, HGRN_HEAD), lambda i: (i, 0, 0, 0))
    o, s = pl.pallas_call(
        functools.partial(_hgrn_sample_kernel, c=t, n_heads=n_heads),
        grid=(n // bn,),
        in_specs=[tok, tok, tok, tok, st, _resident(tri.shape), _resident(masks.shape), _resident(signs.shape),
                  _resident((1, HGRN_HEAD))],
        out_specs=[tok, st],
        out_shape=[jax.ShapeDtypeStruct((n, t, width), BF16),
                   jax.ShapeDtypeStruct((n, n_heads, HGRN_HEAD, HGRN_HEAD), F32)],
        compiler_params=pltpu.CompilerParams(dimension_semantics=("arbitrary",), vmem_limit_bytes=V7X_VMEM_LIMIT),
        name="hgrn_sample",
    )(sq.reshape(n, t, width), g.reshape(n, t, width), kf.reshape(n, t, width), iv.reshape(n, t, width),
      s0, tri, masks, signs, og.reshape(1, HGRN_HEAD))
    return o.reshape(n * t, width), s


def _load_tokens(ref, scr):
    bs, dil, rows, width = ref.shape
    if dil == 1:
        return ref[...].reshape(bs * rows, width).astype(F32)
    for r in range(dil):
        for c in range(width // LANES):
            scr[c, pl.ds(r, rows, stride=dil), :] = ref[0, r, :, c * LANES:(c + 1) * LANES].astype(F32)
    return jnp.concatenate([scr[c] for c in range(width // LANES)], axis=1)


def _out_kernel(*refs, n_att):
    att = refs[:2 * n_att] if n_att > 1 else refs[:1]
    n_in = len(att)
    sza_ref, ob_ref, szb_ref, sga_ref, sgb_ref, x_ref, gate_ref, ex_ref, wa_ref, wb_ref, wo_ref, y_ref, scr = refs[n_in:]
    if n_att > 1:
        ls_ = [_load_tokens(r, scr) for r in att[n_att:]]
        m = functools.reduce(jnp.maximum, ls_)
        es = [jnp.exp2(l - m) for l in ls_]
        inv = 1.0 / functools.reduce(lambda a, b: a + b, es)
        alphas = [_dot((e * inv).astype(BF16), ex_ref[...]) for e in es]
        o_a = functools.reduce(lambda a, b: a + b, [al * _load_tokens(r, scr) for al, r in zip(alphas, att[:n_att])])
    else:
        o_a = _load_tokens(att[0], scr)
    branch_a = _dot((o_a * sza_ref[...].astype(F32)).astype(BF16), wa_ref[...])
    branch_b = _dot((ob_ref[...].astype(F32) * szb_ref[...].astype(F32)).astype(BF16), wb_ref[...])
    merged = sga_ref[...].astype(F32) * branch_a + sgb_ref[...].astype(F32) * branch_b
    upd = _dot(merged.astype(BF16), wo_ref[...])
    x = x_ref[...]
    y_ref[...] = x + gate_ref[...] * upd.reshape(x.shape)


def _out_call(att_inputs, sza, ob, szb, sga, sgb, x, gate, wa_bf, wb_bf, wo_bf):
    s, t, d = x.shape
    bs, bt = _token_tiling(s, t)
    tm = bs * bt
    n_t = t // bt

    def tok_spec(a):
        return pl.BlockSpec((tm, a.shape[1]), lambda i, j: (i * n_t + j, 0))

    def att_spec(a):
        dil = a.shape[1]
        assert dil == 1 or bs == 1
        return pl.BlockSpec((bs, dil, bt // dil, a.shape[3]), lambda i, j: (i, 0, j, 0))

    n_att = (len(att_inputs) + 1) // 2
    toks = [sza, ob, szb, sga, sgb]
    width = att_inputs[0].shape[3]
    lane_head = np.arange(LANES)[:, None] == (np.arange(width) // HEAD_DIM)[None, :]
    expand = jnp.asarray(lane_head, BF16)
    return pl.pallas_call(
        functools.partial(_out_kernel, n_att=n_att),
        grid=(s // bs, n_t),
        in_specs=[att_spec(a) for a in att_inputs] + [tok_spec(a) for a in toks]
        + [pl.BlockSpec((bs, bt, d), lambda i, j: (i, j, 0)),
           pl.BlockSpec((bs, 1, d), lambda i, j: (i, 0, 0)),
           _resident(expand.shape), _resident(wa_bf.shape), _resident(wb_bf.shape), _resident(wo_bf.shape)],
        out_specs=pl.BlockSpec((bs, bt, d), lambda i, j: (i, j, 0)),
        out_shape=jax.ShapeDtypeStruct((s, t, d), F32),
        scratch_shapes=[pltpu.VMEM((width // LANES, tm, LANES), F32)],
        compiler_params=pltpu.CompilerParams(dimension_semantics=("arbitrary", "arbitrary"),
                                             vmem_limit_bytes=V7X_VMEM_LIMIT),
        name="out",
    )(*att_inputs, *toks, x, gate, expand, wa_bf, wb_bf, wo_bf)


def _kvnew_kernel(k_ref, v_ref, *out_refs):
    width = out_refs[0].shape[2]
    for g, o_ref in enumerate(out_refs):
        for kv, src in enumerate((k_ref, v_ref)):
            for t in range(src.shape[1]):
                o_ref[t, kv] = src[:, t, g * width:(g + 1) * width].T


def _kvnew_call(kn, vn, n_groups):
    n, t, cols = kn.shape
    return pl.pallas_call(
        _kvnew_kernel,
        out_shape=[jax.ShapeDtypeStruct((t, 2, cols // n_groups, n), F32) for _ in range(n_groups)],
        compiler_params=pltpu.CompilerParams(vmem_limit_bytes=V7X_VMEM_LIMIT),
        name="kvnew",
    )(kn, vn)


def kernel(x_prompt, x_sample, cache_kv_g0, cache_kv_g1, cache_kv_g2, state_hgrn, c_prompt, c_sample, norm_gain, w_ada, b_ada, w_in, q_norm_gain, k_norm_gain, hgrn_lb_logits, hgrn_out_norm_gain, w_branch_a, w_branch_b, w_out):
    depth = w_in.shape[0]
    assert depth == 1, "single-layer step"
    n_p, t_p, d = x_prompt.shape
    n_s, t_s, _ = x_sample.shape
    a_out = w_branch_a.shape[1]
    n_heads = a_out // HEAD_DIM
    b_f = w_branch_b.shape[1]
    b_heads = b_f // HGRN_HEAD
    slopes = _alibi_slopes(n_heads)

    n_c = n_p + n_s
    pad = (-n_c) % 8
    c_all = jnp.concatenate([c_prompt, c_sample, jnp.zeros((pad, d), F32)], axis=0)
    ada = _ada_call(c_all, w_ada[0], b_ada[0])
    shift, scale, gate = ada[:, :d], ada[:, d:2 * d], ada[:, 2 * d:]

    w_in_bf = w_in[0].astype(BF16)
    wa_bf, wb_bf, wo_bf = w_branch_a[0].astype(BF16), w_branch_b[0].astype(BF16), w_out[0].astype(BF16)
    qg = (jnp.tile(q_norm_gain[0], n_heads) * (ATTN_SCALE * LOG2E)).reshape(1, a_out)
    kg = jnp.tile(k_norm_gain[0], n_heads).reshape(1, a_out)
    og = hgrn_out_norm_gain[0]

    def cond(lo, n):
        return [a[lo:lo + n].reshape(n, 1, d) for a in (shift, scale, gate)]

    sh, sc, gt_p = cond(0, n_p)
    outs = _proj_call(x_prompt, sh, sc, norm_gain[0], w_in_bf, qg, kg, hgrn_lb_logits, n_groups=N_GROUPS,
                      dils=DILATIONS, a_out=a_out, b_f=b_f, kv_dtype=BF16,
                      windows=tuple(min(w, t_p) for w in WINDOWS))
    qs, ks, vs = outs[0:N_GROUPS], outs[N_GROUPS:2 * N_GROUPS], outs[2 * N_GROUPS:3 * N_GROUPS]
    act_p, kv_rows = outs[3 * N_GROUPS:3 * N_GROUPS + 8], outs[3 * N_GROUPS + 8:]
    sh, sc, gt_s = cond(n_p, n_s)
    outs = _proj_call(x_sample, sh, sc, norm_gain[0], w_in_bf, qg, kg, hgrn_lb_logits, n_groups=N_GROUPS,
                      dils=(1,) * N_GROUPS, a_out=a_out, b_f=b_f, kv_dtype=F32)
    qn, kn, vn = outs[:3]
    act_s = outs[3:3 + 8]

    steps = [_attn_steps(qs[gi])[2] for gi in range(N_GROUPS)]
    assert n_s % sum(steps) == 0, "sample sequences must spread evenly over the prompt attention grid steps"
    n_seq = n_s // sum(steps)
    sample = (qn, kn, vn, (cache_kv_g0[0], cache_kv_g1[0], cache_kv_g2[0]))
    sza, sqb, g, kf, ib, szb, sga, sgb = act_p
    carrier = int(np.argmax(steps))
    rec = tuple(a.reshape(n_p, t_p, b_f) for a in (sqb, g, kf, ib)) + (og, b_heads)
    att = [_attn_call(qs[gi], ks[gi], vs[gi], gi, n_heads, slopes, sample, sum(steps[:gi]), n_seq,
                      rec if gi == carrier else None) for gi in range(N_GROUPS)]
    o_a = jnp.concatenate([a[2] for a in att], axis=0)
    ob, hgrn_p = att[carrier][3].reshape(n_p * t_p, b_f), att[carrier][4]

    y_p = _out_call([a[0] for a in att] + [a[1] for a in att], sza, ob, szb, sga, sgb, x_prompt, gt_p,
                    wa_bf, wb_bf, wo_bf)
    kv_p = [jnp.transpose(a.reshape(n_p, 2, n_heads, HEAD_DIM, a.shape[3]), (0, 4, 1, 2, 3))[None]
            for a in kv_rows]

    sza, sqb, g, kf, ib, szb, sga, sgb = act_s
    ob, hgrn_s = _hgrn_sample_call(sqb, g, kf, ib, state_hgrn[0], og, n_s, t_s, b_heads)
    y_s = _out_call([o_a.reshape(n_s, 1, t_s, a_out)], sza, ob, szb, sga, sgb, x_sample, gt_s, wa_bf, wb_bf, wo_bf)
    kv_s = [jnp.transpose(a.reshape(t_s, 2, n_heads, HEAD_DIM, n_s), (4, 0, 1, 2, 3))[None]
            for a in _kvnew_call(kn, vn, N_GROUPS)]

    return (y_p, y_s, kv_p[0], kv_p[1], kv_p[2], hgrn_p[None], kv_s[0], kv_s[1], kv_s[2], hgrn_s[None])
```

```python
import functools

import numpy as np
import jax
import jax.numpy as jnp
from jax import lax
from jax.experimental import pallas as pl
from jax.experimental.pallas import tpu as pltpu

F32 = jnp.float32
BF16 = jnp.bfloat16

HEAD_DIM = 64
N_GROUPS = 3
WINDOWS = (128, 512, 2048)
DILATIONS = (1, 4, 16)
N_BACK = 128
ATTN_BLOCK = 128
ATTN_BLOCKS_PER_STEP = 4
ATTN_SCALE = HEAD_DIM ** -0.5
HGRN_HEAD = 128
LANES = 128
SUBLANES = 8
BF16_ROWS = 16
LOG2E = 1.4426950408889634
EPS = 1e-6
NEG = -1e30

V7X_VMEM_LIMIT = 60 * 1024 * 1024
PROJ_TOKENS = 512
HGRN_CHUNK = 64
HGRN_MAX_SEQS = 4
HGRN_SAMPLE_SEQS = 8


def _sigmoid(x):
    return 1.0 / (1.0 + jnp.exp(-x))


def _silu(x):
    return x * _sigmoid(x)


def _dot(a, b):
    return jnp.dot(a, b, preferred_element_type=F32)


def _dot_nt(a, b):
    return lax.dot_general(a, b, (((1,), (1,)), ((), ())), preferred_element_type=F32)


def _dot_tn(a, b):
    return lax.dot_general(a, b, (((0,), (0,)), ((), ())), preferred_element_type=F32)


def _alibi_slopes(n_heads_per_group):
    a_heads = N_GROUPS * n_heads_per_group
    s = 2.0 ** (-8.0 * (np.arange(a_heads) + 1) / a_heads)
    return s.astype(np.float32).reshape(N_GROUPS, n_heads_per_group)


def _resident(shape):
    nd = len(shape)
    return pl.BlockSpec(shape, lambda *_: (0,) * nd, pipeline_mode=pl.Buffered(1))


def _ada_kernel(c_ref, w_ref, b_ref, o_ref):
    o_ref[...] = _dot(_silu(c_ref[...]).astype(BF16), w_ref[...].astype(BF16)) + b_ref[...]


def _ada_call(c, w_ada, b_ada):
    n, d = c.shape
    cols = w_ada.shape[1]
    blk = d
    return pl.pallas_call(
        _ada_kernel,
        grid=(cols // blk,),
        in_specs=[pl.BlockSpec((n, d), lambda j: (0, 0)),
                  pl.BlockSpec((d, blk), lambda j: (0, j)),
                  pl.BlockSpec((1, blk), lambda j: (0, j))],
        out_specs=pl.BlockSpec((n, blk), lambda j: (0, j)),
        out_shape=jax.ShapeDtypeStruct((n, cols), F32),
        name="ada",
    )(c, w_ada, b_ada.reshape(1, cols))


def _store_dilated(out_ref, val, scr, dil):
    bs, d, rows, width = out_ref.shape
    if dil == 1:
        out_ref[...] = val.reshape(bs, 1, rows, width).astype(out_ref.dtype)
        return
    for c in range(width // LANES):
        scr[c] = val[:, c * LANES:(c + 1) * LANES]
    for r in range(dil):
        for c in range(width // LANES):
            out_ref[0, r, :, c * LANES:(c + 1) * LANES] = scr[c, pl.ds(r, rows, stride=dil), :].astype(out_ref.dtype)


def _proj_kernel(x_ref, shift_ref, scale_ref, ng_ref, w_ref, qg_ref, kg_ref, lbl_ref, *refs,
                 n_groups, dils, a_out, d_model, merged):
    n_qkv = 3 if merged else 3 * n_groups
    qkv_refs = refs[:n_qkv]
    sza_ref, sqb_ref, g_ref, kf_ref, ib_ref, szb_ref, sga_ref, sgb_ref = refs[n_qkv:n_qkv + 8]
    win_refs, scr = refs[n_qkv + 8:-1], refs[-1]
    x = x_ref[...]
    ms = jnp.mean(x * x, axis=-1, keepdims=True)
    h = x * lax.rsqrt(ms + EPS) * ng_ref[...]
    h = h * (1.0 + scale_ref[...]) + shift_ref[...]
    tm = h.shape[0] * h.shape[1]
    hb = h.reshape(tm, d_model).astype(BF16)
    tile = a_out

    def proj(c0):
        return _dot(hb, w_ref[:, c0:c0 + tile])

    def head_norm(u, gain):
        u2 = u * u
        low = lax.broadcasted_iota(jnp.int32, (tm, LANES), 1) < HEAD_DIM
        outs = []
        for c in range(tile // LANES):
            s = u2[:, c * LANES:(c + 1) * LANES]
            lo = jnp.sum(jnp.where(low, s, 0.0), axis=-1, keepdims=True)
            hi = jnp.sum(jnp.where(low, 0.0, s), axis=-1, keepdims=True)
            r_lo = lax.rsqrt(lo * (1.0 / HEAD_DIM) + EPS)
            r_hi = lax.rsqrt(hi * (1.0 / HEAD_DIM) + EPS)
            outs.append(u[:, c * LANES:(c + 1) * LANES] * jnp.where(low, r_lo, r_hi))
        return jnp.concatenate(outs, axis=1) * gain

    c0 = 0
    for kind in range(3):
        for g in range(n_groups):
            u = proj(c0)
            if kind == 0:
                u = head_norm(u, qg_ref[...])
            elif kind == 1:
                u = head_norm(u, kg_ref[...])
            if merged:
                ref = qkv_refs[kind]
                ref[:, :, g * tile:(g + 1) * tile] = u.reshape(ref.shape[0], ref.shape[1], tile).astype(ref.dtype)
            else:
                _store_dilated(qkv_refs[kind * n_groups + g], u, scr, dils[g])
            if win_refs and kind > 0:
                cols = win_refs[g].shape[3]
                win_refs[g][0, kind - 1] = u[tm - cols:, :].T
            c0 += tile
    sza_ref[...] = _silu(proj(c0)).astype(BF16)
    c0 += tile
    sqb_ref[...] = _silu(proj(c0)).astype(BF16)
    c0 += tile
    lbl = lbl_ref[...]
    lbe = jnp.exp(lbl - jnp.max(lbl, axis=0, keepdims=True))
    lb = lbe[0:1] / jnp.sum(lbe, axis=0, keepdims=True)
    fr = proj(c0)
    g_ref[...] = jnp.log2(lb + (1.0 - lb) * _sigmoid(fr))
    kf_ref[...] = ((1.0 - lb) * _sigmoid(-fr)).astype(BF16)
    c0 += tile
    ib_ref[...] = proj(c0).astype(BF16)
    c0 += tile
    szb_ref[...] = _silu(proj(c0)).astype(BF16)
    c0 += tile
    for j in range(d_model // tile):
        sga_ref[:, j * tile:(j + 1) * tile] = _sigmoid(proj(c0)).astype(BF16)
        c0 += tile
    for j in range(d_model // tile):
        sgb_ref[:, j * tile:(j + 1) * tile] = _sigmoid(proj(c0)).astype(BF16)
        c0 += tile


def _token_tiling(s, t):
    if t >= PROJ_TOKENS:
        assert t % PROJ_TOKENS == 0
        return 1, PROJ_TOKENS
    bs = min(s, PROJ_TOKENS // t)
    assert s % bs == 0 and t % SUBLANES == 0
    return bs, t


def _proj_call(x, shift, scale, norm_gain, w_in_bf, qg, kg, lb_logits, *, n_groups, dils, a_out, b_f, kv_dtype,
               windows=()):
    s, t, d = x.shape
    bs, bt = _token_tiling(s, t)
    tm = bs * bt
    n_t = t // bt
    ntok = s * t
    grid = (s // bs, n_t)
    assert all(dl == 1 or (bs == 1 and bt % (dl * BF16_ROWS) == 0) for dl in dils)

    def tok_spec(cols):
        return pl.BlockSpec((tm, cols), lambda i, j: (i * n_t + j, 0))

    merged = all(dl == 1 for dl in dils)
    if merged:
        qkv_shapes = [jax.ShapeDtypeStruct((s, t, n_groups * a_out), dt) for dt in (BF16, kv_dtype, kv_dtype)]
        qkv_specs = [pl.BlockSpec((bs, bt, n_groups * a_out), lambda i, j: (i, j, 0))] * 3
    else:
        qkv_dtypes = [BF16] * n_groups + [kv_dtype] * (2 * n_groups)
        qkv_shapes = [jax.ShapeDtypeStruct((s, dils[g], t // dils[g], a_out), dt)
                      for g, dt in zip(list(range(n_groups)) * 3, qkv_dtypes)]
        qkv_specs = [pl.BlockSpec((bs, dils[g], bt // dils[g], a_out), lambda i, j: (i, 0, j, 0))
                     for g in list(range(n_groups)) * 3]
    out_cols = [(a_out, BF16), (b_f, BF16), (b_f, F32), (b_f, BF16), (b_f, BF16), (b_f, BF16), (d, BF16), (d, BF16)]
    win_shapes, win_specs = [], []
    for w in windows:
        cols = min(w, bt)
        assert bs == 1 and w % cols == 0 and t % cols == 0
        first = (t - w) // cols
        per_step = bt // cols
        win_shapes.append(jax.ShapeDtypeStruct((s, 2, a_out, w), F32))
        win_specs.append(pl.BlockSpec(
            (1, 2, a_out, cols),
            lambda i, j, first=first, per_step=per_step: (i, 0, 0, jnp.maximum((j + 1) * per_step - 1 - first, 0))))
    kern = functools.partial(_proj_kernel, n_groups=n_groups, dils=tuple(dils), a_out=a_out, d_model=d, merged=merged)
    return pl.pallas_call(
        kern,
        grid=grid,
        in_specs=[pl.BlockSpec((bs, bt, d), lambda i, j: (i, j, 0)),
                  pl.BlockSpec((bs, 1, d), lambda i, j: (i, 0, 0)),
                  pl.BlockSpec((bs, 1, d), lambda i, j: (i, 0, 0)),
                  _resident((1, d)),
                  _resident(w_in_bf.shape),
                  _resident(qg.shape),
                  _resident(kg.shape),
                  _resident(lb_logits.shape)],
        out_specs=qkv_specs + [tok_spec(c) for c, _ in out_cols] + win_specs,
        out_shape=qkv_shapes + [jax.ShapeDtypeStruct((ntok, c), dt) for c, dt in out_cols] + win_shapes,
        scratch_shapes=[pltpu.VMEM((a_out // LANES, tm, LANES), F32)],
        compiler_params=pltpu.CompilerParams(dimension_semantics=("arbitrary", "arbitrary"),
                                             vmem_limit_bytes=V7X_VMEM_LIMIT),
        name="proj",
    )(x, shift, scale, norm_gain.reshape(1, d), w_in_bf, qg, kg, lb_logits)


def _prompt_attn(q_ref, kp_ref, kc_ref, vp_ref, vc_ref, bias_ref, o_ref, lse_ref, *, n_heads, n_blk):
    first = (pl.program_id(2) == 0).astype(jnp.int32)
    lane = lax.broadcasted_iota(jnp.int32, (ATTN_BLOCK, 2 * HEAD_DIM), 1)
    low = lane < HEAD_DIM
    for blk in range(n_blk):
        rows = slice(blk * ATTN_BLOCK, (blk + 1) * ATTN_BLOCK)
        before = slice((blk - 1) * ATTN_BLOCK, blk * ATTN_BLOCK)
        table = first if blk == 0 else 0
        scores, values = [], []
        for hp in range(n_heads // 2):
            sl = slice(hp * 2 * HEAD_DIM, (hp + 1) * 2 * HEAD_DIM)
            q2 = q_ref[rows, sl]
            k_prev = kp_ref[:, sl] if blk == 0 else kc_ref[before, sl]
            v_prev = vp_ref[:, sl] if blk == 0 else vc_ref[before, sl]
            k2 = jnp.concatenate([k_prev, kc_ref[rows, sl]], axis=0)
            values.append(jnp.concatenate([v_prev, vc_ref[rows, sl]], axis=0))
            for par in range(2):
                keep = low if par == 0 else jnp.logical_not(low)
                qm = jnp.where(keep, q2, jnp.zeros_like(q2))
                scores.append(_dot_nt(qm, k2) + bias_ref[table, 2 * hp + par])
        soft = []
        for s in scores:
            m = jnp.max(s, axis=-1, keepdims=True)
            p = jnp.exp2(s - m)
            soft.append((p.astype(BF16), m, jnp.sum(p, axis=-1, keepdims=True)))
        outs = [_dot(p, values[i // 2]) for i, (p, m, l) in enumerate(soft)]
        lse = jnp.zeros((ATTN_BLOCK, LANES), F32)
        for hp in range(n_heads // 2):
            sl = slice(hp * 2 * HEAD_DIM, (hp + 1) * 2 * HEAD_DIM)
            (_, m0, l0), (_, m1, l1) = soft[2 * hp], soft[2 * hp + 1]
            o_ref[rows, sl] = jnp.where(low, outs[2 * hp] / l0, outs[2 * hp + 1] / l1).astype(o_ref.dtype)
            lse = jnp.where(lane == 2 * hp, m0 + jnp.log2(l0), lse)
            lse = jnp.where(lane == 2 * hp + 1, m1 + jnp.log2(l1), lse)
        lse_ref[rows, :] = lse


def _sample_attn(q_ref, k_ref, v_ref, caches, biases, bn_ref, o_ref, seq, *, n_heads):
    width = n_heads * HEAD_DIM
    per_head = []
    for h in range(n_heads):
        scores, values = [], []
        for g in range(N_GROUPS):
            hs = slice(g * width + h * HEAD_DIM, g * width + (h + 1) * HEAD_DIM)
            qh = q_ref[seq, :, hs]
            scores.append(_dot(qh, caches[g][seq, 0, h].astype(BF16)) + biases[g][h])
            values.append(caches[g][seq, 1, h].astype(BF16))
            scores.append(_dot_nt(qh, k_ref[seq, :, hs].astype(BF16)) + bn_ref[g, h])
            values.append(v_ref[seq, :, hs].astype(BF16))
        per_head.append((scores, values))
    soft = []
    for scores, values in per_head:
        m = functools.reduce(jnp.maximum, [jnp.max(s, axis=-1, keepdims=True) for s in scores])
        ps = [jnp.exp2(s - m) for s in scores]
        l = functools.reduce(lambda a, b: a + b, [jnp.sum(p, axis=-1, keepdims=True) for p in ps])
        soft.append(([p.astype(BF16) for p in ps], l))
    for h, ((ps, l), (_, values)) in enumerate(zip(soft, per_head)):
        parts = [(_dot_nt(p, v) if i % 2 == 0 else _dot(p, v)) for i, (p, v) in enumerate(zip(ps, values))]
        acc = functools.reduce(lambda a, b: a + b, parts)
        o_ref[seq, :, h * HEAD_DIM:(h + 1) * HEAD_DIM] = (acc / l).astype(o_ref.dtype)


def _attn_kernel(*refs, n_heads, n_blk, n_seq, hgrn):
    prompt_in, (qn_ref, kn_ref, vn_ref), refs = refs[:6], refs[6:9], refs[9:]
    caches, biases = refs[:N_GROUPS], refs[N_GROUPS:2 * N_GROUPS]
    bn_ref, refs = refs[2 * N_GROUPS], refs[2 * N_GROUPS + 1:]
    if hgrn is not None:
        rec_in, refs = refs[:8], refs[8:]
        o_ref, lse_ref, os_ref, ob_ref, st_ref = refs
    else:
        o_ref, lse_ref, os_ref = refs
    for seq in range(n_seq):
        _sample_attn(qn_ref, kn_ref, vn_ref, caches, biases, bn_ref, os_ref, seq, n_heads=n_heads)
    if hgrn is not None:
        first = (pl.program_id(0) == 0) & (pl.program_id(1) == 0) & (pl.program_id(2) == 0)
        _hgrn_prompt_part(*rec_in, ob_ref, st_ref, first, c=hgrn[0], n_heads=hgrn[1])
    _prompt_attn(*prompt_in, o_ref, lse_ref, n_heads=n_heads, n_blk=n_blk)


def _attn_tables(dil, slopes_g):
    i = np.arange(ATTN_BLOCK)[:, None]
    j = np.arange(2 * ATTN_BLOCK)[None, :]
    delta = i + ATTN_BLOCK - j
    valid = (delta >= 0) & (delta <= N_BACK)
    dist = (delta * dil).astype(np.float32)
    bias = np.where(valid[None], -slopes_g[:, None, None] * dist[None] * LOG2E, NEG).astype(np.float32)
    first = np.where((j >= ATTN_BLOCK)[None], bias, NEG).astype(np.float32)
    return jnp.asarray(np.stack([bias, first]))


def _sattn_tables(n_heads, t_new, slopes, wbs):
    t = np.arange(t_new)[:, None]
    biases, bn = [], []
    for g in range(N_GROUPS):
        d, wb = DILATIONS[g], wbs[g]
        dist = wb + t - np.arange(wb)[None, :]
        valid = (dist % d == 0) & (dist // d <= N_BACK)
        biases.append(np.where(valid[None], -slopes[g][:, None, None] * dist[None] * LOG2E, NEG).astype(np.float32))
        dist = t - np.arange(t_new)[None, :]
        valid = (dist >= 0) & (dist % d == 0) & (dist // d <= N_BACK)
        bn.append(np.where(valid[None], -slopes[g][:, None, None] * dist[None] * LOG2E, NEG).astype(np.float32))
    return [jnp.asarray(b) for b in biases], jnp.asarray(np.stack(bn))


def _attn_steps(q):
    n, dil, sub, _ = q.shape
    nb = sub // ATTN_BLOCK
    n_blk = min(ATTN_BLOCKS_PER_STEP, nb)
    assert sub % ATTN_BLOCK == 0 and nb % n_blk == 0
    return n_blk, nb // n_blk, n * dil * (nb // n_blk)


def _attn_call(q, k, v, g, n_heads, slopes, sample, first_block, n_seq, rec=None):
    n, dil, sub, width = q.shape
    assert dil == DILATIONS[g] and width == n_heads * HEAD_DIM and n_heads <= LANES
    n_blk, steps_i, steps = _attn_steps(q)
    cur = pl.BlockSpec((None, None, n_blk * ATTN_BLOCK, width), lambda b, r, i: (b, r, i, 0))
    prev = pl.BlockSpec((None, None, ATTN_BLOCK, width), lambda b, r, i: (b, r, jnp.maximum(i * n_blk - 1, 0), 0))
    lse = pl.BlockSpec((None, None, n_blk * ATTN_BLOCK, LANES), lambda b, r, i: (b, r, i, 0))
    bias = _attn_tables(dil, slopes[g])

    qn, kn, vn, caches = sample
    t_new = qn.shape[1]
    wbs = [c.shape[1] for c in caches]
    sbiases, bn = _sattn_tables(n_heads, t_new, slopes, wbs)
    views = [jnp.transpose(c, (0, 2, 3, 4, 1)) for c in caches]

    def step(b, r, i):
        return (b * dil + r) * steps_i + i

    new = pl.BlockSpec((n_seq, t_new, N_GROUPS * width), lambda b, r, i: (first_block + step(b, r, i), 0, 0))
    cache_specs = [pl.BlockSpec((n_seq, 2, n_heads, HEAD_DIM, wb), lambda b, r, i: (first_block + step(b, r, i), 0, 0, 0, 0))
                   for wb in wbs]
    in_specs = ([cur, prev, cur, prev, cur, _resident(bias.shape)] + [new] * 3 + cache_specs
                + [_resident(b.shape) for b in sbiases] + [_resident(bn.shape)])
    out_specs = [cur, lse, pl.BlockSpec((n_seq, t_new, width), lambda b, r, i: (step(b, r, i), 0, 0))]
    out_shape = [jax.ShapeDtypeStruct(q.shape, BF16), jax.ShapeDtypeStruct((n, dil, sub, LANES), F32),
                 jax.ShapeDtypeStruct((steps * n_seq, t_new, width), BF16)]
    operands = [q, k, k, v, v, bias, qn, kn, vn, *views, *sbiases, bn]
    hgrn = None
    if rec is not None:
        sq, g_log, kf, iv, og, rec_heads = rec
        n_r, t_r, width_r = sq.shape
        c = min(HGRN_CHUNK, t_r)
        assert t_r % steps == 0 and (t_r // steps) % c == 0 and n_r <= HGRN_MAX_SEQS
        tri, masks, signs = _hgrn_tables(c)
        tok = pl.BlockSpec((n_r, t_r // steps, width_r), lambda b, r, i: (0, step(b, r, i), 0))
        in_specs += [tok] * 4 + [_resident(tri.shape), _resident(masks.shape), _resident(signs.shape),
                                 _resident((1, HGRN_HEAD))]
        out_specs += [tok, pl.BlockSpec((n_r, rec_heads, HGRN_HEAD, HGRN_HEAD), lambda b, r, i: (0, 0, 0, 0))]
        out_shape += [jax.ShapeDtypeStruct((n_r, t_r, width_r), BF16),
                      jax.ShapeDtypeStruct((n_r, rec_heads, HGRN_HEAD, HGRN_HEAD), F32)]
        operands += [sq, g_log, kf, iv, tri, masks, signs, og.reshape(1, HGRN_HEAD)]
        hgrn = (c, rec_heads)
    return pl.pallas_call(
        functools.partial(_attn_kernel, n_heads=n_heads, n_blk=n_blk, n_seq=n_seq, hgrn=hgrn),
        grid=(n, dil, steps_i),
        in_specs=in_specs,
        out_specs=out_specs,
        out_shape=out_shape,
        compiler_params=pltpu.CompilerParams(dimension_semantics=("arbitrary", "arbitrary", "arbitrary"),
                                             vmem_limit_bytes=V7X_VMEM_LIMIT),
        name=f"attn_g{g}",
    )(*operands)


def _hgrn_tables(c):
    t = np.arange(c)
    tri = (t[None, :] <= t[:, None]).astype(np.float32)
    masks = [np.eye(c, dtype=bool)]
    signs = []
    w = c // 2
    while w >= 1:
        blk = t // (2 * w)
        late = (t // w) % 2 == 1
        masks.append((blk[:, None] == blk[None, :]) & late[:, None] & ~late[None, :])
        if w < SUBLANES:
            signs.append(np.where(late, 1.0, -1.0))
        w //= 2
    masks = np.stack(masks).astype(np.float32)
    signs = np.repeat(np.stack(signs).reshape(-1, 1), HGRN_HEAD, axis=1).astype(np.float32)
    return jnp.asarray(tri, BF16), jnp.asarray(masks), jnp.asarray(signs)


def _level_exponents(b, g, c):
    width = b.shape[1]
    coarse, fine = [], []
    w = c // 2
    while w >= SUBLANES:
        parts = []
        for k in range(c // (2 * w)):
            lo, mid, hi = k * 2 * w, k * 2 * w + w, (k + 1) * 2 * w
            ref = b[mid - 1:mid]
            parts.append((slice(lo, mid), False, ref - b[lo:mid]))
            parts.append((slice(mid, hi), True, b[mid:hi] - ref))
        coarse.append(parts)
        w //= 2
    assert SUBLANES == 8, "the three finest levels below are written for 8-row registers"
    b3 = b.reshape(c // SUBLANES, SUBLANES, width)
    sub = lax.broadcasted_iota(jnp.int32, b3.shape, 1)
    if c >= SUBLANES:
        fine.append((b3 - jnp.broadcast_to(b3[:, 3:4], b3.shape)).reshape(c, width))
    ref = jnp.where(sub < 4, jnp.broadcast_to(b3[:, 1:2], b3.shape), jnp.broadcast_to(b3[:, 5:6], b3.shape))
    fine.append((b3 - ref).reshape(c, width))
    row = lax.broadcasted_iota(jnp.int32, b.shape, 0)
    fine.append(jnp.where(row % 2 == 1, g, 0.0))
    return coarse, fine


def _hgrn_chunks(seqs, tri, masks, signs, og, c):
    n_lev = masks.shape[0] - 1
    tri_v = tri[...]
    og = og[...]
    pre = []
    for sq, g, kf, iv, state in seqs:
        g1 = g.astype(BF16)
        r1 = g - g1.astype(F32)
        g2 = r1.astype(BF16)
        g3 = (r1 - g2.astype(F32)).astype(BF16)
        b = _dot(tri_v, g1) + _dot(tri_v, g2) + _dot(tri_v, g3)
        pre.append((b, _level_exponents(b, g, c)))
    heads = []
    for (sq, g, kf, iv, state), (b, (coarse, fine)) in zip(seqs, pre):
        e_cum = jnp.exp2(b)
        e_tail = jnp.exp2(b[c - 1:c] - b)
        for h in range(len(state)):
            hs = slice(h * HGRN_HEAD, (h + 1) * HGRN_HEAD)
            sqh, kfh, ivh = sq[:, hs], kf[:, hs], iv[:, hs]
            sqf, kff = sqh.astype(F32), kfh.astype(F32)
            eb = e_cum[:, hs]
            prods = [_dot_nt(sqh, kfh)]
            for parts in coarse:
                x = jnp.concatenate([(sqf if late else kff)[rows] * jnp.exp2(e[:, hs]) for rows, late, e in parts],
                                    axis=0).astype(BF16)
                prods.append(_dot_nt(x, x))
            for lv, e in enumerate(fine):
                sgn = signs[lv * c:(lv + 1) * c]
                x = (jnp.where(sgn > 0.0, sqf, kff) * jnp.exp2(e[:, hs] * sgn)).astype(BF16)
                prods.append(_dot_nt(x, x))
            o_inter = _dot((sqf * eb).astype(BF16), state[h].astype(BF16))
            kt = (kff * e_tail[:, hs]).astype(BF16)
            heads.append((prods, o_inter, _dot_tn(kt, ivh), eb, ivh, state[h]))
    results = []
    for prods, o_inter, upd, eb, ivh, st in heads:
        att = masks[0] * prods[0]
        for lv in range(n_lev):
            att = att + masks[1 + lv] * prods[1 + lv]
        o = o_inter + _dot(att.astype(BF16), ivh)
        decay = jnp.broadcast_to(eb[c - 1:c, :], (HGRN_HEAD, HGRN_HEAD)).T
        results.append((o * lax.rsqrt(jnp.mean(o * o, axis=-1, keepdims=True) + EPS) * og, decay * st + upd))
    n_h = len(seqs[0][4])
    return [([r[0] for r in results[i * n_h:(i + 1) * n_h]], [r[1] for r in results[i * n_h:(i + 1) * n_h]])
            for i in range(len(seqs))]


def _hgrn_prompt_part(sq_ref, g_ref, kf_ref, iv_ref, tri_ref, masks_ref, signs_ref, og_ref, o_ref, s_ref, first,
                      *, c, n_heads):
    @pl.when(first)
    def _():
        s_ref[...] = jnp.zeros_like(s_ref)

    def body(ci, carry):
        rows = pl.ds(pl.multiple_of(ci * c, c), c)
        seqs = [(sq_ref[b, rows, :], g_ref[b, rows, :], kf_ref[b, rows, :], iv_ref[b, rows, :],
                 [s_ref[b, h] for h in range(n_heads)]) for b in range(sq_ref.shape[0])]
        res = _hgrn_chunks(seqs, tri_ref, masks_ref, signs_ref, og_ref, c)
        for b, (outs, new_state) in enumerate(res):
            for h in range(n_heads):
                s_ref[b, h] = new_state[h]
                o_ref[b, rows, h * HGRN_HEAD:(h + 1) * HGRN_HEAD] = outs[h].astype(o_ref.dtype)
        return carry

    lax.fori_loop(0, sq_ref.shape[1] // c, body, 0)


def _hgrn_sample_kernel(sq_ref, g_ref, kf_ref, iv_ref, s0_ref, tri_ref, masks_ref, signs_ref, og_ref, o_ref, s_ref,
                        *, c, n_heads):
    seqs = [(sq_ref[b], g_ref[b], kf_ref[b], iv_ref[b], [s0_ref[b, h] for h in range(n_heads)])
            for b in range(sq_ref.shape[0])]
    res = _hgrn_chunks(seqs, tri_ref, masks_ref, signs_ref, og_ref, c)
    for b, (outs, new_state) in enumerate(res):
        for h in range(n_heads):
            s_ref[b, h] = new_state[h]
            o_ref[b, :, h * HGRN_HEAD:(h + 1) * HGRN_HEAD] = outs[h].astype(o_ref.dtype)


def _hgrn_sample_call(sq, g, kf, iv, s0, og, n, t, n_heads):
    width = n_heads * HGRN_HEAD
    bn = min(HGRN_SAMPLE_SEQS, n)
    assert n % bn == 0 and t % SUBLANES == 0
    tri, masks, signs = _hgrn_tables(t)
    tok = pl.BlockSpec((bn, t, width), lambda i: (i, 0, 0))
    st = pl.BlockSpec((bn, n_heads, HGRN_HEAD, HGRN_HEAD), lambda i: (i, 0, 0, 0))
    o, s = pl.pallas_call(
        functools.partial(_hgrn_sample_kernel, c=t, n_heads=n_heads),
        grid=(n // bn,),
        in_specs=[tok, tok, tok, tok, st, _resident(tri.shape), _resident(masks.shape), _resident(signs.shape),
                  _resident((1, HGRN_HEAD))],
        out_specs=[tok, st],
        out_shape=[jax.ShapeDtypeStruct((n, t, width), BF16),
                   jax.ShapeDtypeStruct((n, n_heads, HGRN_HEAD, HGRN_HEAD), F32)],
        compiler_params=pltpu.CompilerParams(dimension_semantics=("arbitrary",), vmem_limit_bytes=V7X_VMEM_LIMIT),
        name="hgrn_sample",
    )(sq.reshape(n, t, width), g.reshape(n, t, width), kf.reshape(n, t, width), iv.reshape(n, t, width),
      s0, tri, masks, signs, og.reshape(1, HGRN_HEAD))
    return o.reshape(n * t, width), s


def _load_tokens(ref, scr):
    bs, dil, rows, width = ref.shape
    if dil == 1:
        return ref[...].reshape(bs * rows, width).astype(F32)
    for r in range(dil):
        for c in range(width // LANES):
            scr[c, pl.ds(r, rows, stride=dil), :] = ref[0, r, :, c * LANES:(c + 1) * LANES].astype(F32)
    return jnp.concatenate([scr[c] for c in range(width // LANES)], axis=1)


def _out_kernel(*refs, n_att):
    att = refs[:2 * n_att] if n_att > 1 else refs[:1]
    n_in = len(att)
    sza_ref, ob_ref, szb_ref, sga_ref, sgb_ref, x_ref, gate_ref, ex_ref, wa_ref, wb_ref, wo_ref, y_ref, scr = refs[n_in:]
    if n_att > 1:
        ls_ = [_load_tokens(r, scr) for r in att[n_att:]]
        m = functools.reduce(jnp.maximum, ls_)
        es = [jnp.exp2(l - m) for l in ls_]
        inv = 1.0 / functools.reduce(lambda a, b: a + b, es)
        alphas = [_dot((e * inv).astype(BF16), ex_ref[...]) for e in es]
        o_a = functools.reduce(lambda a, b: a + b, [al * _load_tokens(r, scr) for al, r in zip(alphas, att[:n_att])])
    else:
        o_a = _load_tokens(att[0], scr)
    branch_a = _dot((o_a * sza_ref[...].astype(F32)).astype(BF16), wa_ref[...])
    branch_b = _dot((ob_ref[...].astype(F32) * szb_ref[...].astype(F32)).astype(BF16), wb_ref[...])
    merged = sga_ref[...].astype(F32) * branch_a + sgb_ref[...].astype(F32) * branch_b
    upd = _dot(merged.astype(BF16), wo_ref[...])
    x = x_ref[...]
    y_ref[...] = x + gate_ref[...] * upd.reshape(x.shape)


def _out_call(att_inputs, sza, ob, szb, sga, sgb, x, gate, wa_bf, wb_bf, wo_bf):
    s, t, d = x.shape
    bs, bt = _token_tiling(s, t)
    tm = bs * bt
    n_t = t // bt

    def tok_spec(a):
        return pl.BlockSpec((tm, a.shape[1]), lambda i, j: (i * n_t + j, 0))

    def att_spec(a):
        dil = a.shape[1]
        assert dil == 1 or bs == 1
        return pl.BlockSpec((bs, dil, bt // dil, a.shape[3]), lambda i, j: (i, 0, j, 0))

    n_att = (len(att_inputs) + 1) // 2
    toks = [sza, ob, szb, sga, sgb]
    width = att_inputs[0].shape[3]
    lane_head = np.arange(LANES)[:, None] == (np.arange(width) // HEAD_DIM)[None, :]
    expand = jnp.asarray(lane_head, BF16)
    return pl.pallas_call(
        functools.partial(_out_kernel, n_att=n_att),
        grid=(s // bs, n_t),
        in_specs=[att_spec(a) for a in att_inputs] + [tok_spec(a) for a in toks]
        + [pl.BlockSpec((bs, bt, d), lambda i, j: (i, j, 0)),
           pl.BlockSpec((bs, 1, d), lambda i, j: (i, 0, 0)),
           _resident(expand.shape), _resident(wa_bf.shape), _resident(wb_bf.shape), _resident(wo_bf.shape)],
        out_specs=pl.BlockSpec((bs, bt, d), lambda i, j: (i, j, 0)),
        out_shape=jax.ShapeDtypeStruct((s, t, d), F32),
        scratch_shapes=[pltpu.VMEM((width // LANES, tm, LANES), F32)],
        compiler_params=pltpu.CompilerParams(dimension_semantics=("arbitrary", "arbitrary"),
                                             vmem_limit_bytes=V7X_VMEM_LIMIT),
        name="out",
    )(*att_inputs, *toks, x, gate, expand, wa_bf, wb_bf, wo_bf)


def _kvnew_kernel(k_ref, v_ref, *out_refs):
    width = out_refs[0].shape[2]
    for g, o_ref in enumerate(out_refs):
        for kv, src in enumerate((k_ref, v_ref)):
            for t in range(src.shape[1]):
                o_ref[t, kv] = src[:, t, g * width:(g + 1) * width].T


def _kvnew_call(kn, vn, n_groups):
    n, t, cols = kn.shape
    return pl.pallas_call(
        _kvnew_kernel,
        out_shape=[jax.ShapeDtypeStruct((t, 2, cols // n_groups, n), F32) for _ in range(n_groups)],
        compiler_params=pltpu.CompilerParams(vmem_limit_bytes=V7X_VMEM_LIMIT),
        name="kvnew",
    )(kn, vn)


def kernel(x_prompt, x_sample, cache_kv_g0, cache_kv_g1, cache_kv_g2, state_hgrn, c_prompt, c_sample, norm_gain, w_ada, b_ada, w_in, q_norm_gain, k_norm_gain, hgrn_lb_logits, hgrn_out_norm_gain, w_branch_a, w_branch_b, w_out):
    depth = w_in.shape[0]
    assert depth == 1, "single-layer step"
    n_p, t_p, d = x_prompt.shape
    n_s, t_s, _ = x_sample.shape
    a_out = w_branch_a.shape[1]
    n_heads = a_out // HEAD_DIM
    b_f = w_branch_b.shape[1]
    b_heads = b_f // HGRN_HEAD
    slopes = _alibi_slopes(n_heads)

    n_c = n_p + n_s
    pad = (-n_c) % SUBLANES
    c_all = jnp.concatenate([c_prompt, c_sample, jnp.zeros((pad, d), F32)], axis=0)
    ada = _ada_call(c_all, w_ada[0], b_ada[0])
    shift, scale, gate = ada[:, :d], ada[:, d:2 * d], ada[:, 2 * d:]

    w_in_bf = w_in[0].astype(BF16)
    wa_bf, wb_bf, wo_bf = w_branch_a[0].astype(BF16), w_branch_b[0].astype(BF16), w_out[0].astype(BF16)
    qg = (jnp.tile(q_norm_gain[0], n_heads) * (ATTN_SCALE * LOG2E)).reshape(1, a_out)
    kg = jnp.tile(k_norm_gain[0], n_heads).reshape(1, a_out)
    og = hgrn_out_norm_gain[0]

    def cond(lo, n):
        return [a[lo:lo + n].reshape(n, 1, d) for a in (shift, scale, gate)]

    sh, sc, gt_p = cond(0, n_p)
    outs = _proj_call(x_prompt, sh, sc, norm_gain[0], w_in_bf, qg, kg, hgrn_lb_logits, n_groups=N_GROUPS,
                      dils=DILATIONS, a_out=a_out, b_f=b_f, kv_dtype=BF16,
                      windows=tuple(min(w, t_p) for w in WINDOWS))
    qs, ks, vs = outs[0:N_GROUPS], outs[N_GROUPS:2 * N_GROUPS], outs[2 * N_GROUPS:3 * N_GROUPS]
    act_p, kv_rows = outs[3 * N_GROUPS:3 * N_GROUPS + 8], outs[3 * N_GROUPS + 8:]
    sh, sc, gt_s = cond(n_p, n_s)
    outs = _proj_call(x_sample, sh, sc, norm_gain[0], w_in_bf, qg, kg, hgrn_lb_logits, n_groups=N_GROUPS,
                      dils=(1,) * N_GROUPS, a_out=a_out, b_f=b_f, kv_dtype=F32)
    qn, kn, vn = outs[:3]
    act_s = outs[3:3 + 8]

    steps = [_attn_steps(qs[gi])[2] for gi in range(N_GROUPS)]
    assert n_s % sum(steps) == 0, "sample sequences must spread evenly over the prompt attention grid steps"
    n_seq = n_s // sum(steps)
    sample = (qn, kn, vn, (cache_kv_g0[0], cache_kv_g1[0], cache_kv_g2[0]))
    sza, sqb, g, kf, ib, szb, sga, sgb = act_p
    carrier = int(np.argmax(steps))
    rec = tuple(a.reshape(n_p, t_p, b_f) for a in (sqb, g, kf, ib)) + (og, b_heads)
    att = [_attn_call(qs[gi], ks[gi], vs[gi], gi, n_heads, slopes, sample, sum(steps[:gi]), n_seq,
                      rec if gi == carrier else None) for gi in range(N_GROUPS)]
    o_a = jnp.concatenate([a[2] for a in att], axis=0)
    ob, hgrn_p = att[carrier][3].reshape(n_p * t_p, b_f), att[carrier][4]

    y_p = _out_call([a[0] for a in att] + [a[1] for a in att], sza, ob, szb, sga, sgb, x_prompt, gt_p,
                    wa_bf, wb_bf, wo_bf)
    kv_p = [jnp.transpose(a.reshape(n_p, 2, n_heads, HEAD_DIM, a.shape[3]), (0, 4, 1, 2, 3))[None]
            for a in kv_rows]

    sza, sqb, g, kf, ib, szb, sga, sgb = act_s
    ob, hgrn_s = _hgrn_sample_call(sqb, g, kf, ib, state_hgrn[0], og, n_s, t_s, b_heads)
    y_s = _out_call([o_a.reshape(n_s, 1, t_s, a_out)], sza, ob, szb, sga, sgb, x_sample, gt_s, wa_bf, wb_bf, wo_bf)
    kv_s = [jnp.transpose(a.reshape(t_s, 2, n_heads, HEAD_DIM, n_s), (4, 0, 1, 2, 3))[None]
            for a in _kvnew_call(kn, vn, N_GROUPS)]

    return (y_p, y_s, kv_p[0], kv_p[1], kv_p[2], hgrn_p[None], kv_s[0], kv_s[1], kv_s[2], hgrn_s[None])
```

```python
import functools

import numpy as np
import jax
import jax.numpy as jnp
from jax import lax
from jax.experimental import pallas as pl
from jax.experimental.pallas import tpu as pltpu

F32 = jnp.float32
BF16 = jnp.bfloat16

HEAD_DIM = 64
N_GROUPS = 3
WINDOWS = (128, 512, 2048)
DILATIONS = (1, 4, 16)
N_BACK = 128
ATTN_BLOCK = 128
ATTN_BLOCKS_PER_STEP = 4
CACHE_SLOTS = 3
ATTN_SCALE = HEAD_DIM ** -0.5
HGRN_HEAD = 128
LANES = 128
SUBLANES = 8
BF16_ROWS = 16
LOG2E = 1.4426950408889634
EPS = 1e-6
NEG = -1e30

V7X_VMEM_LIMIT = 60 * 1024 * 1024
PROJ_TOKENS = 512
HGRN_CHUNK = 64
HGRN_MAX_SEQS = 4
HGRN_SAMPLE_SEQS = 8


def _sigmoid(x):
    return 1.0 / (1.0 + jnp.exp(-x))


def _silu(x):
    return x * _sigmoid(x)


def _dot(a, b):
    return jnp.dot(a, b, preferred_element_type=F32)


def _dot_nt(a, b):
    return lax.dot_general(a, b, (((1,), (1,)), ((), ())), preferred_element_type=F32)


def _dot_tn(a, b):
    return lax.dot_general(a, b, (((0,), (0,)), ((), ())), preferred_element_type=F32)


def _alibi_slopes(n_heads_per_group):
    a_heads = N_GROUPS * n_heads_per_group
    s = 2.0 ** (-8.0 * (np.arange(a_heads) + 1) / a_heads)
    return s.astype(np.float32).reshape(N_GROUPS, n_heads_per_group)


def _resident(shape):
    nd = len(shape)
    return pl.BlockSpec(shape, lambda *_: (0,) * nd, pipeline_mode=pl.Buffered(1))


def _ada_kernel(c_ref, w_ref, b_ref, o_ref):
    o_ref[...] = _dot(_silu(c_ref[...]).astype(BF16), w_ref[...].astype(BF16)) + b_ref[...]


def _ada_call(c, w_ada, b_ada):
    n, d = c.shape
    cols = w_ada.shape[1]
    blk = d
    return pl.pallas_call(
        _ada_kernel,
        grid=(cols // blk,),
        in_specs=[pl.BlockSpec((n, d), lambda j: (0, 0)),
                  pl.BlockSpec((d, blk), lambda j: (0, j)),
                  pl.BlockSpec((1, blk), lambda j: (0, j))],
        out_specs=pl.BlockSpec((n, blk), lambda j: (0, j)),
        out_shape=jax.ShapeDtypeStruct((n, cols), F32),
        name="ada",
    )(c, w_ada, b_ada.reshape(1, cols))


def _store_dilated(out_ref, val, scr, dil):
    bs, d, rows, width = out_ref.shape
    if dil == 1:
        out_ref[...] = val.reshape(bs, 1, rows, width).astype(out_ref.dtype)
        return
    for c in range(width // LANES):
        scr[c] = val[:, c * LANES:(c + 1) * LANES]
    for r in range(dil):
        for c in range(width // LANES):
            out_ref[0, r, :, c * LANES:(c + 1) * LANES] = scr[c, pl.ds(r, rows, stride=dil), :].astype(out_ref.dtype)


def _proj_kernel(x_ref, shift_ref, scale_ref, ng_ref, w_ref, qg_ref, kg_ref, lbl_ref, *refs,
                 n_groups, dils, a_out, d_model, merged):
    n_qkv = 3 if merged else 3 * n_groups
    qkv_refs = refs[:n_qkv]
    sza_ref, sqb_ref, g_ref, kf_ref, ib_ref, szb_ref, sga_ref, sgb_ref = refs[n_qkv:n_qkv + 8]
    win_refs, scr = refs[n_qkv + 8:-1], refs[-1]
    x = x_ref[...]
    ms = jnp.mean(x * x, axis=-1, keepdims=True)
    h = x * lax.rsqrt(ms + EPS) * ng_ref[...]
    h = h * (1.0 + scale_ref[...]) + shift_ref[...]
    tm = h.shape[0] * h.shape[1]
    hb = h.reshape(tm, d_model).astype(BF16)
    tile = a_out

    def proj(c0):
        return _dot(hb, w_ref[:, c0:c0 + tile])

    def head_norm(u, gain):
        u2 = u * u
        low = lax.broadcasted_iota(jnp.int32, (tm, LANES), 1) < HEAD_DIM
        outs = []
        for c in range(tile // LANES):
            s = u2[:, c * LANES:(c + 1) * LANES]
            lo = jnp.sum(jnp.where(low, s, 0.0), axis=-1, keepdims=True)
            hi = jnp.sum(jnp.where(low, 0.0, s), axis=-1, keepdims=True)
            r_lo = lax.rsqrt(lo * (1.0 / HEAD_DIM) + EPS)
            r_hi = lax.rsqrt(hi * (1.0 / HEAD_DIM) + EPS)
            outs.append(u[:, c * LANES:(c + 1) * LANES] * jnp.where(low, r_lo, r_hi))
        return jnp.concatenate(outs, axis=1) * gain

    c0 = 0
    for kind in range(3):
        for g in range(n_groups):
            u = proj(c0)
            if kind == 0:
                u = head_norm(u, qg_ref[...])
            elif kind == 1:
                u = head_norm(u, kg_ref[...])
            if merged:
                ref = qkv_refs[kind]
                ref[:, :, g * tile:(g + 1) * tile] = u.reshape(ref.shape[0], ref.shape[1], tile).astype(ref.dtype)
            else:
                _store_dilated(qkv_refs[kind * n_groups + g], u, scr, dils[g])
            if win_refs and kind > 0:
                cols = win_refs[g].shape[3]
                win_refs[g][0, kind - 1] = u[tm - cols:, :].T
            c0 += tile
    sza_ref[...] = _silu(proj(c0)).astype(BF16)
    c0 += tile
    sqb_ref[...] = _silu(proj(c0)).astype(BF16)
    c0 += tile
    lbl = lbl_ref[...]
    lbe = jnp.exp(lbl - jnp.max(lbl, axis=0, keepdims=True))
    lb = lbe[0:1] / jnp.sum(lbe, axis=0, keepdims=True)
    fr = proj(c0)
    g_ref[...] = jnp.log2(lb + (1.0 - lb) * _sigmoid(fr))
    kf_ref[...] = ((1.0 - lb) * _sigmoid(-fr)).astype(BF16)
    c0 += tile
    ib_ref[...] = proj(c0).astype(BF16)
    c0 += tile
    szb_ref[...] = _silu(proj(c0)).astype(BF16)
    c0 += tile
    for j in range(d_model // tile):
        sga_ref[:, j * tile:(j + 1) * tile] = _sigmoid(proj(c0)).astype(BF16)
        c0 += tile
    for j in range(d_model // tile):
        sgb_ref[:, j * tile:(j + 1) * tile] = _sigmoid(proj(c0)).astype(BF16)
        c0 += tile


def _token_tiling(s, t):
    if t >= PROJ_TOKENS:
        assert t % PROJ_TOKENS == 0
        return 1, PROJ_TOKENS
    bs = min(s, PROJ_TOKENS // t)
    assert s % bs == 0 and t % SUBLANES == 0
    return bs, t


def _proj_call(x, shift, scale, norm_gain, w_in_bf, qg, kg, lb_logits, *, n_groups, dils, a_out, b_f, kv_dtype,
               windows=()):
    s, t, d = x.shape
    bs, bt = _token_tiling(s, t)
    tm = bs * bt
    n_t = t // bt
    ntok = s * t
    grid = (s // bs, n_t)
    assert all(dl == 1 or (bs == 1 and bt % (dl * BF16_ROWS) == 0) for dl in dils)

    def tok_spec(cols):
        return pl.BlockSpec((tm, cols), lambda i, j: (i * n_t + j, 0))

    merged = all(dl == 1 for dl in dils)
    if merged:
        qkv_shapes = [jax.ShapeDtypeStruct((s, t, n_groups * a_out), dt) for dt in (BF16, kv_dtype, kv_dtype)]
        qkv_specs = [pl.BlockSpec((bs, bt, n_groups * a_out), lambda i, j: (i, j, 0))] * 3
    else:
        qkv_dtypes = [BF16] * n_groups + [kv_dtype] * (2 * n_groups)
        qkv_shapes = [jax.ShapeDtypeStruct((s, dils[g], t // dils[g], a_out), dt)
                      for g, dt in zip(list(range(n_groups)) * 3, qkv_dtypes)]
        qkv_specs = [pl.BlockSpec((bs, dils[g], bt // dils[g], a_out), lambda i, j: (i, 0, j, 0))
                     for g in list(range(n_groups)) * 3]
    out_cols = [(a_out, BF16), (b_f, BF16), (b_f, F32), (b_f, BF16), (b_f, BF16), (b_f, BF16), (d, BF16), (d, BF16)]
    win_shapes, win_specs = [], []
    for w in windows:
        cols = min(w, bt)
        assert bs == 1 and w % cols == 0 and t % cols == 0
        first = (t - w) // cols
        per_step = bt // cols
        win_shapes.append(jax.ShapeDtypeStruct((s, 2, a_out, w), F32))
        win_specs.append(pl.BlockSpec(
            (1, 2, a_out, cols),
            lambda i, j, first=first, per_step=per_step: (i, 0, 0, jnp.maximum((j + 1) * per_step - 1 - first, 0))))
    kern = functools.partial(_proj_kernel, n_groups=n_groups, dils=tuple(dils), a_out=a_out, d_model=d, merged=merged)
    return pl.pallas_call(
        kern,
        grid=grid,
        in_specs=[pl.BlockSpec((bs, bt, d), lambda i, j: (i, j, 0)),
                  pl.BlockSpec((bs, 1, d), lambda i, j: (i, 0, 0)),
                  pl.BlockSpec((bs, 1, d), lambda i, j: (i, 0, 0)),
                  _resident((1, d)),
                  _resident(w_in_bf.shape),
                  _resident(qg.shape),
                  _resident(kg.shape),
                  _resident(lb_logits.shape)],
        out_specs=qkv_specs + [tok_spec(c) for c, _ in out_cols] + win_specs,
        out_shape=qkv_shapes + [jax.ShapeDtypeStruct((ntok, c), dt) for c, dt in out_cols] + win_shapes,
        scratch_shapes=[pltpu.VMEM((a_out // LANES, tm, LANES), F32)],
        compiler_params=pltpu.CompilerParams(dimension_semantics=("arbitrary", "arbitrary"),
                                             vmem_limit_bytes=V7X_VMEM_LIMIT),
        name="proj",
    )(x, shift, scale, norm_gain.reshape(1, d), w_in_bf, qg, kg, lb_logits)


def _prompt_attn(q_ref, kp_ref, kc_ref, vp_ref, vc_ref, bias_ref, o_ref, lse_ref, *, n_heads, n_blk):
    first = (pl.program_id(2) == 0).astype(jnp.int32)
    lane = lax.broadcasted_iota(jnp.int32, (ATTN_BLOCK, 2 * HEAD_DIM), 1)
    low = lane < HEAD_DIM
    for blk in range(n_blk):
        rows = slice(blk * ATTN_BLOCK, (blk + 1) * ATTN_BLOCK)
        before = slice((blk - 1) * ATTN_BLOCK, blk * ATTN_BLOCK)
        table = first if blk == 0 else 0
        scores, values = [], []
        for hp in range(n_heads // 2):
            sl = slice(hp * 2 * HEAD_DIM, (hp + 1) * 2 * HEAD_DIM)
            q2 = q_ref[rows, sl]
            k_prev = kp_ref[:, sl] if blk == 0 else kc_ref[before, sl]
            v_prev = vp_ref[:, sl] if blk == 0 else vc_ref[before, sl]
            k2 = jnp.concatenate([k_prev, kc_ref[rows, sl]], axis=0)
            values.append(jnp.concatenate([v_prev, vc_ref[rows, sl]], axis=0))
            for par in range(2):
                keep = low if par == 0 else jnp.logical_not(low)
                qm = jnp.where(keep, q2, jnp.zeros_like(q2))
                scores.append(_dot_nt(qm, k2) + bias_ref[table, 2 * hp + par])
        soft = []
        for s in scores:
            m = jnp.max(s, axis=-1, keepdims=True)
            p = jnp.exp2(s - m)
            soft.append((p.astype(BF16), m, jnp.sum(p, axis=-1, keepdims=True)))
        outs = [_dot(p, values[i // 2]) for i, (p, m, l) in enumerate(soft)]
        lse = jnp.zeros((ATTN_BLOCK, LANES), F32)
        for hp in range(n_heads // 2):
            sl = slice(hp * 2 * HEAD_DIM, (hp + 1) * 2 * HEAD_DIM)
            (_, m0, l0), (_, m1, l1) = soft[2 * hp], soft[2 * hp + 1]
            o_ref[rows, sl] = jnp.where(low, outs[2 * hp] / l0, outs[2 * hp + 1] / l1).astype(o_ref.dtype)
            lse = jnp.where(lane == 2 * hp, m0 + jnp.log2(l0), lse)
            lse = jnp.where(lane == 2 * hp + 1, m1 + jnp.log2(l1), lse)
        lse_ref[rows, :] = lse


def _sample_attn(q_ref, k_ref, v_ref, caches, biases, bn_ref, o_ref, seq, *, n_heads):
    width = n_heads * HEAD_DIM
    per_head = []
    for h in range(n_heads):
        scores, values = [], []
        for g in range(N_GROUPS):
            hs = slice(g * width + h * HEAD_DIM, g * width + (h + 1) * HEAD_DIM)
            qh = q_ref[seq, :, hs]
            scores.append(_dot(qh, caches[g][seq, 0, h].astype(BF16)) + biases[g][h])
            values.append(caches[g][seq, 1, h].astype(BF16))
            scores.append(_dot_nt(qh, k_ref[seq, :, hs].astype(BF16)) + bn_ref[g, h])
            values.append(v_ref[seq, :, hs].astype(BF16))
        per_head.append((scores, values))
    soft = []
    for scores, values in per_head:
        m = functools.reduce(jnp.maximum, [jnp.max(s, axis=-1, keepdims=True) for s in scores])
        ps = [jnp.exp2(s - m) for s in scores]
        l = functools.reduce(lambda a, b: a + b, [jnp.sum(p, axis=-1, keepdims=True) for p in ps])
        soft.append(([p.astype(BF16) for p in ps], l))
    for h, ((ps, l), (_, values)) in enumerate(zip(soft, per_head)):
        parts = [(_dot_nt(p, v) if i % 2 == 0 else _dot(p, v)) for i, (p, v) in enumerate(zip(ps, values))]
        acc = functools.reduce(lambda a, b: a + b, parts)
        o_ref[seq, :, h * HEAD_DIM:(h + 1) * HEAD_DIM] = (acc / l).astype(o_ref.dtype)


def _attn_kernel(*refs, n_heads, n_blk, n_seq, hgrn, first_block, grid):
    prompt_in, (qn_ref, kn_ref, vn_ref), refs = refs[:6], refs[6:9], refs[9:]
    caches_hbm, biases = refs[:N_GROUPS], refs[N_GROUPS:2 * N_GROUPS]
    bn_ref, refs = refs[2 * N_GROUPS], refs[2 * N_GROUPS + 1:]
    rings, sems, refs = refs[-N_GROUPS - 1:-1], refs[-1], refs[:-N_GROUPS - 1]
    if hgrn is not None:
        rec_in, refs = refs[:8], refs[8:]
        o_ref, lse_ref, os_ref, ob_ref, st_ref = refs
    else:
        o_ref, lse_ref, os_ref = refs

    total = grid[0] * grid[1] * grid[2]
    st = (pl.program_id(0) * grid[1] + pl.program_id(1)) * grid[2] + pl.program_id(2)

    def fetch(g, step, slot):
        src = caches_hbm[g].at[pl.ds((first_block + step) * n_seq, n_seq)]
        return pltpu.make_async_copy(src, rings[g].at[slot], sems.at[g, slot])

    @pl.when(st == 0)
    def _():
        for ahead in range(min(CACHE_SLOTS - 1, total)):
            for g in range(N_GROUPS):
                fetch(g, ahead, ahead).start()

    @pl.when(st + (CACHE_SLOTS - 1) < total)
    def _():
        for g in range(N_GROUPS):
            fetch(g, st + (CACHE_SLOTS - 1), lax.rem(st + (CACHE_SLOTS - 1), CACHE_SLOTS)).start()

    slot = lax.rem(st, CACHE_SLOTS)
    for g in range(N_GROUPS):
        fetch(g, st, slot).wait()
    caches = [r.at[slot] for r in rings]
    for seq in range(n_seq):
        _sample_attn(qn_ref, kn_ref, vn_ref, caches, biases, bn_ref, os_ref, seq, n_heads=n_heads)
    if hgrn is not None:
        first = (pl.program_id(0) == 0) & (pl.program_id(1) == 0) & (pl.program_id(2) == 0)
        _hgrn_prompt_part(*rec_in, ob_ref, st_ref, first, c=hgrn[0], n_heads=hgrn[1])
    _prompt_attn(*prompt_in, o_ref, lse_ref, n_heads=n_heads, n_blk=n_blk)


def _attn_tables(dil, slopes_g):
    i = np.arange(ATTN_BLOCK)[:, None]
    j = np.arange(2 * ATTN_BLOCK)[None, :]
    delta = i + ATTN_BLOCK - j
    valid = (delta >= 0) & (delta <= N_BACK)
    dist = (delta * dil).astype(np.float32)
    bias = np.where(valid[None], -slopes_g[:, None, None] * dist[None] * LOG2E, NEG).astype(np.float32)
    first = np.where((j >= ATTN_BLOCK)[None], bias, NEG).astype(np.float32)
    return jnp.asarray(np.stack([bias, first]))


def _sattn_tables(n_heads, t_new, slopes, wbs):
    t = np.arange(t_new)[:, None]
    biases, bn = [], []
    for g in range(N_GROUPS):
        d, wb = DILATIONS[g], wbs[g]
        dist = wb + t - np.arange(wb)[None, :]
        valid = (dist % d == 0) & (dist // d <= N_BACK)
        biases.append(np.where(valid[None], -slopes[g][:, None, None] * dist[None] * LOG2E, NEG).astype(np.float32))
        dist = t - np.arange(t_new)[None, :]
        valid = (dist >= 0) & (dist % d == 0) & (dist // d <= N_BACK)
        bn.append(np.where(valid[None], -slopes[g][:, None, None] * dist[None] * LOG2E, NEG).astype(np.float32))
    return [jnp.asarray(b) for b in biases], jnp.asarray(np.stack(bn))


def _attn_steps(q):
    n, dil, sub, _ = q.shape
    nb = sub // ATTN_BLOCK
    n_blk = min(ATTN_BLOCKS_PER_STEP, nb)
    assert sub % ATTN_BLOCK == 0 and nb % n_blk == 0
    return n_blk, nb // n_blk, n * dil * (nb // n_blk)


def _attn_call(q, k, v, g, n_heads, slopes, sample, first_block, n_seq, rec=None):
    n, dil, sub, width = q.shape
    assert dil == DILATIONS[g] and width == n_heads * HEAD_DIM and n_heads <= LANES
    n_blk, steps_i, steps = _attn_steps(q)
    cur = pl.BlockSpec((None, None, n_blk * ATTN_BLOCK, width), lambda b, r, i: (b, r, i, 0))
    prev = pl.BlockSpec((None, None, ATTN_BLOCK, width), lambda b, r, i: (b, r, jnp.maximum(i * n_blk - 1, 0), 0))
    lse = pl.BlockSpec((None, None, n_blk * ATTN_BLOCK, LANES), lambda b, r, i: (b, r, i, 0))
    bias = _attn_tables(dil, slopes[g])

    qn, kn, vn, caches = sample
    t_new = qn.shape[1]
    wbs = [c.shape[1] for c in caches]
    sbiases, bn = _sattn_tables(n_heads, t_new, slopes, wbs)
    views = [jnp.transpose(c, (0, 2, 3, 4, 1)) for c in caches]

    def step(b, r, i):
        return (b * dil + r) * steps_i + i

    new = pl.BlockSpec((n_seq, t_new, N_GROUPS * width), lambda b, r, i: (first_block + step(b, r, i), 0, 0))
    cache_specs = [pl.BlockSpec(memory_space=pl.ANY)] * N_GROUPS
    scratch = [pltpu.VMEM((CACHE_SLOTS, n_seq, 2, n_heads, HEAD_DIM, wb), F32) for wb in wbs]
    scratch.append(pltpu.SemaphoreType.DMA((N_GROUPS, CACHE_SLOTS)))
    in_specs = ([cur, prev, cur, prev, cur, _resident(bias.shape)] + [new] * 3 + cache_specs
                + [_resident(b.shape) for b in sbiases] + [_resident(bn.shape)])
    out_specs = [cur, lse, pl.BlockSpec((n_seq, t_new, width), lambda b, r, i: (step(b, r, i), 0, 0))]
    out_shape = [jax.ShapeDtypeStruct(q.shape, BF16), jax.ShapeDtypeStruct((n, dil, sub, LANES), F32),
                 jax.ShapeDtypeStruct((steps * n_seq, t_new, width), BF16)]
    operands = [q, k, k, v, v, bias, qn, kn, vn, *views, *sbiases, bn]
    hgrn = None
    if rec is not None:
        sq, g_log, kf, iv, og, rec_heads = rec
        n_r, t_r, width_r = sq.shape
        c = min(HGRN_CHUNK, t_r)
        assert t_r % steps == 0 and (t_r // steps) % c == 0 and n_r <= HGRN_MAX_SEQS
        tri, masks, signs = _hgrn_tables(c)
        tok = pl.BlockSpec((n_r, t_r // steps, width_r), lambda b, r, i: (0, step(b, r, i), 0))
        in_specs += [tok] * 4 + [_resident(tri.shape), _resident(masks.shape), _resident(signs.shape),
                                 _resident((1, HGRN_HEAD))]
        out_specs += [tok, pl.BlockSpec((n_r, rec_heads, HGRN_HEAD, HGRN_HEAD), lambda b, r, i: (0, 0, 0, 0))]
        out_shape += [jax.ShapeDtypeStruct((n_r, t_r, width_r), BF16),
                      jax.ShapeDtypeStruct((n_r, rec_heads, HGRN_HEAD, HGRN_HEAD), F32)]
        operands += [sq, g_log, kf, iv, tri, masks, signs, og.reshape(1, HGRN_HEAD)]
        hgrn = (c, rec_heads)
    return pl.pallas_call(
        functools.partial(_attn_kernel, n_heads=n_heads, n_blk=n_blk, n_seq=n_seq, hgrn=hgrn, first_block=first_block,
                          grid=(n, dil, steps_i)),
        grid=(n, dil, steps_i),
        in_specs=in_specs,
        out_specs=out_specs,
        out_shape=out_shape,
        scratch_shapes=scratch,
        compiler_params=pltpu.CompilerParams(dimension_semantics=("arbitrary", "arbitrary", "arbitrary"),
                                             vmem_limit_bytes=V7X_VMEM_LIMIT),
        name=f"attn_g{g}",
    )(*operands)


def _hgrn_tables(c):
    t = np.arange(c)
    tri = (t[None, :] <= t[:, None]).astype(np.float32)
    masks = [np.eye(c, dtype=bool)]
    signs = []
    w = c // 2
    while w >= 1:
        blk = t // (2 * w)
        late = (t // w) % 2 == 1
        masks.append((blk[:, None] == blk[None, :]) & late[:, None] & ~late[None, :])
        if w < SUBLANES:
            signs.append(np.where(late, 1.0, -1.0))
        w //= 2
    masks = np.stack(masks).astype(np.float32)
    signs = np.repeat(np.stack(signs).reshape(-1, 1), HGRN_HEAD, axis=1).astype(np.float32)
    return jnp.asarray(tri, BF16), jnp.asarray(masks), jnp.asarray(signs)


def _level_exponents(b, g, c):
    width = b.shape[1]
    coarse, fine = [], []
    w = c // 2
    while w >= SUBLANES:
        parts = []
        for k in range(c // (2 * w)):
            lo, mid, hi = k * 2 * w, k * 2 * w + w, (k + 1) * 2 * w
            ref = b[mid - 1:mid]
            parts.append((slice(lo, mid), False, ref - b[lo:mid]))
            parts.append((slice(mid, hi), True, b[mid:hi] - ref))
        coarse.append(parts)
        w //= 2
    assert SUBLANES == 8, "the three finest levels below are written for 8-row registers"
    b3 = b.reshape(c // SUBLANES, SUBLANES, width)
    sub = lax.broadcasted_iota(jnp.int32, b3.shape, 1)
    if c >= SUBLANES:
        fine.append((b3 - jnp.broadcast_to(b3[:, 3:4], b3.shape)).reshape(c, width))
    ref = jnp.where(sub < 4, jnp.broadcast_to(b3[:, 1:2], b3.shape), jnp.broadcast_to(b3[:, 5:6], b3.shape))
    fine.append((b3 - ref).reshape(c, width))
    row = lax.broadcasted_iota(jnp.int32, b.shape, 0)
    fine.append(jnp.where(row % 2 == 1, g, 0.0))
    return coarse, fine


def _hgrn_chunks(seqs, tri, masks, signs, og, c):
    n_lev = masks.shape[0] - 1
    tri_v = tri[...]
    og = og[...]
    pre = []
    for sq, g, kf, iv, state in seqs:
        g1 = g.astype(BF16)
        r1 = g - g1.astype(F32)
        g2 = r1.astype(BF16)
        g3 = (r1 - g2.astype(F32)).astype(BF16)
        b = _dot(tri_v, g1) + _dot(tri_v, g2) + _dot(tri_v, g3)
        pre.append((b, _level_exponents(b, g, c)))
    heads = []
    for (sq, g, kf, iv, state), (b, (coarse, fine)) in zip(seqs, pre):
        e_cum = jnp.exp2(b)
        e_tail = jnp.exp2(b[c - 1:c] - b)
        for h in range(len(state)):
            hs = slice(h * HGRN_HEAD, (h + 1) * HGRN_HEAD)
            sqh, kfh, ivh = sq[:, hs], kf[:, hs], iv[:, hs]
            sqf, kff = sqh.astype(F32), kfh.astype(F32)
            eb = e_cum[:, hs]
            prods = [_dot_nt(sqh, kfh)]
            for parts in coarse:
                x = jnp.concatenate([(sqf if late else kff)[rows] * jnp.exp2(e[:, hs]) for rows, late, e in parts],
                                    axis=0).astype(BF16)
                prods.append(_dot_nt(x, x))
            for lv, e in enumerate(fine):
                sgn = signs[lv * c:(lv + 1) * c]
                x = (jnp.where(sgn > 0.0, sqf, kff) * jnp.exp2(e[:, hs] * sgn)).astype(BF16)
                prods.append(_dot_nt(x, x))
            o_inter = _dot((sqf * eb).astype(BF16), state[h].astype(BF16))
            kt = (kff * e_tail[:, hs]).astype(BF16)
            heads.append((prods, o_inter, _dot_tn(kt, ivh), eb, ivh, state[h]))
    results = []
    for prods, o_inter, upd, eb, ivh, st in heads:
        att = masks[0] * prods[0]
        for lv in range(n_lev):
            att = att + masks[1 + lv] * prods[1 + lv]
        o = o_inter + _dot(att.astype(BF16), ivh)
        decay = jnp.broadcast_to(eb[c - 1:c, :], (HGRN_HEAD, HGRN_HEAD)).T
        results.append((o * lax.rsqrt(jnp.mean(o * o, axis=-1, keepdims=True) + EPS) * og, decay * st + upd))
    n_h = len(seqs[0][4])
    return [([r[0] for r in results[i * n_h:(i + 1) * n_h]], [r[1] for r in results[i * n_h:(i + 1) * n_h]])
            for i in range(len(seqs))]


def _hgrn_prompt_part(sq_ref, g_ref, kf_ref, iv_ref, tri_ref, masks_ref, signs_ref, og_ref, o_ref, s_ref, first,
                      *, c, n_heads):
    @pl.when(first)
    def _():
        s_ref[...] = jnp.zeros_like(s_ref)

    def body(ci, carry):
        rows = pl.ds(pl.multiple_of(ci * c, c), c)
        seqs = [(sq_ref[b, rows, :], g_ref[b, rows, :], kf_ref[b, rows, :], iv_ref[b, rows, :],
                 [s_ref[b, h] for h in range(n_heads)]) for b in range(sq_ref.shape[0])]
        res = _hgrn_chunks(seqs, tri_ref, masks_ref, signs_ref, og_ref, c)
        for b, (outs, new_state) in enumerate(res):
            for h in range(n_heads):
                s_ref[b, h] = new_state[h]
                o_ref[b, rows, h * HGRN_HEAD:(h + 1) * HGRN_HEAD] = outs[h].astype(o_ref.dtype)
        return carry

    lax.fori_loop(0, sq_ref.shape[1] // c, body, 0)


def _hgrn_sample_kernel(sq_ref, g_ref, kf_ref, iv_ref, s0_ref, tri_ref, masks_ref, signs_ref, og_ref, o_ref, s_ref,
                        *, c, n_heads):
    seqs = [(sq_ref[b], g_ref[b], kf_ref[b], iv_ref[b], [s0_ref[b, h] for h in range(n_heads)])
            for b in range(sq_ref.shape[0])]
    res = _hgrn_chunks(seqs, tri_ref, masks_ref, signs_ref, og_ref, c)
    for b, (outs, new_state) in enumerate(res):
        for h in range(n_heads):
            s_ref[b, h] = new_state[h]
            o_ref[b, :, h * HGRN_HEAD:(h + 1) * HGRN_HEAD] = outs[h].astype(o_ref.dtype)


def _hgrn_sample_call(sq, g, kf, iv, s0, og, n, t, n_heads):
    width = n_heads * HGRN_HEAD
    bn = min(HGRN_SAMPLE_SEQS, n)
    assert n % bn == 0 and t % SUBLANES == 0
    tri, masks, signs = _hgrn_tables(t)
    tok = pl.BlockSpec((bn, t, width), lambda i: (i, 0, 0))
    st = pl.BlockSpec((bn, n_heads, HGRN_HEAD, HGRN_HEAD), lambda i: (i, 0, 0, 0))
    o, s = pl.pallas_call(
        functools.partial(_hgrn_sample_kernel, c=t, n_heads=n_heads),
        grid=(n // bn,),
        in_specs=[tok, tok, tok, tok, st, _resident(tri.shape), _resident(masks.shape), _resident(signs.shape),
                  _resident((1, HGRN_HEAD))],
        out_specs=[tok, st],
        out_shape=[jax.ShapeDtypeStruct((n, t, width), BF16),
                   jax.ShapeDtypeStruct((n, n_heads, HGRN_HEAD, HGRN_HEAD), F32)],
        compiler_params=pltpu.CompilerParams(dimension_semantics=("arbitrary",), vmem_limit_bytes=V7X_VMEM_LIMIT),
        name="hgrn_sample",
    )(sq.reshape(n, t, width), g.reshape(n, t, width), kf.reshape(n, t, width), iv.reshape(n, t, width),
      s0, tri, masks, signs, og.reshape(1, HGRN_HEAD))
    return o.reshape(n * t, width), s


def _load_tokens(ref, scr):
    bs, dil, rows, width = ref.shape
    if dil == 1:
        return ref[...].reshape(bs * rows, width).astype(F32)
    for r in range(dil):
        for c in range(width // LANES):
            scr[c, pl.ds(r, rows, stride=dil), :] = ref[0, r, :, c * LANES:(c + 1) * LANES].astype(F32)
    return jnp.concatenate([scr[c] for c in range(width // LANES)], axis=1)


def _out_kernel(*refs, n_att):
    att = refs[:2 * n_att] if n_att > 1 else refs[:1]
    n_in = len(att)
    sza_ref, ob_ref, szb_ref, sga_ref, sgb_ref, x_ref, gate_ref, ex_ref, wa_ref, wb_ref, wo_ref, y_ref, scr = refs[n_in:]
    if n_att > 1:
        ls_ = [_load_tokens(r, scr) for r in att[n_att:]]
        m = functools.reduce(jnp.maximum, ls_)
        es = [jnp.exp2(l - m) for l in ls_]
        inv = 1.0 / functools.reduce(lambda a, b: a + b, es)
        alphas = [_dot((e * inv).astype(BF16), ex_ref[...]) for e in es]
        o_a = functools.reduce(lambda a, b: a + b, [al * _load_tokens(r, scr) for al, r in zip(alphas, att[:n_att])])
    else:
        o_a = _load_tokens(att[0], scr)
    branch_a = _dot((o_a * sza_ref[...].astype(F32)).astype(BF16), wa_ref[...])
    branch_b = _dot((ob_ref[...].astype(F32) * szb_ref[...].astype(F32)).astype(BF16), wb_ref[...])
    merged = sga_ref[...].astype(F32) * branch_a + sgb_ref[...].astype(F32) * branch_b
    upd = _dot(merged.astype(BF16), wo_ref[...])
    x = x_ref[...]
    y_ref[...] = x + gate_ref[...] * upd.reshape(x.shape)


def _out_call(att_inputs, sza, ob, szb, sga, sgb, x, gate, wa_bf, wb_bf, wo_bf):
    s, t, d = x.shape
    bs, bt = _token_tiling(s, t)
    tm = bs * bt
    n_t = t // bt

    def tok_spec(a):
        return pl.BlockSpec((tm, a.shape[1]), lambda i, j: (i * n_t + j, 0))

    def att_spec(a):
        dil = a.shape[1]
        assert dil == 1 or bs == 1
        return pl.BlockSpec((bs, dil, bt // dil, a.shape[3]), lambda i, j: (i, 0, j, 0))

    n_att = (len(att_inputs) + 1) // 2
    toks = [sza, ob, szb, sga, sgb]
    width = att_inputs[0].shape[3]
    lane_head = np.arange(LANES)[:, None] == (np.arange(width) // HEAD_DIM)[None, :]
    expand = jnp.asarray(lane_head, BF16)
    return pl.pallas_call(
        functools.partial(_out_kernel, n_att=n_att),
        grid=(s // bs, n_t),
        in_specs=[att_spec(a) for a in att_inputs] + [tok_spec(a) for a in toks]
        + [pl.BlockSpec((bs, bt, d), lambda i, j: (i, j, 0)),
           pl.BlockSpec((bs, 1, d), lambda i, j: (i, 0, 0)),
           _resident(expand.shape), _resident(wa_bf.shape), _resident(wb_bf.shape), _resident(wo_bf.shape)],
        out_specs=pl.BlockSpec((bs, bt, d), lambda i, j: (i, j, 0)),
        out_shape=jax.ShapeDtypeStruct((s, t, d), F32),
        scratch_shapes=[pltpu.VMEM((width // LANES, tm, LANES), F32)],
        compiler_params=pltpu.CompilerParams(dimension_semantics=("arbitrary", "arbitrary"),
                                             vmem_limit_bytes=V7X_VMEM_LIMIT),
        name="out",
    )(*att_inputs, *toks, x, gate, expand, wa_bf, wb_bf, wo_bf)


def _kvnew_kernel(k_ref, v_ref, *out_refs):
    width = out_refs[0].shape[2]
    for g, o_ref in enumerate(out_refs):
        for kv, src in enumerate((k_ref, v_ref)):
            for t in range(src.shape[1]):
                o_ref[t, kv] = src[:, t, g * width:(g + 1) * width].T


def _kvnew_call(kn, vn, n_groups):
    n, t, cols = kn.shape
    return pl.pallas_call(
        _kvnew_kernel,
        out_shape=[jax.ShapeDtypeStruct((t, 2, cols // n_groups, n), F32) for _ in range(n_groups)],
        compiler_params=pltpu.CompilerParams(vmem_limit_bytes=V7X_VMEM_LIMIT),
        name="kvnew",
    )(kn, vn)


def kernel(x_prompt, x_sample, cache_kv_g0, cache_kv_g1, cache_kv_g2, state_hgrn, c_prompt, c_sample, norm_gain, w_ada, b_ada, w_in, q_norm_gain, k_norm_gain, hgrn_lb_logits, hgrn_out_norm_gain, w_branch_a, w_branch_b, w_out):
    depth = w_in.shape[0]
    assert depth == 1, "single-layer step"
    n_p, t_p, d = x_prompt.shape
    n_s, t_s, _ = x_sample.shape
    a_out = w_branch_a.shape[1]
    n_heads = a_out // HEAD_DIM
    b_f = w_branch_b.shape[1]
    b_heads = b_f // HGRN_HEAD
    slopes = _alibi_slopes(n_heads)

    n_c = n_p + n_s
    pad = (-n_c) % SUBLANES
    c_all = jnp.concatenate([c_prompt, c_sample, jnp.zeros((pad, d), F32)], axis=0)
    ada = _ada_call(c_all, w_ada[0], b_ada[0])
    shift, scale, gate = ada[:, :d], ada[:, d:2 * d], ada[:, 2 * d:]

    w_in_bf = w_in[0].astype(BF16)
    wa_bf, wb_bf, wo_bf = w_branch_a[0].astype(BF16), w_branch_b[0].astype(BF16), w_out[0].astype(BF16)
    qg = (jnp.tile(q_norm_gain[0], n_heads) * (ATTN_SCALE * LOG2E)).reshape(1, a_out)
    kg = jnp.tile(k_norm_gain[0], n_heads).reshape(1, a_out)
    og = hgrn_out_norm_gain[0]

    def cond(lo, n):
        return [a[lo:lo + n].reshape(n, 1, d) for a in (shift, scale, gate)]

    sh, sc, gt_p = cond(0, n_p)
    outs = _proj_call(x_prompt, sh, sc, norm_gain[0], w_in_bf, qg, kg, hgrn_lb_logits, n_groups=N_GROUPS,
                      dils=DILATIONS, a_out=a_out, b_f=b_f, kv_dtype=BF16,
                      windows=tuple(min(w, t_p) for w in WINDOWS))
    qs, ks, vs = outs[0:N_GROUPS], outs[N_GROUPS:2 * N_GROUPS], outs[2 * N_GROUPS:3 * N_GROUPS]
    act_p, kv_rows = outs[3 * N_GROUPS:3 * N_GROUPS + 8], outs[3 * N_GROUPS + 8:]
    sh, sc, gt_s = cond(n_p, n_s)
    outs = _proj_call(x_sample, sh, sc, norm_gain[0], w_in_bf, qg, kg, hgrn_lb_logits, n_groups=N_GROUPS,
                      dils=(1,) * N_GROUPS, a_out=a_out, b_f=b_f, kv_dtype=F32)
    qn, kn, vn = outs[:3]
    act_s = outs[3:3 + 8]

    steps = [_attn_steps(qs[gi])[2] for gi in range(N_GROUPS)]
    assert n_s % sum(steps) == 0, "sample sequences must spread evenly over the prompt attention grid steps"
    n_seq = n_s // sum(steps)
    sample = (qn, kn, vn, (cache_kv_g0[0], cache_kv_g1[0], cache_kv_g2[0]))
    sza, sqb, g, kf, ib, szb, sga, sgb = act_p
    carrier = int(np.argmax(steps))
    rec = tuple(a.reshape(n_p, t_p, b_f) for a in (sqb, g, kf, ib)) + (og, b_heads)
    att = [_attn_call(qs[gi], ks[gi], vs[gi], gi, n_heads, slopes, sample, sum(steps[:gi]), n_seq,
                      rec if gi == carrier else None) for gi in range(N_GROUPS)]
    o_a = jnp.concatenate([a[2] for a in att], axis=0)
    ob, hgrn_p = att[carrier][3].reshape(n_p * t_p, b_f), att[carrier][4]

    y_p = _out_call([a[0] for a in att] + [a[1] for a in att], sza, ob, szb, sga, sgb, x_prompt, gt_p,
                    wa_bf, wb_bf, wo_bf)
    kv_p = [jnp.transpose(a.reshape(n_p, 2, n_heads, HEAD_DIM, a.shape[3]), (0, 4, 1, 2, 3))[None]
            for a in kv_rows]

    sza, sqb, g, kf, ib, szb, sga, sgb = act_s
    ob, hgrn_s = _hgrn_sample_call(sqb, g, kf, ib, state_hgrn[0], og, n_s, t_s, b_heads)
    y_s = _out_call([o_a.reshape(n_s, 1, t_s, a_out)], sza, ob, szb, sga, sgb, x_sample, gt_s, wa_bf, wb_bf, wo_bf)
    kv_s = [jnp.transpose(a.reshape(t_s, 2, n_heads, HEAD_DIM, n_s), (4, 0, 1, 2, 3))[None]
            for a in _kvnew_call(kn, vn, N_GROUPS)]

    return (y_p, y_s, kv_p[0], kv_p[1], kv_p[2], hgrn_p[None], kv_s[0], kv_s[1], kv_s[2], hgrn_s[None])
```

```python
import functools

import numpy as np
import jax
import jax.numpy as jnp
from jax import lax
from jax.experimental import pallas as pl
from jax.experimental.pallas import tpu as pltpu

F32 = jnp.float32
BF16 = jnp.bfloat16

HEAD_DIM = 64
N_GROUPS = 3
WINDOWS = (128, 512, 2048)
DILATIONS = (1, 4, 16)
N_BACK = 128
ATTN_BLOCK = 128
ATTN_BLOCKS_PER_STEP = 4
CACHE_SLOTS = 3
ATTN_SCALE = HEAD_DIM ** -0.5
HGRN_HEAD = 128
LANES = 128
SUBLANES = 8
BF16_ROWS = 16
LOG2E = 1.4426950408889634
EPS = 1e-6
NEG = -1e30

V7X_VMEM_LIMIT = 60 * 1024 * 1024
PROJ_TOKENS = 512
OUT_TOKENS = 1024
HGRN_CHUNK = 64
HGRN_MAX_SEQS = 4
HGRN_SAMPLE_SEQS = 16


def _sigmoid(x):
    return 1.0 / (1.0 + jnp.exp(-x))


def _silu(x):
    return x * _sigmoid(x)


def _dot(a, b):
    return jnp.dot(a, b, preferred_element_type=F32)


def _dot_nt(a, b):
    return lax.dot_general(a, b, (((1,), (1,)), ((), ())), preferred_element_type=F32)


def _dot_tn(a, b):
    return lax.dot_general(a, b, (((0,), (0,)), ((), ())), preferred_element_type=F32)


def _alibi_slopes(n_heads_per_group):
    a_heads = N_GROUPS * n_heads_per_group
    s = 2.0 ** (-8.0 * (np.arange(a_heads) + 1) / a_heads)
    return s.astype(np.float32).reshape(N_GROUPS, n_heads_per_group)


def _resident(shape):
    nd = len(shape)
    return pl.BlockSpec(shape, lambda *_: (0,) * nd, pipeline_mode=pl.Buffered(1))


def _ada_kernel(c_ref, w_ref, b_ref, o_ref):
    o_ref[...] = _dot(_silu(c_ref[...]).astype(BF16), w_ref[...].astype(BF16)) + b_ref[...]


def _ada_call(c, w_ada, b_ada):
    n, d = c.shape
    cols = w_ada.shape[1]
    blk = d
    return pl.pallas_call(
        _ada_kernel,
        grid=(cols // blk,),
        in_specs=[pl.BlockSpec((n, d), lambda j: (0, 0)),
                  pl.BlockSpec((d, blk), lambda j: (0, j)),
                  pl.BlockSpec((1, blk), lambda j: (0, j))],
        out_specs=pl.BlockSpec((n, blk), lambda j: (0, j)),
        out_shape=jax.ShapeDtypeStruct((n, cols), F32),
        name="ada",
    )(c, w_ada, b_ada.reshape(1, cols))


def _store_dilated(out_ref, val, scr, dil):
    bs, d, rows, width = out_ref.shape
    if dil == 1:
        out_ref[...] = val.reshape(bs, 1, rows, width).astype(out_ref.dtype)
        return
    for c in range(width // LANES):
        scr[c] = val[:, c * LANES:(c + 1) * LANES]
    for r in range(dil):
        for c in range(width // LANES):
            out_ref[0, r, :, c * LANES:(c + 1) * LANES] = scr[c, pl.ds(r, rows, stride=dil), :].astype(out_ref.dtype)


def _proj_kernel(x_ref, shift_ref, scale_ref, ng_ref, w_ref, qg_ref, kg_ref, lbl_ref, *refs,
                 n_groups, dils, a_out, d_model, merged):
    n_qkv = 3 if merged else 3 * n_groups
    qkv_refs = refs[:n_qkv]
    sza_ref, sqb_ref, g_ref, kf_ref, ib_ref, szb_ref, sga_ref, sgb_ref = refs[n_qkv:n_qkv + 8]
    win_refs, scr = refs[n_qkv + 8:-1], refs[-1]
    x = x_ref[...]
    ms = jnp.mean(x * x, axis=-1, keepdims=True)
    h = x * lax.rsqrt(ms + EPS) * ng_ref[...]
    h = h * (1.0 + scale_ref[...]) + shift_ref[...]
    tm = h.shape[0] * h.shape[1]
    hb = h.reshape(tm, d_model).astype(BF16)
    tile = a_out

    def proj(c0):
        return _dot(hb, w_ref[:, c0:c0 + tile])

    def head_norm(u, gain):
        u2 = u * u
        low = lax.broadcasted_iota(jnp.int32, (tm, LANES), 1) < HEAD_DIM
        outs = []
        for c in range(tile // LANES):
            s = u2[:, c * LANES:(c + 1) * LANES]
            lo = jnp.sum(jnp.where(low, s, 0.0), axis=-1, keepdims=True)
            hi = jnp.sum(jnp.where(low, 0.0, s), axis=-1, keepdims=True)
            r_lo = lax.rsqrt(lo * (1.0 / HEAD_DIM) + EPS)
            r_hi = lax.rsqrt(hi * (1.0 / HEAD_DIM) + EPS)
            outs.append(u[:, c * LANES:(c + 1) * LANES] * jnp.where(low, r_lo, r_hi))
        return jnp.concatenate(outs, axis=1) * gain

    c0 = 0
    for kind in range(3):
        for g in range(n_groups):
            u = proj(c0)
            if kind == 0:
                u = head_norm(u, qg_ref[...])
            elif kind == 1:
                u = head_norm(u, kg_ref[...])
            if merged:
                ref = qkv_refs[kind]
                ref[:, :, g * tile:(g + 1) * tile] = u.reshape(ref.shape[0], ref.shape[1], tile).astype(ref.dtype)
            else:
                _store_dilated(qkv_refs[kind * n_groups + g], u, scr, dils[g])
            if win_refs and kind > 0:
                cols = win_refs[g].shape[3]
                win_refs[g][0, kind - 1] = u[tm - cols:, :].T
            c0 += tile
    sza_ref[...] = _silu(proj(c0)).astype(BF16)
    c0 += tile
    sqb_ref[...] = _silu(proj(c0)).astype(BF16)
    c0 += tile
    lbl = lbl_ref[...]
    lbe = jnp.exp(lbl - jnp.max(lbl, axis=0, keepdims=True))
    lb = lbe[0:1] / jnp.sum(lbe, axis=0, keepdims=True)
    fr = proj(c0)
    g_ref[...] = jnp.log2(lb + (1.0 - lb) * _sigmoid(fr))
    kf_ref[...] = ((1.0 - lb) * _sigmoid(-fr)).astype(BF16)
    c0 += tile
    ib_ref[...] = proj(c0).astype(BF16)
    c0 += tile
    szb_ref[...] = _silu(proj(c0)).astype(BF16)
    c0 += tile
    for j in range(d_model // tile):
        sga_ref[:, j * tile:(j + 1) * tile] = _sigmoid(proj(c0)).astype(BF16)
        c0 += tile
    for j in range(d_model // tile):
        sgb_ref[:, j * tile:(j + 1) * tile] = _sigmoid(proj(c0)).astype(BF16)
        c0 += tile


def _token_tiling(s, t, tokens):
    if t >= tokens:
        assert t % tokens == 0
        return 1, tokens
    bs = min(s, tokens // t)
    assert s % bs == 0 and t % SUBLANES == 0
    return bs, t


def _proj_call(x, shift, scale, norm_gain, w_in_bf, qg, kg, lb_logits, *, n_groups, dils, a_out, b_f, kv_dtype,
               windows=()):
    s, t, d = x.shape
    bs, bt = _token_tiling(s, t, PROJ_TOKENS)
    tm = bs * bt
    n_t = t // bt
    ntok = s * t
    grid = (s // bs, n_t)
    assert all(dl == 1 or (bs == 1 and bt % (dl * BF16_ROWS) == 0) for dl in dils)

    def tok_spec(cols):
        return pl.BlockSpec((tm, cols), lambda i, j: (i * n_t + j, 0))

    merged = all(dl == 1 for dl in dils)
    if merged:
        qkv_shapes = [jax.ShapeDtypeStruct((s, t, n_groups * a_out), dt) for dt in (BF16, kv_dtype, kv_dtype)]
        qkv_specs = [pl.BlockSpec((bs, bt, n_groups * a_out), lambda i, j: (i, j, 0))] * 3
    else:
        qkv_dtypes = [BF16] * n_groups + [kv_dtype] * (2 * n_groups)
        qkv_shapes = [jax.ShapeDtypeStruct((s, dils[g], t // dils[g], a_out), dt)
                      for g, dt in zip(list(range(n_groups)) * 3, qkv_dtypes)]
        qkv_specs = [pl.BlockSpec((bs, dils[g], bt // dils[g], a_out), lambda i, j: (i, 0, j, 0))
                     for g in list(range(n_groups)) * 3]
    out_cols = [(a_out, BF16), (b_f, BF16), (b_f, F32), (b_f, BF16), (b_f, BF16), (b_f, BF16), (d, BF16), (d, BF16)]
    win_shapes, win_specs = [], []
    for w in windows:
        cols = min(w, bt)
        assert bs == 1 and w % cols == 0 and t % cols == 0
        first = (t - w) // cols
        per_step = bt // cols
        win_shapes.append(jax.ShapeDtypeStruct((s, 2, a_out, w), F32))
        win_specs.append(pl.BlockSpec(
            (1, 2, a_out, cols),
            lambda i, j, first=first, per_step=per_step: (i, 0, 0, jnp.maximum((j + 1) * per_step - 1 - first, 0))))
    kern = functools.partial(_proj_kernel, n_groups=n_groups, dils=tuple(dils), a_out=a_out, d_model=d, merged=merged)
    return pl.pallas_call(
        kern,
        grid=grid,
        in_specs=[pl.BlockSpec((bs, bt, d), lambda i, j: (i, j, 0)),
                  pl.BlockSpec((bs, 1, d), lambda i, j: (i, 0, 0)),
                  pl.BlockSpec((bs, 1, d), lambda i, j: (i, 0, 0)),
                  _resident((1, d)),
                  _resident(w_in_bf.shape),
                  _resident(qg.shape),
                  _resident(kg.shape),
                  _resident(lb_logits.shape)],
        out_specs=qkv_specs + [tok_spec(c) for c, _ in out_cols] + win_specs,
        out_shape=qkv_shapes + [jax.ShapeDtypeStruct((ntok, c), dt) for c, dt in out_cols] + win_shapes,
        scratch_shapes=[pltpu.VMEM((a_out // LANES, tm, LANES), F32)],
        compiler_params=pltpu.CompilerParams(dimension_semantics=("arbitrary", "arbitrary"),
                                             vmem_limit_bytes=V7X_VMEM_LIMIT),
        name="proj",
    )(x, shift, scale, norm_gain.reshape(1, d), w_in_bf, qg, kg, lb_logits)


def _prompt_attn(q_ref, kp_ref, kc_ref, vp_ref, vc_ref, bias_ref, o_ref, lse_ref, *, n_heads, n_blk):
    first = (pl.program_id(2) == 0).astype(jnp.int32)
    lane = lax.broadcasted_iota(jnp.int32, (ATTN_BLOCK, 2 * HEAD_DIM), 1)
    low = lane < HEAD_DIM
    for blk in range(n_blk):
        rows = slice(blk * ATTN_BLOCK, (blk + 1) * ATTN_BLOCK)
        before = slice((blk - 1) * ATTN_BLOCK, blk * ATTN_BLOCK)
        table = first if blk == 0 else 0
        scores, values = [], []
        for hp in range(n_heads // 2):
            sl = slice(hp * 2 * HEAD_DIM, (hp + 1) * 2 * HEAD_DIM)
            q2 = q_ref[rows, sl]
            k_prev = kp_ref[:, sl] if blk == 0 else kc_ref[before, sl]
            v_prev = vp_ref[:, sl] if blk == 0 else vc_ref[before, sl]
            k2 = jnp.concatenate([k_prev, kc_ref[rows, sl]], axis=0)
            values.append(jnp.concatenate([v_prev, vc_ref[rows, sl]], axis=0))
            for par in range(2):
                keep = low if par == 0 else jnp.logical_not(low)
                qm = jnp.where(keep, q2, jnp.zeros_like(q2))
                scores.append(_dot_nt(qm, k2) + bias_ref[table, 2 * hp + par])
        soft = []
        for s in scores:
            m = jnp.max(s, axis=-1, keepdims=True)
            p = jnp.exp2(s - m)
            soft.append((p.astype(BF16), m, jnp.sum(p, axis=-1, keepdims=True)))
        outs = [_dot(p, values[i // 2]) for i, (p, m, l) in enumerate(soft)]
        lse = jnp.zeros((ATTN_BLOCK, LANES), F32)
        for hp in range(n_heads // 2):
            sl = slice(hp * 2 * HEAD_DIM, (hp + 1) * 2 * HEAD_DIM)
            (_, m0, l0), (_, m1, l1) = soft[2 * hp], soft[2 * hp + 1]
            o_ref[rows, sl] = jnp.where(low, outs[2 * hp] / l0, outs[2 * hp + 1] / l1).astype(o_ref.dtype)
            lse = jnp.where(lane == 2 * hp, m0 + jnp.log2(l0), lse)
            lse = jnp.where(lane == 2 * hp + 1, m1 + jnp.log2(l1), lse)
        lse_ref[rows, :] = lse


def _sample_attn(q_ref, k_ref, v_ref, caches, biases, bn_ref, o_ref, seq, *, n_heads):
    width = n_heads * HEAD_DIM
    per_head = []
    for h in range(n_heads):
        scores, values = [], []
        for g in range(N_GROUPS):
            hs = slice(g * width + h * HEAD_DIM, g * width + (h + 1) * HEAD_DIM)
            qh = q_ref[seq, :, hs]
            scores.append(_dot(qh, caches[g][seq, 0, h].astype(BF16)) + biases[g][h])
            values.append(caches[g][seq, 1, h].astype(BF16))
            scores.append(_dot_nt(qh, k_ref[seq, :, hs].astype(BF16)) + bn_ref[g, h])
            values.append(v_ref[seq, :, hs].astype(BF16))
        per_head.append((scores, values))
    soft = []
    for scores, values in per_head:
        m = functools.reduce(jnp.maximum, [jnp.max(s, axis=-1, keepdims=True) for s in scores])
        ps = [jnp.exp2(s - m) for s in scores]
        l = functools.reduce(lambda a, b: a + b, [jnp.sum(p, axis=-1, keepdims=True) for p in ps])
        soft.append(([p.astype(BF16) for p in ps], l))
    for h, ((ps, l), (_, values)) in enumerate(zip(soft, per_head)):
        parts = [(_dot_nt(p, v) if i % 2 == 0 else _dot(p, v)) for i, (p, v) in enumerate(zip(ps, values))]
        acc = functools.reduce(lambda a, b: a + b, parts)
        o_ref[seq, :, h * HEAD_DIM:(h + 1) * HEAD_DIM] = (acc / l).astype(o_ref.dtype)


def _attn_kernel(*refs, n_heads, n_blk, n_seq, hgrn, first_block, grid):
    prompt_in, (qn_ref, kn_ref, vn_ref), refs = refs[:6], refs[6:9], refs[9:]
    caches_hbm, biases = refs[:N_GROUPS], refs[N_GROUPS:2 * N_GROUPS]
    bn_ref, refs = refs[2 * N_GROUPS], refs[2 * N_GROUPS + 1:]
    rings, sems, refs = refs[-N_GROUPS - 1:-1], refs[-1], refs[:-N_GROUPS - 1]
    if hgrn is not None:
        rec_in, refs = refs[:8], refs[8:]
        o_ref, lse_ref, os_ref, ob_ref, st_ref = refs
    else:
        o_ref, lse_ref, os_ref = refs

    total = grid[0] * grid[1] * grid[2]
    st = (pl.program_id(0) * grid[1] + pl.program_id(1)) * grid[2] + pl.program_id(2)

    def fetch(g, step, slot):
        src = caches_hbm[g].at[pl.ds((first_block + step) * n_seq, n_seq)]
        return pltpu.make_async_copy(src, rings[g].at[slot], sems.at[g, slot])

    @pl.when(st == 0)
    def _():
        for ahead in range(min(CACHE_SLOTS - 1, total)):
            for g in range(N_GROUPS):
                fetch(g, ahead, ahead).start()

    @pl.when(st + (CACHE_SLOTS - 1) < total)
    def _():
        for g in range(N_GROUPS):
            fetch(g, st + (CACHE_SLOTS - 1), lax.rem(st + (CACHE_SLOTS - 1), CACHE_SLOTS)).start()

    slot = lax.rem(st, CACHE_SLOTS)
    for g in range(N_GROUPS):
        fetch(g, st, slot).wait()
    caches = [r.at[slot] for r in rings]
    for seq in range(n_seq):
        _sample_attn(qn_ref, kn_ref, vn_ref, caches, biases, bn_ref, os_ref, seq, n_heads=n_heads)
    if hgrn is not None:
        first = (pl.program_id(0) == 0) & (pl.program_id(1) == 0) & (pl.program_id(2) == 0)
        _hgrn_prompt_part(*rec_in, ob_ref, st_ref, first, c=hgrn[0], n_heads=hgrn[1])
    _prompt_attn(*prompt_in, o_ref, lse_ref, n_heads=n_heads, n_blk=n_blk)


def _attn_tables(dil, slopes_g):
    i = np.arange(ATTN_BLOCK)[:, None]
    j = np.arange(2 * ATTN_BLOCK)[None, :]
    delta = i + ATTN_BLOCK - j
    valid = (delta >= 0) & (delta <= N_BACK)
    dist = (delta * dil).astype(np.float32)
    bias = np.where(valid[None], -slopes_g[:, None, None] * dist[None] * LOG2E, NEG).astype(np.float32)
    first = np.where((j >= ATTN_BLOCK)[None], bias, NEG).astype(np.float32)
    return jnp.asarray(np.stack([bias, first]))


def _sattn_tables(n_heads, t_new, slopes, wbs):
    t = np.arange(t_new)[:, None]
    biases, bn = [], []
    for g in range(N_GROUPS):
        d, wb = DILATIONS[g], wbs[g]
        dist = wb + t - np.arange(wb)[None, :]
        valid = (dist % d == 0) & (dist // d <= N_BACK)
        biases.append(np.where(valid[None], -slopes[g][:, None, None] * dist[None] * LOG2E, NEG).astype(np.float32))
        dist = t - np.arange(t_new)[None, :]
        valid = (dist >= 0) & (dist % d == 0) & (dist // d <= N_BACK)
        bn.append(np.where(valid[None], -slopes[g][:, None, None] * dist[None] * LOG2E, NEG).astype(np.float32))
    return [jnp.asarray(b) for b in biases], jnp.asarray(np.stack(bn))


def _attn_steps(q):
    n, dil, sub, _ = q.shape
    nb = sub // ATTN_BLOCK
    n_blk = min(ATTN_BLOCKS_PER_STEP, nb)
    assert sub % ATTN_BLOCK == 0 and nb % n_blk == 0
    return n_blk, nb // n_blk, n * dil * (nb // n_blk)


def _attn_call(q, k, v, g, n_heads, slopes, sample, first_block, n_seq, rec=None):
    n, dil, sub, width = q.shape
    assert dil == DILATIONS[g] and width == n_heads * HEAD_DIM and n_heads <= LANES
    n_blk, steps_i, steps = _attn_steps(q)
    cur = pl.BlockSpec((None, None, n_blk * ATTN_BLOCK, width), lambda b, r, i: (b, r, i, 0))
    prev = pl.BlockSpec((None, None, ATTN_BLOCK, width), lambda b, r, i: (b, r, jnp.maximum(i * n_blk - 1, 0), 0))
    lse = pl.BlockSpec((None, None, n_blk * ATTN_BLOCK, LANES), lambda b, r, i: (b, r, i, 0))
    bias = _attn_tables(dil, slopes[g])

    qn, kn, vn, caches = sample
    t_new = qn.shape[1]
    wbs = [c.shape[1] for c in caches]
    sbiases, bn = _sattn_tables(n_heads, t_new, slopes, wbs)
    views = [jnp.transpose(c, (0, 2, 3, 4, 1)) for c in caches]

    def step(b, r, i):
        return (b * dil + r) * steps_i + i

    new = pl.BlockSpec((n_seq, t_new, N_GROUPS * width), lambda b, r, i: (first_block + step(b, r, i), 0, 0))
    cache_specs = [pl.BlockSpec(memory_space=pl.ANY)] * N_GROUPS
    scratch = [pltpu.VMEM((CACHE_SLOTS, n_seq, 2, n_heads, HEAD_DIM, wb), F32) for wb in wbs]
    scratch.append(pltpu.SemaphoreType.DMA((N_GROUPS, CACHE_SLOTS)))
    in_specs = ([cur, prev, cur, prev, cur, _resident(bias.shape)] + [new] * 3 + cache_specs
                + [_resident(b.shape) for b in sbiases] + [_resident(bn.shape)])
    out_specs = [cur, lse, pl.BlockSpec((n_seq, t_new, width), lambda b, r, i: (step(b, r, i), 0, 0))]
    out_shape = [jax.ShapeDtypeStruct(q.shape, BF16), jax.ShapeDtypeStruct((n, dil, sub, LANES), F32),
                 jax.ShapeDtypeStruct((steps * n_seq, t_new, width), BF16)]
    operands = [q, k, k, v, v, bias, qn, kn, vn, *views, *sbiases, bn]
    hgrn = None
    if rec is not None:
        sq, g_log, kf, iv, og, rec_heads = rec
        n_r, t_r, width_r = sq.shape
        c = min(HGRN_CHUNK, t_r)
        assert t_r % steps == 0 and (t_r // steps) % c == 0 and n_r <= HGRN_MAX_SEQS
        tri, masks, signs = _hgrn_tables(c)
        tok = pl.BlockSpec((n_r, t_r // steps, width_r), lambda b, r, i: (0, step(b, r, i), 0))
        in_specs += [tok] * 4 + [_resident(tri.shape), _resident(masks.shape), _resident(signs.shape),
                                 _resident((1, HGRN_HEAD))]
        out_specs += [tok, pl.BlockSpec((n_r, rec_heads, HGRN_HEAD, HGRN_HEAD), lambda b, r, i: (0, 0, 0, 0))]
        out_shape += [jax.ShapeDtypeStruct((n_r, t_r, width_r), BF16),
                      jax.ShapeDtypeStruct((n_r, rec_heads, HGRN_HEAD, HGRN_HEAD), F32)]
        operands += [sq, g_log, kf, iv, tri, masks, signs, og.reshape(1, HGRN_HEAD)]
        hgrn = (c, rec_heads)
    return pl.pallas_call(
        functools.partial(_attn_kernel, n_heads=n_heads, n_blk=n_blk, n_seq=n_seq, hgrn=hgrn, first_block=first_block,
                          grid=(n, dil, steps_i)),
        grid=(n, dil, steps_i),
        in_specs=in_specs,
        out_specs=out_specs,
        out_shape=out_shape,
        scratch_shapes=scratch,
        compiler_params=pltpu.CompilerParams(dimension_semantics=("arbitrary", "arbitrary", "arbitrary"),
                                             vmem_limit_bytes=V7X_VMEM_LIMIT),
        name=f"attn_g{g}",
    )(*operands)


def _hgrn_tables(c):
    t = np.arange(c)
    tri = (t[None, :] <= t[:, None]).astype(np.float32)
    masks = [np.eye(c, dtype=bool)]
    signs = []
    w = c // 2
    while w >= 1:
        blk = t // (2 * w)
        late = (t // w) % 2 == 1
        masks.append((blk[:, None] == blk[None, :]) & late[:, None] & ~late[None, :])
        if w < SUBLANES:
            signs.append(np.where(late, 1.0, -1.0))
        w //= 2
    masks = np.stack(masks).astype(np.float32)
    signs = np.repeat(np.stack(signs).reshape(-1, 1), HGRN_HEAD, axis=1).astype(np.float32)
    return jnp.asarray(tri, BF16), jnp.asarray(masks), jnp.asarray(signs)


def _level_exponents(b, g, c):
    width = b.shape[1]
    coarse, fine = [], []
    w = c // 2
    while w >= SUBLANES:
        parts = []
        for k in range(c // (2 * w)):
            lo, mid, hi = k * 2 * w, k * 2 * w + w, (k + 1) * 2 * w
            ref = b[mid - 1:mid]
            parts.append((slice(lo, mid), False, ref - b[lo:mid]))
            parts.append((slice(mid, hi), True, b[mid:hi] - ref))
        coarse.append(parts)
        w //= 2
    assert SUBLANES == 8, "the three finest levels below are written for 8-row registers"
    b3 = b.reshape(c // SUBLANES, SUBLANES, width)
    sub = lax.broadcasted_iota(jnp.int32, b3.shape, 1)
    if c >= SUBLANES:
        fine.append((b3 - jnp.broadcast_to(b3[:, 3:4], b3.shape)).reshape(c, width))
    ref = jnp.where(sub < 4, jnp.broadcast_to(b3[:, 1:2], b3.shape), jnp.broadcast_to(b3[:, 5:6], b3.shape))
    fine.append((b3 - ref).reshape(c, width))
    row = lax.broadcasted_iota(jnp.int32, b.shape, 0)
    fine.append(jnp.where(row % 2 == 1, g, 0.0))
    return coarse, fine


def _hgrn_chunks(seqs, tri, masks, signs, og, c):
    n_lev = masks.shape[0] - 1
    tri_v = tri[...]
    og = og[...]
    pre = []
    for sq, g, kf, iv, state in seqs:
        g1 = g.astype(BF16)
        r1 = g - g1.astype(F32)
        g2 = r1.astype(BF16)
        g3 = (r1 - g2.astype(F32)).astype(BF16)
        b = _dot(tri_v, g1) + _dot(tri_v, g2) + _dot(tri_v, g3)
        pre.append((b, _level_exponents(b, g, c)))
    heads = []
    for (sq, g, kf, iv, state), (b, (coarse, fine)) in zip(seqs, pre):
        e_cum = jnp.exp2(b)
        e_tail = jnp.exp2(b[c - 1:c] - b)
        for h in range(len(state)):
            hs = slice(h * HGRN_HEAD, (h + 1) * HGRN_HEAD)
            sqh, kfh, ivh = sq[:, hs], kf[:, hs], iv[:, hs]
            sqf, kff = sqh.astype(F32), kfh.astype(F32)
            eb = e_cum[:, hs]
            prods = [_dot_nt(sqh, kfh)]
            for parts in coarse:
                x = jnp.concatenate([(sqf if late else kff)[rows] * jnp.exp2(e[:, hs]) for rows, late, e in parts],
                                    axis=0).astype(BF16)
                prods.append(_dot_nt(x, x))
            for lv, e in enumerate(fine):
                sgn = signs[lv * c:(lv + 1) * c]
                x = (jnp.where(sgn > 0.0, sqf, kff) * jnp.exp2(e[:, hs] * sgn)).astype(BF16)
                prods.append(_dot_nt(x, x))
            o_inter = _dot((sqf * eb).astype(BF16), state[h].astype(BF16))
            kt = (kff * e_tail[:, hs]).astype(BF16)
            heads.append((prods, o_inter, _dot_tn(kt, ivh), eb, ivh, state[h]))
    results = []
    for prods, o_inter, upd, eb, ivh, st in heads:
        att = masks[0] * prods[0]
        for lv in range(n_lev):
            att = att + masks[1 + lv] * prods[1 + lv]
        o = o_inter + _dot(att.astype(BF16), ivh)
        decay = jnp.broadcast_to(eb[c - 1:c, :], (HGRN_HEAD, HGRN_HEAD)).T
        results.append((o * lax.rsqrt(jnp.mean(o * o, axis=-1, keepdims=True) + EPS) * og, decay * st + upd))
    n_h = len(seqs[0][4])
    return [([r[0] for r in results[i * n_h:(i + 1) * n_h]], [r[1] for r in results[i * n_h:(i + 1) * n_h]])
            for i in range(len(seqs))]


def _hgrn_prompt_part(sq_ref, g_ref, kf_ref, iv_ref, tri_ref, masks_ref, signs_ref, og_ref, o_ref, s_ref, first,
                      *, c, n_heads):
    @pl.when(first)
    def _():
        s_ref[...] = jnp.zeros_like(s_ref)

    def body(ci, carry):
        rows = pl.ds(pl.multiple_of(ci * c, c), c)
        seqs = [(sq_ref[b, rows, :], g_ref[b, rows, :], kf_ref[b, rows, :], iv_ref[b, rows, :],
                 [s_ref[b, h] for h in range(n_heads)]) for b in range(sq_ref.shape[0])]
        res = _hgrn_chunks(seqs, tri_ref, masks_ref, signs_ref, og_ref, c)
        for b, (outs, new_state) in enumerate(res):
            for h in range(n_heads):
                s_ref[b, h] = new_state[h]
                o_ref[b, rows, h * HGRN_HEAD:(h + 1) * HGRN_HEAD] = outs[h].astype(o_ref.dtype)
        return carry

    lax.fori_loop(0, sq_ref.shape[1] // c, body, 0)


def _hgrn_sample_kernel(sq_ref, g_ref, kf_ref, iv_ref, s0_ref, tri_ref, masks_ref, signs_ref, og_ref, o_ref, s_ref,
                        *, c, n_heads):
    seqs = [(sq_ref[b], g_ref[b], kf_ref[b], iv_ref[b], [s0_ref[b, h] for h in range(n_heads)])
            for b in range(sq_ref.shape[0])]
    res = _hgrn_chunks(seqs, tri_ref, masks_ref, signs_ref, og_ref, c)
    for b, (outs, new_state) in enumerate(res):
        for h in range(n_heads):
            s_ref[b, h] = new_state[h]
            o_ref[b, :, h * HGRN_HEAD:(h + 1) * HGRN_HEAD] = outs[h].astype(o_ref.dtype)


def _hgrn_sample_call(sq, g, kf, iv, s0, og, n, t, n_heads):
    width = n_heads * HGRN_HEAD
    bn = min(HGRN_SAMPLE_SEQS, n)
    assert n % bn == 0 and t % SUBLANES == 0
    tri, masks, signs = _hgrn_tables(t)
    tok = pl.BlockSpec((bn, t, width), lambda i: (i, 0, 0))
    st = pl.BlockSpec((bn, n_heads, HGRN_HEAD, HGRN_HEAD), lambda i: (i, 0, 0, 0))
    o, s = pl.pallas_call(
        functools.partial(_hgrn_sample_kernel, c=t, n_heads=n_heads),
        grid=(n // bn,),
        in_specs=[tok, tok, tok, tok, st, _resident(tri.shape), _resident(masks.shape), _resident(signs.shape),
                  _resident((1, HGRN_HEAD))],
        out_specs=[tok, st],
        out_shape=[jax.ShapeDtypeStruct((n, t, width), BF16),
                   jax.ShapeDtypeStruct((n, n_heads, HGRN_HEAD, HGRN_HEAD), F32)],
        compiler_params=pltpu.CompilerParams(dimension_semantics=("arbitrary",), vmem_limit_bytes=V7X_VMEM_LIMIT),
        name="hgrn_sample",
    )(sq.reshape(n, t, width), g.reshape(n, t, width), kf.reshape(n, t, width), iv.reshape(n, t, width),
      s0, tri, masks, signs, og.reshape(1, HGRN_HEAD))
    return o.reshape(n * t, width), s


def _load_tokens(ref, scr):
    bs, dil, rows, width = ref.shape
    if dil == 1:
        return ref[...].reshape(bs * rows, width).astype(F32)
    for r in range(dil):
        for c in range(width // LANES):
            scr[c, pl.ds(r, rows, stride=dil), :] = ref[0, r, :, c * LANES:(c + 1) * LANES].astype(F32)
    return jnp.concatenate([scr[c] for c in range(width // LANES)], axis=1)


def _out_kernel(*refs, n_att):
    att = refs[:2 * n_att] if n_att > 1 else refs[:1]
    n_in = len(att)
    sza_ref, ob_ref, szb_ref, sga_ref, sgb_ref, x_ref, gate_ref, ex_ref, wa_ref, wb_ref, wo_ref, y_ref, scr = refs[n_in:]
    if n_att > 1:
        ls_ = [_load_tokens(r, scr) for r in att[n_att:]]
        m = functools.reduce(jnp.maximum, ls_)
        es = [jnp.exp2(l - m) for l in ls_]
        inv = 1.0 / functools.reduce(lambda a, b: a + b, es)
        alphas = [_dot((e * inv).astype(BF16), ex_ref[...]) for e in es]
        o_a = functools.reduce(lambda a, b: a + b, [al * _load_tokens(r, scr) for al, r in zip(alphas, att[:n_att])])
    else:
        o_a = _load_tokens(att[0], scr)
    branch_a = _dot((o_a * sza_ref[...].astype(F32)).astype(BF16), wa_ref[...])
    branch_b = _dot((ob_ref[...].astype(F32) * szb_ref[...].astype(F32)).astype(BF16), wb_ref[...])
    merged = sga_ref[...].astype(F32) * branch_a + sgb_ref[...].astype(F32) * branch_b
    upd = _dot(merged.astype(BF16), wo_ref[...])
    x = x_ref[...]
    y_ref[...] = x + gate_ref[...] * upd.reshape(x.shape)


def _out_call(att_inputs, sza, ob, szb, sga, sgb, x, gate, wa_bf, wb_bf, wo_bf):
    s, t, d = x.shape
    bs, bt = _token_tiling(s, t, OUT_TOKENS)
    tm = bs * bt
    n_t = t // bt

    def tok_spec(a):
        return pl.BlockSpec((tm, a.shape[1]), lambda i, j: (i * n_t + j, 0))

    def att_spec(a):
        dil = a.shape[1]
        assert dil == 1 or bs == 1
        return pl.BlockSpec((bs, dil, bt // dil, a.shape[3]), lambda i, j: (i, 0, j, 0))

    n_att = (len(att_inputs) + 1) // 2
    toks = [sza, ob, szb, sga, sgb]
    width = att_inputs[0].shape[3]
    lane_head = np.arange(LANES)[:, None] == (np.arange(width) // HEAD_DIM)[None, :]
    expand = jnp.asarray(lane_head, BF16)
    return pl.pallas_call(
        functools.partial(_out_kernel, n_att=n_att),
        grid=(s // bs, n_t),
        in_specs=[att_spec(a) for a in att_inputs] + [tok_spec(a) for a in toks]
        + [pl.BlockSpec((bs, bt, d), lambda i, j: (i, j, 0)),
           pl.BlockSpec((bs, 1, d), lambda i, j: (i, 0, 0)),
           _resident(expand.shape), _resident(wa_bf.shape), _resident(wb_bf.shape), _resident(wo_bf.shape)],
        out_specs=pl.BlockSpec((bs, bt, d), lambda i, j: (i, j, 0)),
        out_shape=jax.ShapeDtypeStruct((s, t, d), F32),
        scratch_shapes=[pltpu.VMEM((width // LANES, tm, LANES), F32)],
        compiler_params=pltpu.CompilerParams(dimension_semantics=("arbitrary", "arbitrary"),
                                             vmem_limit_bytes=V7X_VMEM_LIMIT),
        name="out",
    )(*att_inputs, *toks, x, gate, expand, wa_bf, wb_bf, wo_bf)


def _kvnew_kernel(k_ref, v_ref, *out_refs):
    width = out_refs[0].shape[2]
    for g, o_ref in enumerate(out_refs):
        for kv, src in enumerate((k_ref, v_ref)):
            for t in range(src.shape[1]):
                o_ref[t, kv] = src[:, t, g * width:(g + 1) * width].T


def _kvnew_call(kn, vn, n_groups):
    n, t, cols = kn.shape
    return pl.pallas_call(
        _kvnew_kernel,
        out_shape=[jax.ShapeDtypeStruct((t, 2, cols // n_groups, n), F32) for _ in range(n_groups)],
        compiler_params=pltpu.CompilerParams(vmem_limit_bytes=V7X_VMEM_LIMIT),
        name="kvnew",
    )(kn, vn)


def kernel(x_prompt, x_sample, cache_kv_g0, cache_kv_g1, cache_kv_g2, state_hgrn, c_prompt, c_sample, norm_gain, w_ada, b_ada, w_in, q_norm_gain, k_norm_gain, hgrn_lb_logits, hgrn_out_norm_gain, w_branch_a, w_branch_b, w_out):
    depth = w_in.shape[0]
    assert depth == 1, "single-layer step"
    n_p, t_p, d = x_prompt.shape
    n_s, t_s, _ = x_sample.shape
    a_out = w_branch_a.shape[1]
    n_heads = a_out // HEAD_DIM
    b_f = w_branch_b.shape[1]
    b_heads = b_f // HGRN_HEAD
    slopes = _alibi_slopes(n_heads)

    n_c = n_p + n_s
    pad = (-n_c) % SUBLANES
    c_all = jnp.concatenate([c_prompt, c_sample, jnp.zeros((pad, d), F32)], axis=0)
    ada = _ada_call(c_all, w_ada[0], b_ada[0])
    shift, scale, gate = ada[:, :d], ada[:, d:2 * d], ada[:, 2 * d:]

    w_in_bf = w_in[0].astype(BF16)
    wa_bf, wb_bf, wo_bf = w_branch_a[0].astype(BF16), w_branch_b[0].astype(BF16), w_out[0].astype(BF16)
    qg = (jnp.tile(q_norm_gain[0], n_heads) * (ATTN_SCALE * LOG2E)).reshape(1, a_out)
    kg = jnp.tile(k_norm_gain[0], n_heads).reshape(1, a_out)
    og = hgrn_out_norm_gain[0]

    def cond(lo, n):
        return [a[lo:lo + n].reshape(n, 1, d) for a in (shift, scale, gate)]

    sh, sc, gt_p = cond(0, n_p)
    outs = _proj_call(x_prompt, sh, sc, norm_gain[0], w_in_bf, qg, kg, hgrn_lb_logits, n_groups=N_GROUPS,
                      dils=DILATIONS, a_out=a_out, b_f=b_f, kv_dtype=BF16,
                      windows=tuple(min(w, t_p) for w in WINDOWS))
    qs, ks, vs = outs[0:N_GROUPS], outs[N_GROUPS:2 * N_GROUPS], outs[2 * N_GROUPS:3 * N_GROUPS]
    act_p, kv_rows = outs[3 * N_GROUPS:3 * N_GROUPS + 8], outs[3 * N_GROUPS + 8:]
    sh, sc, gt_s = cond(n_p, n_s)
    outs = _proj_call(x_sample, sh, sc, norm_gain[0], w_in_bf, qg, kg, hgrn_lb_logits, n_groups=N_GROUPS,
                      dils=(1,) * N_GROUPS, a_out=a_out, b_f=b_f, kv_dtype=F32)
    qn, kn, vn = outs[:3]
    act_s = outs[3:3 + 8]

    steps = [_attn_steps(qs[gi])[2] for gi in range(N_GROUPS)]
    assert n_s % sum(steps) == 0, "sample sequences must spread evenly over the prompt attention grid steps"
    n_seq = n_s // sum(steps)
    sample = (qn, kn, vn, (cache_kv_g0[0], cache_kv_g1[0], cache_kv_g2[0]))
    sza, sqb, g, kf, ib, szb, sga, sgb = act_p
    carrier = int(np.argmax(steps))
    rec = tuple(a.reshape(n_p, t_p, b_f) for a in (sqb, g, kf, ib)) + (og, b_heads)
    att = [_attn_call(qs[gi], ks[gi], vs[gi], gi, n_heads, slopes, sample, sum(steps[:gi]), n_seq,
                      rec if gi == carrier else None) for gi in range(N_GROUPS)]
    o_a = jnp.concatenate([a[2] for a in att], axis=0)
    ob, hgrn_p = att[carrier][3].reshape(n_p * t_p, b_f), att[carrier][4]

    y_p = _out_call([a[0] for a in att] + [a[1] for a in att], sza, ob, szb, sga, sgb, x_prompt, gt_p,
                    wa_bf, wb_bf, wo_bf)
    kv_p = [jnp.transpose(a.reshape(n_p, 2, n_heads, HEAD_DIM, a.shape[3]), (0, 4, 1, 2, 3))[None]
            for a in kv_rows]

    sza, sqb, g, kf, ib, szb, sga, sgb = act_s
    ob, hgrn_s = _hgrn_sample_call(sqb, g, kf, ib, state_hgrn[0], og, n_s, t_s, b_heads)
    y_s = _out_call([o_a.reshape(n_s, 1, t_s, a_out)], sza, ob, szb, sga, sgb, x_sample, gt_s, wa_bf, wb_bf, wo_bf)
    kv_s = [jnp.transpose(a.reshape(t_s, 2, n_heads, HEAD_DIM, n_s), (4, 0, 1, 2, 3))[None]
            for a in _kvnew_call(kn, vn, N_GROUPS)]

    return (y_p, y_s, kv_p[0], kv_p[1], kv_p[2], hgrn_p[None], kv_s[0], kv_s[1], kv_s[2], hgrn_s[None])
```

```python
import functools

import numpy as np
import jax
import jax.numpy as jnp
from jax import lax
from jax.experimental import pallas as pl
from jax.experimental.pallas import tpu as pltpu

F32 = jnp.float32
BF16 = jnp.bfloat16

HEAD_DIM = 64
N_GROUPS = 3
WINDOWS = (128, 512, 2048)
DILATIONS = (1, 4, 16)
N_BACK = 128
ATTN_BLOCK = 128
ATTN_BLOCKS_PER_STEP = 4
CACHE_SLOTS = 3
ATTN_SCALE = HEAD_DIM ** -0.5
HGRN_HEAD = 128
LANES = 128
SUBLANES = 8
BF16_ROWS = 16
LOG2E = 1.4426950408889634
EPS = 1e-6
NEG = -1e30

V7X_VMEM_LIMIT = 60 * 1024 * 1024
PROJ_TOKENS = 512
OUT_TOKENS = 1024
HGRN_CHUNK = 64
HGRN_MAX_SEQS = 4
HGRN_SAMPLE_SEQS = 16


def _sigmoid(x):
    return 1.0 / (1.0 + jnp.exp(-x))


def _silu(x):
    return x * _sigmoid(x)


def _dot(a, b):
    return jnp.dot(a, b, preferred_element_type=F32)


def _dot_nt(a, b):
    return lax.dot_general(a, b, (((1,), (1,)), ((), ())), preferred_element_type=F32)


def _dot_tn(a, b):
    return lax.dot_general(a, b, (((0,), (0,)), ((), ())), preferred_element_type=F32)


def _alibi_slopes(n_heads_per_group):
    a_heads = N_GROUPS * n_heads_per_group
    s = 2.0 ** (-8.0 * (np.arange(a_heads) + 1) / a_heads)
    return s.astype(np.float32).reshape(N_GROUPS, n_heads_per_group)


def _resident(shape):
    nd = len(shape)
    return pl.BlockSpec(shape, lambda *_: (0,) * nd, pipeline_mode=pl.Buffered(1))


def _ada_kernel(c_ref, w_ref, b_ref, o_ref):
    o_ref[...] = _dot(_silu(c_ref[...]).astype(BF16), w_ref[...].astype(BF16)) + b_ref[...]


def _ada_call(c, w_ada, b_ada):
    n, d = c.shape
    cols = w_ada.shape[1]
    blk = d
    return pl.pallas_call(
        _ada_kernel,
        grid=(cols // blk,),
        in_specs=[pl.BlockSpec((n, d), lambda j: (0, 0)),
                  pl.BlockSpec((d, blk), lambda j: (0, j)),
                  pl.BlockSpec((1, blk), lambda j: (0, j))],
        out_specs=pl.BlockSpec((n, blk), lambda j: (0, j)),
        out_shape=jax.ShapeDtypeStruct((n, cols), F32),
        name="ada",
    )(c, w_ada, b_ada.reshape(1, cols))


def _store_dilated(out_ref, val, scr, dil):
    bs, d, rows, width = out_ref.shape
    if dil == 1:
        out_ref[...] = val.reshape(bs, 1, rows, width).astype(out_ref.dtype)
        return
    for c in range(width // LANES):
        scr[c] = val[:, c * LANES:(c + 1) * LANES]
    for r in range(dil):
        for c in range(width // LANES):
            out_ref[0, r, :, c * LANES:(c + 1) * LANES] = scr[c, pl.ds(r, rows, stride=dil), :].astype(out_ref.dtype)


def _proj_kernel(x_ref, shift_ref, scale_ref, ng_ref, w_ref, qg_ref, kg_ref, lbl_ref, *refs,
                 n_groups, dils, a_out, d_model, merged):
    n_qkv = 3 if merged else 3 * n_groups
    qkv_refs = refs[:n_qkv]
    sza_ref, sqb_ref, g_ref, kf_ref, ib_ref, szb_ref, sga_ref, sgb_ref = refs[n_qkv:n_qkv + 8]
    win_refs, scr = refs[n_qkv + 8:-1], refs[-1]
    x = x_ref[...]
    ms = jnp.mean(x * x, axis=-1, keepdims=True)
    h = x * lax.rsqrt(ms + EPS) * ng_ref[...]
    h = h * (1.0 + scale_ref[...]) + shift_ref[...]
    tm = h.shape[0] * h.shape[1]
    hb = h.reshape(tm, d_model).astype(BF16)
    tile = a_out

    def proj(c0):
        return _dot(hb, w_ref[:, c0:c0 + tile])

    def head_norm(u, gain):
        u2 = u * u
        low = lax.broadcasted_iota(jnp.int32, (tm, LANES), 1) < HEAD_DIM
        outs = []
        for c in range(tile // LANES):
            s = u2[:, c * LANES:(c + 1) * LANES]
            lo = jnp.sum(jnp.where(low, s, 0.0), axis=-1, keepdims=True)
            hi = jnp.sum(jnp.where(low, 0.0, s), axis=-1, keepdims=True)
            r_lo = lax.rsqrt(lo * (1.0 / HEAD_DIM) + EPS)
            r_hi = lax.rsqrt(hi * (1.0 / HEAD_DIM) + EPS)
            outs.append(u[:, c * LANES:(c + 1) * LANES] * jnp.where(low, r_lo, r_hi))
        return jnp.concatenate(outs, axis=1) * gain

    c0 = 0
    for kind in range(3):
        for g in range(n_groups):
            u = proj(c0)
            if kind == 0:
                u = head_norm(u, qg_ref[...])
            elif kind == 1:
                u = head_norm(u, kg_ref[...])
            if merged:
                ref = qkv_refs[kind]
                ref[:, :, g * tile:(g + 1) * tile] = u.reshape(ref.shape[0], ref.shape[1], tile).astype(ref.dtype)
            else:
                _store_dilated(qkv_refs[kind * n_groups + g], u, scr, dils[g])
            if win_refs and kind > 0:
                cols = win_refs[g].shape[3]
                win_refs[g][0, kind - 1] = u[tm - cols:, :].T
            c0 += tile
    sza_ref[...] = _silu(proj(c0)).astype(BF16)
    c0 += tile
    sqb_ref[...] = _silu(proj(c0)).astype(BF16)
    c0 += tile
    lbl = lbl_ref[...]
    lbe = jnp.exp(lbl - jnp.max(lbl, axis=0, keepdims=True))
    lb = lbe[0:1] / jnp.sum(lbe, axis=0, keepdims=True)
    fr = proj(c0)
    g_ref[...] = jnp.log2(lb + (1.0 - lb) * _sigmoid(fr))
    kf_ref[...] = ((1.0 - lb) * _sigmoid(-fr)).astype(BF16)
    c0 += tile
    ib_ref[...] = proj(c0).astype(BF16)
    c0 += tile
    szb_ref[...] = _silu(proj(c0)).astype(BF16)
    c0 += tile
    for j in range(d_model // tile):
        sga_ref[:, j * tile:(j + 1) * tile] = _sigmoid(proj(c0)).astype(BF16)
        c0 += tile
    for j in range(d_model // tile):
        sgb_ref[:, j * tile:(j + 1) * tile] = _sigmoid(proj(c0)).astype(BF16)
        c0 += tile


def _token_tiling(s, t, tokens):
    if t >= tokens:
        assert t % tokens == 0
        return 1, tokens
    bs = min(s, tokens // t)
    assert s % bs == 0 and t % SUBLANES == 0
    return bs, t


def _proj_call(x, shift, scale, norm_gain, w_in_bf, qg, kg, lb_logits, *, n_groups, dils, a_out, b_f, kv_dtype,
               windows=()):
    s, t, d = x.shape
    bs, bt = _token_tiling(s, t, PROJ_TOKENS)
    tm = bs * bt
    n_t = t // bt
    ntok = s * t
    grid = (s // bs, n_t)
    assert all(dl == 1 or (bs == 1 and bt % (dl * BF16_ROWS) == 0) for dl in dils)

    def tok_spec(cols):
        return pl.BlockSpec((tm, cols), lambda i, j: (i * n_t + j, 0))

    merged = all(dl == 1 for dl in dils)
    if merged:
        qkv_shapes = [jax.ShapeDtypeStruct((s, t, n_groups * a_out), dt) for dt in (BF16, kv_dtype, kv_dtype)]
        qkv_specs = [pl.BlockSpec((bs, bt, n_groups * a_out), lambda i, j: (i, j, 0))] * 3
    else:
        qkv_dtypes = [BF16] * n_groups + [kv_dtype] * (2 * n_groups)
        qkv_shapes = [jax.ShapeDtypeStruct((s, dils[g], t // dils[g], a_out), dt)
                      for g, dt in zip(list(range(n_groups)) * 3, qkv_dtypes)]
        qkv_specs = [pl.BlockSpec((bs, dils[g], bt // dils[g], a_out), lambda i, j: (i, 0, j, 0))
                     for g in list(range(n_groups)) * 3]
    out_cols = [(a_out, BF16), (b_f, BF16), (b_f, F32), (b_f, BF16), (b_f, BF16), (b_f, BF16), (d, BF16), (d, BF16)]
    win_shapes, win_specs = [], []
    for w in windows:
        cols = min(w, bt)
        assert bs == 1 and w % cols == 0 and t % cols == 0
        first = (t - w) // cols
        per_step = bt // cols
        win_shapes.append(jax.ShapeDtypeStruct((s, 2, a_out, w), F32))
        win_specs.append(pl.BlockSpec(
            (1, 2, a_out, cols),
            lambda i, j, first=first, per_step=per_step: (i, 0, 0, jnp.maximum((j + 1) * per_step - 1 - first, 0))))
    kern = functools.partial(_proj_kernel, n_groups=n_groups, dils=tuple(dils), a_out=a_out, d_model=d, merged=merged)
    return pl.pallas_call(
        kern,
        grid=grid,
        in_specs=[pl.BlockSpec((bs, bt, d), lambda i, j: (i, j, 0)),
                  pl.BlockSpec((bs, 1, d), lambda i, j: (i, 0, 0)),
                  pl.BlockSpec((bs, 1, d), lambda i, j: (i, 0, 0)),
                  _resident((1, d)),
                  _resident(w_in_bf.shape),
                  _resident(qg.shape),
                  _resident(kg.shape),
                  _resident(lb_logits.shape)],
        out_specs=qkv_specs + [tok_spec(c) for c, _ in out_cols] + win_specs,
        out_shape=qkv_shapes + [jax.ShapeDtypeStruct((ntok, c), dt) for c, dt in out_cols] + win_shapes,
        scratch_shapes=[pltpu.VMEM((a_out // LANES, tm, LANES), F32)],
        compiler_params=pltpu.CompilerParams(dimension_semantics=("arbitrary", "arbitrary"),
                                             vmem_limit_bytes=V7X_VMEM_LIMIT),
        name="proj",
    )(x, shift, scale, norm_gain.reshape(1, d), w_in_bf, qg, kg, lb_logits)


def _prompt_attn(q_ref, kp_ref, kc_ref, vp_ref, vc_ref, bias_ref, o_ref, lse_ref, *, n_heads, n_blk):
    first = (pl.program_id(2) == 0).astype(jnp.int32)
    lane = lax.broadcasted_iota(jnp.int32, (ATTN_BLOCK, 2 * HEAD_DIM), 1)
    low = lane < HEAD_DIM
    for blk in range(n_blk):
        rows = slice(blk * ATTN_BLOCK, (blk + 1) * ATTN_BLOCK)
        before = slice((blk - 1) * ATTN_BLOCK, blk * ATTN_BLOCK)
        table = first if blk == 0 else 0
        scores, values = [], []
        for hp in range(n_heads // 2):
            sl = slice(hp * 2 * HEAD_DIM, (hp + 1) * 2 * HEAD_DIM)
            q2 = q_ref[rows, sl]
            k_prev = kp_ref[:, sl] if blk == 0 else kc_ref[before, sl]
            v_prev = vp_ref[:, sl] if blk == 0 else vc_ref[before, sl]
            k2 = jnp.concatenate([k_prev, kc_ref[rows, sl]], axis=0)
            values.append(jnp.concatenate([v_prev, vc_ref[rows, sl]], axis=0))
            for par in range(2):
                keep = low if par == 0 else jnp.logical_not(low)
                qm = jnp.where(keep, q2, jnp.zeros_like(q2))
                scores.append(_dot_nt(qm, k2) + bias_ref[table, 2 * hp + par])
        soft = []
        for s in scores:
            m = jnp.max(s, axis=-1, keepdims=True)
            p = jnp.exp2(s - m)
            soft.append((p.astype(BF16), m, jnp.sum(p, axis=-1, keepdims=True)))
        outs = [_dot(p, values[i // 2]) for i, (p, m, l) in enumerate(soft)]
        lse = jnp.zeros((ATTN_BLOCK, LANES), F32)
        for hp in range(n_heads // 2):
            sl = slice(hp * 2 * HEAD_DIM, (hp + 1) * 2 * HEAD_DIM)
            (_, m0, l0), (_, m1, l1) = soft[2 * hp], soft[2 * hp + 1]
            o_ref[rows, sl] = jnp.where(low, outs[2 * hp] / l0, outs[2 * hp + 1] / l1).astype(o_ref.dtype)
            lse = jnp.where(lane == 2 * hp, m0 + jnp.log2(l0), lse)
            lse = jnp.where(lane == 2 * hp + 1, m1 + jnp.log2(l1), lse)
        lse_ref[rows, :] = lse


def _sample_attn(q_ref, k_ref, v_ref, caches, biases, bn_ref, o_ref, seq, *, n_heads):
    width = n_heads * HEAD_DIM
    per_head = []
    for h in range(n_heads):
        scores, values = [], []
        for g in range(N_GROUPS):
            hs = slice(g * width + h * HEAD_DIM, g * width + (h + 1) * HEAD_DIM)
            qh = q_ref[seq, :, hs]
            scores.append(_dot(qh, caches[g][seq, 0, h].astype(BF16)) + biases[g][h])
            values.append(caches[g][seq, 1, h].astype(BF16))
            scores.append(_dot_nt(qh, k_ref[seq, :, hs].astype(BF16)) + bn_ref[g, h])
            values.append(v_ref[seq, :, hs].astype(BF16))
        per_head.append((scores, values))
    soft = []
    for scores, values in per_head:
        m = functools.reduce(jnp.maximum, [jnp.max(s, axis=-1, keepdims=True) for s in scores])
        ps = [jnp.exp2(s - m) for s in scores]
        l = functools.reduce(lambda a, b: a + b, [jnp.sum(p, axis=-1, keepdims=True) for p in ps])
        soft.append(([p.astype(BF16) for p in ps], l))
    for h, ((ps, l), (_, values)) in enumerate(zip(soft, per_head)):
        parts = [(_dot_nt(p, v) if i % 2 == 0 else _dot(p, v)) for i, (p, v) in enumerate(zip(ps, values))]
        acc = functools.reduce(lambda a, b: a + b, parts)
        o_ref[seq, :, h * HEAD_DIM:(h + 1) * HEAD_DIM] = (acc / l).astype(o_ref.dtype)


def _attn_kernel(*refs, n_heads, n_blk, n_seq, hgrn, first_block, grid):
    prompt_in, (qn_ref, kn_ref, vn_ref), refs = refs[:6], refs[6:9], refs[9:]
    caches_hbm, biases = refs[:N_GROUPS], refs[N_GROUPS:2 * N_GROUPS]
    bn_ref, refs = refs[2 * N_GROUPS], refs[2 * N_GROUPS + 1:]
    rings, sems, refs = refs[-N_GROUPS - 1:-1], refs[-1], refs[:-N_GROUPS - 1]
    if hgrn is not None:
        rec_in, refs = refs[:8], refs[8:]
        o_ref, lse_ref, os_ref, ob_ref, st_ref = refs
    else:
        o_ref, lse_ref, os_ref = refs

    total = grid[0] * grid[1] * grid[2]
    st = (pl.program_id(0) * grid[1] + pl.program_id(1)) * grid[2] + pl.program_id(2)

    def fetch(g, step, slot):
        src = caches_hbm[g].at[pl.ds((first_block + step) * n_seq, n_seq)]
        return pltpu.make_async_copy(src, rings[g].at[slot], sems.at[g, slot])

    @pl.when(st == 0)
    def _():
        for ahead in range(min(CACHE_SLOTS - 1, total)):
            for g in range(N_GROUPS):
                fetch(g, ahead, ahead).start()

    @pl.when(st + (CACHE_SLOTS - 1) < total)
    def _():
        for g in range(N_GROUPS):
            fetch(g, st + (CACHE_SLOTS - 1), lax.rem(st + (CACHE_SLOTS - 1), CACHE_SLOTS)).start()

    slot = lax.rem(st, CACHE_SLOTS)
    for g in range(N_GROUPS):
        fetch(g, st, slot).wait()
    caches = [r.at[slot] for r in rings]

    def sample_part():
        for seq in range(n_seq):
            _sample_attn(qn_ref, kn_ref, vn_ref, caches, biases, bn_ref, os_ref, seq, n_heads=n_heads)

    if hgrn is not None:
        _hgrn_prompt_part(*rec_in, ob_ref, st_ref, st == 0, c=hgrn[0], n_heads=hgrn[1])
        sample_part()
        _prompt_attn(*prompt_in, o_ref, lse_ref, n_heads=n_heads, n_blk=n_blk)
    else:
        _prompt_attn(*prompt_in, o_ref, lse_ref, n_heads=n_heads, n_blk=n_blk)
        sample_part()


def _attn_tables(dil, slopes_g):
    i = np.arange(ATTN_BLOCK)[:, None]
    j = np.arange(2 * ATTN_BLOCK)[None, :]
    delta = i + ATTN_BLOCK - j
    valid = (delta >= 0) & (delta <= N_BACK)
    dist = (delta * dil).astype(np.float32)
    bias = np.where(valid[None], -slopes_g[:, None, None] * dist[None] * LOG2E, NEG).astype(np.float32)
    first = np.where((j >= ATTN_BLOCK)[None], bias, NEG).astype(np.float32)
    return jnp.asarray(np.stack([bias, first]))


def _sattn_tables(n_heads, t_new, slopes, wbs):
    t = np.arange(t_new)[:, None]
    biases, bn = [], []
    for g in range(N_GROUPS):
        d, wb = DILATIONS[g], wbs[g]
        dist = wb + t - np.arange(wb)[None, :]
        valid = (dist % d == 0) & (dist // d <= N_BACK)
        biases.append(np.where(valid[None], -slopes[g][:, None, None] * dist[None] * LOG2E, NEG).astype(np.float32))
        dist = t - np.arange(t_new)[None, :]
        valid = (dist >= 0) & (dist % d == 0) & (dist // d <= N_BACK)
        bn.append(np.where(valid[None], -slopes[g][:, None, None] * dist[None] * LOG2E, NEG).astype(np.float32))
    return [jnp.asarray(b) for b in biases], jnp.asarray(np.stack(bn))


def _attn_steps(q):
    n, dil, sub, _ = q.shape
    nb = sub // ATTN_BLOCK
    n_blk = min(ATTN_BLOCKS_PER_STEP, nb)
    assert sub % ATTN_BLOCK == 0 and nb % n_blk == 0
    return n_blk, nb // n_blk, n * dil * (nb // n_blk)


def _attn_call(q, k, v, g, n_heads, slopes, sample, first_block, n_seq, rec=None):
    n, dil, sub, width = q.shape
    assert dil == DILATIONS[g] and width == n_heads * HEAD_DIM and n_heads <= LANES
    n_blk, steps_i, steps = _attn_steps(q)
    cur = pl.BlockSpec((None, None, n_blk * ATTN_BLOCK, width), lambda b, r, i: (b, r, i, 0))
    prev = pl.BlockSpec((None, None, ATTN_BLOCK, width), lambda b, r, i: (b, r, jnp.maximum(i * n_blk - 1, 0), 0))
    lse = pl.BlockSpec((None, None, n_blk * ATTN_BLOCK, LANES), lambda b, r, i: (b, r, i, 0))
    bias = _attn_tables(dil, slopes[g])

    qn, kn, vn, caches = sample
    t_new = qn.shape[1]
    wbs = [c.shape[1] for c in caches]
    sbiases, bn = _sattn_tables(n_heads, t_new, slopes, wbs)
    views = [jnp.transpose(c, (0, 2, 3, 4, 1)) for c in caches]

    def step(b, r, i):
        return (b * dil + r) * steps_i + i

    new = pl.BlockSpec((n_seq, t_new, N_GROUPS * width), lambda b, r, i: (first_block + step(b, r, i), 0, 0))
    cache_specs = [pl.BlockSpec(memory_space=pl.ANY)] * N_GROUPS
    scratch = [pltpu.VMEM((CACHE_SLOTS, n_seq, 2, n_heads, HEAD_DIM, wb), F32) for wb in wbs]
    scratch.append(pltpu.SemaphoreType.DMA((N_GROUPS, CACHE_SLOTS)))
    in_specs = ([cur, prev, cur, prev, cur, _resident(bias.shape)] + [new] * 3 + cache_specs
                + [_resident(b.shape) for b in sbiases] + [_resident(bn.shape)])
    out_specs = [cur, lse, pl.BlockSpec((n_seq, t_new, width), lambda b, r, i: (step(b, r, i), 0, 0))]
    out_shape = [jax.ShapeDtypeStruct(q.shape, BF16), jax.ShapeDtypeStruct((n, dil, sub, LANES), F32),
                 jax.ShapeDtypeStruct((steps * n_seq, t_new, width), BF16)]
    operands = [q, k, k, v, v, bias, qn, kn, vn, *views, *sbiases, bn]
    hgrn = None
    if rec is not None:
        sq, g_log, kf, iv, og, rec_heads = rec
        n_r, t_r, width_r = sq.shape
        c = min(HGRN_CHUNK, t_r)
        assert t_r % steps == 0 and (t_r // steps) % c == 0 and n_r <= HGRN_MAX_SEQS
        tri, masks, signs = _hgrn_tables(c)
        tok = pl.BlockSpec((n_r, t_r // steps, width_r), lambda b, r, i: (0, step(b, r, i), 0))
        in_specs += [tok] * 4 + [_resident(tri.shape), _resident(masks.shape), _resident(signs.shape),
                                 _resident((1, HGRN_HEAD))]
        out_specs += [tok, pl.BlockSpec((n_r, rec_heads, HGRN_HEAD, HGRN_HEAD), lambda b, r, i: (0, 0, 0, 0))]
        out_shape += [jax.ShapeDtypeStruct((n_r, t_r, width_r), BF16),
                      jax.ShapeDtypeStruct((n_r, rec_heads, HGRN_HEAD, HGRN_HEAD), F32)]
        operands += [sq, g_log, kf, iv, tri, masks, signs, og.reshape(1, HGRN_HEAD)]
        hgrn = (c, rec_heads)
    return pl.pallas_call(
        functools.partial(_attn_kernel, n_heads=n_heads, n_blk=n_blk, n_seq=n_seq, hgrn=hgrn, first_block=first_block,
                          grid=(n, dil, steps_i)),
        grid=(n, dil, steps_i),
        in_specs=in_specs,
        out_specs=out_specs,
        out_shape=out_shape,
        scratch_shapes=scratch,
        compiler_params=pltpu.CompilerParams(dimension_semantics=("arbitrary", "arbitrary", "arbitrary"),
                                             vmem_limit_bytes=V7X_VMEM_LIMIT),
        name=f"attn_g{g}",
    )(*operands)


def _hgrn_tables(c):
    t = np.arange(c)
    tri = (t[None, :] <= t[:, None]).astype(np.float32)
    masks = [np.eye(c, dtype=bool)]
    signs = []
    w = c // 2
    while w >= 1:
        blk = t // (2 * w)
        late = (t // w) % 2 == 1
        masks.append((blk[:, None] == blk[None, :]) & late[:, None] & ~late[None, :])
        if w < SUBLANES:
            signs.append(np.where(late, 1.0, -1.0))
        w //= 2
    masks = np.stack(masks).astype(np.float32)
    signs = np.repeat(np.stack(signs).reshape(-1, 1), HGRN_HEAD, axis=1).astype(np.float32)
    return jnp.asarray(tri, BF16), jnp.asarray(masks), jnp.asarray(signs)


def _level_exponents(b, g, c):
    width = b.shape[1]
    coarse, fine = [], []
    w = c // 2
    while w >= SUBLANES:
        parts = []
        for k in range(c // (2 * w)):
            lo, mid, hi = k * 2 * w, k * 2 * w + w, (k + 1) * 2 * w
            ref = b[mid - 1:mid]
            parts.append((slice(lo, mid), False, ref - b[lo:mid]))
            parts.append((slice(mid, hi), True, b[mid:hi] - ref))
        coarse.append(parts)
        w //= 2
    assert SUBLANES == 8, "the three finest levels below are written for 8-row registers"
    b3 = b.reshape(c // SUBLANES, SUBLANES, width)
    sub = lax.broadcasted_iota(jnp.int32, b3.shape, 1)
    if c >= SUBLANES:
        fine.append((b3 - jnp.broadcast_to(b3[:, 3:4], b3.shape)).reshape(c, width))
    ref = jnp.where(sub < 4, jnp.broadcast_to(b3[:, 1:2], b3.shape), jnp.broadcast_to(b3[:, 5:6], b3.shape))
    fine.append((b3 - ref).reshape(c, width))
    row = lax.broadcasted_iota(jnp.int32, b.shape, 0)
    fine.append(jnp.where(row % 2 == 1, g, 0.0))
    return coarse, fine


def _hgrn_chunks(seqs, tri, masks, signs, og, c):
    n_lev = masks.shape[0] - 1
    tri_v = tri[...]
    og = og[...]
    pre = []
    for sq, g, kf, iv, state in seqs:
        g1 = g.astype(BF16)
        r1 = g - g1.astype(F32)
        g2 = r1.astype(BF16)
        g3 = (r1 - g2.astype(F32)).astype(BF16)
        b = _dot(tri_v, g1) + _dot(tri_v, g2) + _dot(tri_v, g3)
        pre.append((b, _level_exponents(b, g, c)))
    heads = []
    for (sq, g, kf, iv, state), (b, (coarse, fine)) in zip(seqs, pre):
        e_cum = jnp.exp2(b)
        e_tail = jnp.exp2(b[c - 1:c] - b)
        for h in range(len(state)):
            hs = slice(h * HGRN_HEAD, (h + 1) * HGRN_HEAD)
            sqh, kfh, ivh = sq[:, hs], kf[:, hs], iv[:, hs]
            sqf, kff = sqh.astype(F32), kfh.astype(F32)
            eb = e_cum[:, hs]
            prods = [_dot_nt(sqh, kfh)]
            for parts in coarse:
                x = jnp.concatenate([(sqf if late else kff)[rows] * jnp.exp2(e[:, hs]) for rows, late, e in parts],
                                    axis=0).astype(BF16)
                prods.append(_dot_nt(x, x))
            for lv, e in enumerate(fine):
                sgn = signs[lv * c:(lv + 1) * c]
                x = (jnp.where(sgn > 0.0, sqf, kff) * jnp.exp2(e[:, hs] * sgn)).astype(BF16)
                prods.append(_dot_nt(x, x))
            o_inter = _dot((sqf * eb).astype(BF16), state[h].astype(BF16))
            kt = (kff * e_tail[:, hs]).astype(BF16)
            heads.append((prods, o_inter, _dot_tn(kt, ivh), eb, ivh, state[h]))
    results = []
    for prods, o_inter, upd, eb, ivh, st in heads:
        att = masks[0] * prods[0]
        for lv in range(n_lev):
            att = att + masks[1 + lv] * prods[1 + lv]
        o = o_inter + _dot(att.astype(BF16), ivh)
        decay = jnp.broadcast_to(eb[c - 1:c, :], (HGRN_HEAD, HGRN_HEAD)).T
        results.append((o * lax.rsqrt(jnp.mean(o * o, axis=-1, keepdims=True) + EPS) * og, decay * st + upd))
    n_h = len(seqs[0][4])
    return [([r[0] for r in results[i * n_h:(i + 1) * n_h]], [r[1] for r in results[i * n_h:(i + 1) * n_h]])
            for i in range(len(seqs))]


def _hgrn_prompt_part(sq_ref, g_ref, kf_ref, iv_ref, tri_ref, masks_ref, signs_ref, og_ref, o_ref, s_ref, first,
                      *, c, n_heads):
    @pl.when(first)
    def _():
        s_ref[...] = jnp.zeros_like(s_ref)

    def body(ci, carry):
        rows = pl.ds(pl.multiple_of(ci * c, c), c)
        seqs = [(sq_ref[b, rows, :], g_ref[b, rows, :], kf_ref[b, rows, :], iv_ref[b, rows, :],
                 [s_ref[b, h] for h in range(n_heads)]) for b in range(sq_ref.shape[0])]
        res = _hgrn_chunks(seqs, tri_ref, masks_ref, signs_ref, og_ref, c)
        for b, (outs, new_state) in enumerate(res):
            for h in range(n_heads):
                s_ref[b, h] = new_state[h]
                o_ref[b, rows, h * HGRN_HEAD:(h + 1) * HGRN_HEAD] = outs[h].astype(o_ref.dtype)
        return carry

    lax.fori_loop(0, sq_ref.shape[1] // c, body, 0)


def _hgrn_sample_kernel(sq_ref, g_ref, kf_ref, iv_ref, s0_ref, tri_ref, masks_ref, signs_ref, og_ref, o_ref, s_ref,
                        *, c, n_heads):
    seqs = [(sq_ref[b], g_ref[b], kf_ref[b], iv_ref[b], [s0_ref[b, h] for h in range(n_heads)])
            for b in range(sq_ref.shape[0])]
    res = _hgrn_chunks(seqs, tri_ref, masks_ref, signs_ref, og_ref, c)
    for b, (outs, new_state) in enumerate(res):
        for h in range(n_heads):
            s_ref[b, h] = new_state[h]
            o_ref[b, :, h * HGRN_HEAD:(h + 1) * HGRN_HEAD] = outs[h].astype(o_ref.dtype)


def _hgrn_sample_call(sq, g, kf, iv, s0, og, n, t, n_heads):
    width = n_heads * HGRN_HEAD
    bn = min(HGRN_SAMPLE_SEQS, n)
    assert n % bn == 0 and t % SUBLANES == 0
    tri, masks, signs = _hgrn_tables(t)
    tok = pl.BlockSpec((bn, t, width), lambda i: (i, 0, 0))
    st = pl.BlockSpec((bn, n_heads, HGRN_HEAD, HGRN_HEAD), lambda i: (i, 0, 0, 0))
    o, s = pl.pallas_call(
        functools.partial(_hgrn_sample_kernel, c=t, n_heads=n_heads),
        grid=(n // bn,),
        in_specs=[tok, tok, tok, tok, st, _resident(tri.shape), _resident(masks.shape), _resident(signs.shape),
                  _resident((1, HGRN_HEAD))],
        out_specs=[tok, st],
        out_shape=[jax.ShapeDtypeStruct((n, t, width), BF16),
                   jax.ShapeDtypeStruct((n, n_heads, HGRN_HEAD, HGRN_HEAD), F32)],
        compiler_params=pltpu.CompilerParams(dimension_semantics=("arbitrary",), vmem_limit_bytes=V7X_VMEM_LIMIT),
        name="hgrn_sample",
    )(sq.reshape(n, t, width), g.reshape(n, t, width), kf.reshape(n, t, width), iv.reshape(n, t, width),
      s0, tri, masks, signs, og.reshape(1, HGRN_HEAD))
    return o.reshape(n * t, width), s


def _load_tokens(ref, scr):
    bs, dil, rows, width = ref.shape
    if dil == 1:
        return ref[...].reshape(bs * rows, width).astype(F32)
    for r in range(dil):
        for c in range(width // LANES):
            scr[c, pl.ds(r, rows, stride=dil), :] = ref[0, r, :, c * LANES:(c + 1) * LANES].astype(F32)
    return jnp.concatenate([scr[c] for c in range(width // LANES)], axis=1)


def _out_kernel(*refs, n_att):
    att = refs[:2 * n_att] if n_att > 1 else refs[:1]
    n_in = len(att)
    sza_ref, ob_ref, szb_ref, sga_ref, sgb_ref, x_ref, gate_ref, ex_ref, wa_ref, wb_ref, wo_ref, y_ref, scr = refs[n_in:]
    if n_att > 1:
        ls_ = [_load_tokens(r, scr) for r in att[n_att:]]
        m = functools.reduce(jnp.maximum, ls_)
        es = [jnp.exp2(l - m) for l in ls_]
        inv = 1.0 / functools.reduce(lambda a, b: a + b, es)
        alphas = [_dot((e * inv).astype(BF16), ex_ref[...]) for e in es]
        o_a = functools.reduce(lambda a, b: a + b, [al * _load_tokens(r, scr) for al, r in zip(alphas, att[:n_att])])
    else:
        o_a = _load_tokens(att[0], scr)
    branch_a = _dot((o_a * sza_ref[...].astype(F32)).astype(BF16), wa_ref[...])
    branch_b = _dot((ob_ref[...].astype(F32) * szb_ref[...].astype(F32)).astype(BF16), wb_ref[...])
    merged = sga_ref[...].astype(F32) * branch_a + sgb_ref[...].astype(F32) * branch_b
    upd = _dot(merged.astype(BF16), wo_ref[...])
    x = x_ref[...]
    y_ref[...] = x + gate_ref[...] * upd.reshape(x.shape)


def _out_call(att_inputs, sza, ob, szb, sga, sgb, x, gate, wa_bf, wb_bf, wo_bf):
    s, t, d = x.shape
    bs, bt = _token_tiling(s, t, OUT_TOKENS)
    tm = bs * bt
    n_t = t // bt

    def tok_spec(a):
        return pl.BlockSpec((tm, a.shape[1]), lambda i, j: (i * n_t + j, 0))

    def att_spec(a):
        dil = a.shape[1]
        assert dil == 1 or bs == 1
        return pl.BlockSpec((bs, dil, bt // dil, a.shape[3]), lambda i, j: (i, 0, j, 0))

    n_att = (len(att_inputs) + 1) // 2
    toks = [sza, ob, szb, sga, sgb]
    width = att_inputs[0].shape[3]
    lane_head = np.arange(LANES)[:, None] == (np.arange(width) // HEAD_DIM)[None, :]
    expand = jnp.asarray(lane_head, BF16)
    return pl.pallas_call(
        functools.partial(_out_kernel, n_att=n_att),
        grid=(s // bs, n_t),
        in_specs=[att_spec(a) for a in att_inputs] + [tok_spec(a) for a in toks]
        + [pl.BlockSpec((bs, bt, d), lambda i, j: (i, j, 0)),
           pl.BlockSpec((bs, 1, d), lambda i, j: (i, 0, 0)),
           _resident(expand.shape), _resident(wa_bf.shape), _resident(wb_bf.shape), _resident(wo_bf.shape)],
        out_specs=pl.BlockSpec((bs, bt, d), lambda i, j: (i, j, 0)),
        out_shape=jax.ShapeDtypeStruct((s, t, d), F32),
        scratch_shapes=[pltpu.VMEM((width // LANES, tm, LANES), F32)],
        compiler_params=pltpu.CompilerParams(dimension_semantics=("arbitrary", "arbitrary"),
                                             vmem_limit_bytes=V7X_VMEM_LIMIT),
        name="out",
    )(*att_inputs, *toks, x, gate, expand, wa_bf, wb_bf, wo_bf)


def _kvnew_kernel(k_ref, v_ref, *out_refs):
    width = out_refs[0].shape[2]
    for g, o_ref in enumerate(out_refs):
        for kv, src in enumerate((k_ref, v_ref)):
            for t in range(src.shape[1]):
                o_ref[t, kv] = src[:, t, g * width:(g + 1) * width].T


def _kvnew_call(kn, vn, n_groups):
    n, t, cols = kn.shape
    return pl.pallas_call(
        _kvnew_kernel,
        out_shape=[jax.ShapeDtypeStruct((t, 2, cols // n_groups, n), F32) for _ in range(n_groups)],
        compiler_params=pltpu.CompilerParams(vmem_limit_bytes=V7X_VMEM_LIMIT),
        name="kvnew",
    )(kn, vn)


def kernel(x_prompt, x_sample, cache_kv_g0, cache_kv_g1, cache_kv_g2, state_hgrn, c_prompt, c_sample, norm_gain, w_ada, b_ada, w_in, q_norm_gain, k_norm_gain, hgrn_lb_logits, hgrn_out_norm_gain, w_branch_a, w_branch_b, w_out):
    depth = w_in.shape[0]
    assert depth == 1, "single-layer step"
    n_p, t_p, d = x_prompt.shape
    n_s, t_s, _ = x_sample.shape
    a_out = w_branch_a.shape[1]
    n_heads = a_out // HEAD_DIM
    b_f = w_branch_b.shape[1]
    b_heads = b_f // HGRN_HEAD
    slopes = _alibi_slopes(n_heads)

    n_c = n_p + n_s
    pad = (-n_c) % SUBLANES
    c_all = jnp.concatenate([c_prompt, c_sample, jnp.zeros((pad, d), F32)], axis=0)
    ada = _ada_call(c_all, w_ada[0], b_ada[0])
    shift, scale, gate = ada[:, :d], ada[:, d:2 * d], ada[:, 2 * d:]

    w_in_bf = w_in[0].astype(BF16)
    wa_bf, wb_bf, wo_bf = w_branch_a[0].astype(BF16), w_branch_b[0].astype(BF16), w_out[0].astype(BF16)
    qg = (jnp.tile(q_norm_gain[0], n_heads) * (ATTN_SCALE * LOG2E)).reshape(1, a_out)
    kg = jnp.tile(k_norm_gain[0], n_heads).reshape(1, a_out)
    og = hgrn_out_norm_gain[0]

    def cond(lo, n):
        return [a[lo:lo + n].reshape(n, 1, d) for a in (shift, scale, gate)]

    sh, sc, gt_p = cond(0, n_p)
    outs = _proj_call(x_prompt, sh, sc, norm_gain[0], w_in_bf, qg, kg, hgrn_lb_logits, n_groups=N_GROUPS,
                      dils=DILATIONS, a_out=a_out, b_f=b_f, kv_dtype=BF16,
                      windows=tuple(min(w, t_p) for w in WINDOWS))
    qs, ks, vs = outs[0:N_GROUPS], outs[N_GROUPS:2 * N_GROUPS], outs[2 * N_GROUPS:3 * N_GROUPS]
    act_p, kv_rows = outs[3 * N_GROUPS:3 * N_GROUPS + 8], outs[3 * N_GROUPS + 8:]
    sh, sc, gt_s = cond(n_p, n_s)
    outs = _proj_call(x_sample, sh, sc, norm_gain[0], w_in_bf, qg, kg, hgrn_lb_logits, n_groups=N_GROUPS,
                      dils=(1,) * N_GROUPS, a_out=a_out, b_f=b_f, kv_dtype=F32)
    qn, kn, vn = outs[:3]
    act_s = outs[3:3 + 8]

    steps = [_attn_steps(qs[gi])[2] for gi in range(N_GROUPS)]
    assert n_s % sum(steps) == 0, "sample sequences must spread evenly over the prompt attention grid steps"
    n_seq = n_s // sum(steps)
    sample = (qn, kn, vn, (cache_kv_g0[0], cache_kv_g1[0], cache_kv_g2[0]))
    sza, sqb, g, kf, ib, szb, sga, sgb = act_p
    carrier = int(np.argmax(steps))
    rec = tuple(a.reshape(n_p, t_p, b_f) for a in (sqb, g, kf, ib)) + (og, b_heads)
    att = [_attn_call(qs[gi], ks[gi], vs[gi], gi, n_heads, slopes, sample, sum(steps[:gi]), n_seq,
                      rec if gi == carrier else None) for gi in range(N_GROUPS)]
    o_a = jnp.concatenate([a[2] for a in att], axis=0)
    ob, hgrn_p = att[carrier][3].reshape(n_p * t_p, b_f), att[carrier][4]

    y_p = _out_call([a[0] for a in att] + [a[1] for a in att], sza, ob, szb, sga, sgb, x_prompt, gt_p,
                    wa_bf, wb_bf, wo_bf)
    kv_p = [jnp.transpose(a.reshape(n_p, 2, n_heads, HEAD_DIM, a.shape[3]), (0, 4, 1, 2, 3))[None]
            for a in kv_rows]

    sza, sqb, g, kf, ib, szb, sga, sgb = act_s
    ob, hgrn_s = _hgrn_sample_call(sqb, g, kf, ib, state_hgrn[0], og, n_s, t_s, b_heads)
    y_s = _out_call([o_a.reshape(n_s, 1, t_s, a_out)], sza, ob, szb, sga, sgb, x_sample, gt_s, wa_bf, wb_bf, wo_bf)
    kv_s = [jnp.transpose(a.reshape(t_s, 2, n_heads, HEAD_DIM, n_s), (4, 0, 1, 2, 3))[None]
            for a in _kvnew_call(kn, vn, N_GROUPS)]

    return (y_p, y_s, kv_p[0], kv_p[1], kv_p[2], hgrn_p[None], kv_s[0], kv_s[1], kv_s[2], hgrn_s[None])
```

```python
import functools

import numpy as np
import jax
import jax.numpy as jnp
from jax import lax
from jax.experimental import pallas as pl
from jax.experimental.pallas import tpu as pltpu

F32 = jnp.float32
BF16 = jnp.bfloat16

HEAD_DIM = 64
N_GROUPS = 3
WINDOWS = (128, 512, 2048)
DILATIONS = (1, 4, 16)
N_BACK = 128
ATTN_BLOCK = 128
ATTN_BLOCKS_PER_STEP = 4
CACHE_SLOTS = 3
CACHE_DMA_PRIORITY = 1
ATTN_SCALE = HEAD_DIM ** -0.5
HGRN_HEAD = 128
LANES = 128
SUBLANES = 8
BF16_ROWS = 16
LOG2E = 1.4426950408889634
EPS = 1e-6
NEG = -1e30

V7X_VMEM_LIMIT = 60 * 1024 * 1024
PROJ_TOKENS = 512
OUT_TOKENS = 1024
HGRN_CHUNK = 64
HGRN_MAX_SEQS = 4
HGRN_SAMPLE_SEQS = 16


def _sigmoid(x):
    return 1.0 / (1.0 + jnp.exp(-x))


def _silu(x):
    return x * _sigmoid(x)


def _dot(a, b):
    return jnp.dot(a, b, preferred_element_type=F32)


def _dot_nt(a, b):
    return lax.dot_general(a, b, (((1,), (1,)), ((), ())), preferred_element_type=F32)


def _dot_tn(a, b):
    return lax.dot_general(a, b, (((0,), (0,)), ((), ())), preferred_element_type=F32)


def _alibi_slopes(n_heads_per_group):
    a_heads = N_GROUPS * n_heads_per_group
    s = 2.0 ** (-8.0 * (np.arange(a_heads) + 1) / a_heads)
    return s.astype(np.float32).reshape(N_GROUPS, n_heads_per_group)


def _resident(shape):
    nd = len(shape)
    return pl.BlockSpec(shape, lambda *_: (0,) * nd, pipeline_mode=pl.Buffered(1))


def _ada_kernel(c_ref, w_ref, b_ref, o_ref):
    o_ref[...] = _dot(_silu(c_ref[...]).astype(BF16), w_ref[...].astype(BF16)) + b_ref[...]


def _ada_call(c, w_ada, b_ada):
    n, d = c.shape
    cols = w_ada.shape[1]
    blk = d
    return pl.pallas_call(
        _ada_kernel,
        grid=(cols // blk,),
        in_specs=[pl.BlockSpec((n, d), lambda j: (0, 0)),
                  pl.BlockSpec((d, blk), lambda j: (0, j)),
                  pl.BlockSpec((1, blk), lambda j: (0, j))],
        out_specs=pl.BlockSpec((n, blk), lambda j: (0, j)),
        out_shape=jax.ShapeDtypeStruct((n, cols), F32),
        name="ada",
    )(c, w_ada, b_ada.reshape(1, cols))


def _store_dilated(out_ref, val, scr, dil):
    bs, d, rows, width = out_ref.shape
    if dil == 1:
        out_ref[...] = val.reshape(bs, 1, rows, width).astype(out_ref.dtype)
        return
    for c in range(width // LANES):
        scr[c] = val[:, c * LANES:(c + 1) * LANES]
    for r in range(dil):
        for c in range(width // LANES):
            out_ref[0, r, :, c * LANES:(c + 1) * LANES] = scr[c, pl.ds(r, rows, stride=dil), :].astype(out_ref.dtype)


def _proj_kernel(x_ref, shift_ref, scale_ref, ng_ref, w_ref, qg_ref, kg_ref, lbl_ref, *refs,
                 n_groups, dils, a_out, d_model, merged):
    n_qkv = 3 if merged else 3 * n_groups
    qkv_refs = refs[:n_qkv]
    sza_ref, sqb_ref, g_ref, kf_ref, ib_ref, szb_ref, sga_ref, sgb_ref = refs[n_qkv:n_qkv + 8]
    win_refs, scr = refs[n_qkv + 8:-1], refs[-1]
    x = x_ref[...]
    ms = jnp.mean(x * x, axis=-1, keepdims=True)
    h = x * lax.rsqrt(ms + EPS) * ng_ref[...]
    h = h * (1.0 + scale_ref[...]) + shift_ref[...]
    tm = h.shape[0] * h.shape[1]
    hb = h.reshape(tm, d_model).astype(BF16)
    tile = a_out

    def proj(c0):
        return _dot(hb, w_ref[:, c0:c0 + tile])

    def head_norm(u, gain):
        u2 = u * u
        low = lax.broadcasted_iota(jnp.int32, (tm, LANES), 1) < HEAD_DIM
        outs = []
        for c in range(tile // LANES):
            s = u2[:, c * LANES:(c + 1) * LANES]
            lo = jnp.sum(jnp.where(low, s, 0.0), axis=-1, keepdims=True)
            hi = jnp.sum(jnp.where(low, 0.0, s), axis=-1, keepdims=True)
            r_lo = lax.rsqrt(lo * (1.0 / HEAD_DIM) + EPS)
            r_hi = lax.rsqrt(hi * (1.0 / HEAD_DIM) + EPS)
            outs.append(u[:, c * LANES:(c + 1) * LANES] * jnp.where(low, r_lo, r_hi))
        return jnp.concatenate(outs, axis=1) * gain

    c0 = 0
    for kind in range(3):
        for g in range(n_groups):
            u = proj(c0)
            if kind == 0:
                u = head_norm(u, qg_ref[...])
            elif kind == 1:
                u = head_norm(u, kg_ref[...])
            if merged:
                ref = qkv_refs[kind]
                ref[:, :, g * tile:(g + 1) * tile] = u.reshape(ref.shape[0], ref.shape[1], tile).astype(ref.dtype)
            else:
                _store_dilated(qkv_refs[kind * n_groups + g], u, scr, dils[g])
            if win_refs and kind > 0:
                cols = win_refs[g].shape[3]
                win_refs[g][0, kind - 1] = u[tm - cols:, :].T
            c0 += tile
    sza_ref[...] = _silu(proj(c0)).astype(BF16)
    c0 += tile
    sqb_ref[...] = _silu(proj(c0)).astype(BF16)
    c0 += tile
    lbl = lbl_ref[...]
    lbe = jnp.exp(lbl - jnp.max(lbl, axis=0, keepdims=True))
    lb = lbe[0:1] / jnp.sum(lbe, axis=0, keepdims=True)
    fr = proj(c0)
    g_ref[...] = jnp.log2(lb + (1.0 - lb) * _sigmoid(fr))
    kf_ref[...] = ((1.0 - lb) * _sigmoid(-fr)).astype(BF16)
    c0 += tile
    ib_ref[...] = proj(c0).astype(BF16)
    c0 += tile
    szb_ref[...] = _silu(proj(c0)).astype(BF16)
    c0 += tile
    for j in range(d_model // tile):
        sga_ref[:, j * tile:(j + 1) * tile] = _sigmoid(proj(c0)).astype(BF16)
        c0 += tile
    for j in range(d_model // tile):
        sgb_ref[:, j * tile:(j + 1) * tile] = _sigmoid(proj(c0)).astype(BF16)
        c0 += tile


def _token_tiling(s, t, tokens):
    if t >= tokens:
        assert t % tokens == 0
        return 1, tokens
    bs = min(s, tokens // t)
    assert s % bs == 0 and t % SUBLANES == 0
    return bs, t


def _proj_call(x, shift, scale, norm_gain, w_in_bf, qg, kg, lb_logits, *, n_groups, dils, a_out, b_f, kv_dtype,
               windows=()):
    s, t, d = x.shape
    bs, bt = _token_tiling(s, t, PROJ_TOKENS)
    tm = bs * bt
    n_t = t // bt
    ntok = s * t
    grid = (s // bs, n_t)
    assert all(dl == 1 or (bs == 1 and bt % (dl * BF16_ROWS) == 0) for dl in dils)

    def tok_spec(cols):
        return pl.BlockSpec((tm, cols), lambda i, j: (i * n_t + j, 0))

    merged = all(dl == 1 for dl in dils)
    if merged:
        qkv_shapes = [jax.ShapeDtypeStruct((s, t, n_groups * a_out), dt) for dt in (BF16, kv_dtype, kv_dtype)]
        qkv_specs = [pl.BlockSpec((bs, bt, n_groups * a_out), lambda i, j: (i, j, 0))] * 3
    else:
        qkv_dtypes = [BF16] * n_groups + [kv_dtype] * (2 * n_groups)
        qkv_shapes = [jax.ShapeDtypeStruct((s, dils[g], t // dils[g], a_out), dt)
                      for g, dt in zip(list(range(n_groups)) * 3, qkv_dtypes)]
        qkv_specs = [pl.BlockSpec((bs, dils[g], bt // dils[g], a_out), lambda i, j: (i, 0, j, 0))
                     for g in list(range(n_groups)) * 3]
    out_cols = [(a_out, BF16), (b_f, BF16), (b_f, F32), (b_f, BF16), (b_f, BF16), (b_f, BF16), (d, BF16), (d, BF16)]
    win_shapes, win_specs = [], []
    for w in windows:
        cols = min(w, bt)
        assert bs == 1 and w % cols == 0 and t % cols == 0
        first = (t - w) // cols
        per_step = bt // cols
        win_shapes.append(jax.ShapeDtypeStruct((s, 2, a_out, w), F32))
        win_specs.append(pl.BlockSpec(
            (1, 2, a_out, cols),
            lambda i, j, first=first, per_step=per_step: (i, 0, 0, jnp.maximum((j + 1) * per_step - 1 - first, 0))))
    kern = functools.partial(_proj_kernel, n_groups=n_groups, dils=tuple(dils), a_out=a_out, d_model=d, merged=merged)
    return pl.pallas_call(
        kern,
        grid=grid,
        in_specs=[pl.BlockSpec((bs, bt, d), lambda i, j: (i, j, 0)),
                  pl.BlockSpec((bs, 1, d), lambda i, j: (i, 0, 0)),
                  pl.BlockSpec((bs, 1, d), lambda i, j: (i, 0, 0)),
                  _resident((1, d)),
                  _resident(w_in_bf.shape),
                  _resident(qg.shape),
                  _resident(kg.shape),
                  _resident(lb_logits.shape)],
        out_specs=qkv_specs + [tok_spec(c) for c, _ in out_cols] + win_specs,
        out_shape=qkv_shapes + [jax.ShapeDtypeStruct((ntok, c), dt) for c, dt in out_cols] + win_shapes,
        scratch_shapes=[pltpu.VMEM((a_out // LANES, tm, LANES), F32)],
        compiler_params=pltpu.CompilerParams(dimension_semantics=("arbitrary", "arbitrary"),
                                             vmem_limit_bytes=V7X_VMEM_LIMIT),
        name="proj",
    )(x, shift, scale, norm_gain.reshape(1, d), w_in_bf, qg, kg, lb_logits)


def _prompt_attn(q_ref, kp_ref, kc_ref, vp_ref, vc_ref, bias_ref, o_ref, lse_ref, *, n_heads, n_blk):
    first = (pl.program_id(2) == 0).astype(jnp.int32)
    lane = lax.broadcasted_iota(jnp.int32, (ATTN_BLOCK, 2 * HEAD_DIM), 1)
    low = lane < HEAD_DIM
    for blk in range(n_blk):
        rows = slice(blk * ATTN_BLOCK, (blk + 1) * ATTN_BLOCK)
        before = slice((blk - 1) * ATTN_BLOCK, blk * ATTN_BLOCK)
        table = first if blk == 0 else 0
        scores, values = [], []
        for hp in range(n_heads // 2):
            sl = slice(hp * 2 * HEAD_DIM, (hp + 1) * 2 * HEAD_DIM)
            q2 = q_ref[rows, sl]
            k_prev = kp_ref[:, sl] if blk == 0 else kc_ref[before, sl]
            v_prev = vp_ref[:, sl] if blk == 0 else vc_ref[before, sl]
            k2 = jnp.concatenate([k_prev, kc_ref[rows, sl]], axis=0)
            values.append(jnp.concatenate([v_prev, vc_ref[rows, sl]], axis=0))
            for par in range(2):
                keep = low if par == 0 else jnp.logical_not(low)
                qm = jnp.where(keep, q2, jnp.zeros_like(q2))
                scores.append(_dot_nt(qm, k2) + bias_ref[table, 2 * hp + par])
        soft = []
        for s in scores:
            m = jnp.max(s, axis=-1, keepdims=True)
            p = jnp.exp2(s - m)
            soft.append((p.astype(BF16), m, jnp.sum(p, axis=-1, keepdims=True)))
        outs = [_dot(p, values[i // 2]) for i, (p, m, l) in enumerate(soft)]
        lse = jnp.zeros((ATTN_BLOCK, LANES), F32)
        for hp in range(n_heads // 2):
            sl = slice(hp * 2 * HEAD_DIM, (hp + 1) * 2 * HEAD_DIM)
            (_, m0, l0), (_, m1, l1) = soft[2 * hp], soft[2 * hp + 1]
            o_ref[rows, sl] = jnp.where(low, outs[2 * hp] / l0, outs[2 * hp + 1] / l1).astype(o_ref.dtype)
            lse = jnp.where(lane == 2 * hp, m0 + jnp.log2(l0), lse)
            lse = jnp.where(lane == 2 * hp + 1, m1 + jnp.log2(l1), lse)
        lse_ref[rows, :] = lse


def _sample_attn(q_ref, k_ref, v_ref, caches, biases, bn_ref, o_ref, seq, *, n_heads):
    width = n_heads * HEAD_DIM
    per_head = []
    for h in range(n_heads):
        scores, values = [], []
        for g in range(N_GROUPS):
            hs = slice(g * width + h * HEAD_DIM, g * width + (h + 1) * HEAD_DIM)
            qh = q_ref[seq, :, hs]
            scores.append(_dot(qh, caches[g][seq, 0, h].astype(BF16)) + biases[g][h])
            values.append(caches[g][seq, 1, h].astype(BF16))
            scores.append(_dot_nt(qh, k_ref[seq, :, hs].astype(BF16)) + bn_ref[g, h])
            values.append(v_ref[seq, :, hs].astype(BF16))
        per_head.append((scores, values))
    soft = []
    for scores, values in per_head:
        m = functools.reduce(jnp.maximum, [jnp.max(s, axis=-1, keepdims=True) for s in scores])
        ps = [jnp.exp2(s - m) for s in scores]
        l = functools.reduce(lambda a, b: a + b, [jnp.sum(p, axis=-1, keepdims=True) for p in ps])
        soft.append(([p.astype(BF16) for p in ps], l))
    for h, ((ps, l), (_, values)) in enumerate(zip(soft, per_head)):
        parts = [(_dot_nt(p, v) if i % 2 == 0 else _dot(p, v)) for i, (p, v) in enumerate(zip(ps, values))]
        acc = functools.reduce(lambda a, b: a + b, parts)
        o_ref[seq, :, h * HEAD_DIM:(h + 1) * HEAD_DIM] = (acc / l).astype(o_ref.dtype)


def _attn_kernel(*refs, n_heads, n_blk, n_seq, hgrn, first_block, grid):
    prompt_in, (qn_ref, kn_ref, vn_ref), refs = refs[:6], refs[6:9], refs[9:]
    caches_hbm, biases = refs[:N_GROUPS], refs[N_GROUPS:2 * N_GROUPS]
    bn_ref, refs = refs[2 * N_GROUPS], refs[2 * N_GROUPS + 1:]
    rings, sems, refs = refs[-N_GROUPS - 1:-1], refs[-1], refs[:-N_GROUPS - 1]
    if hgrn is not None:
        rec_in, refs = refs[:8], refs[8:]
        o_ref, lse_ref, os_ref, ob_ref, st_ref = refs
    else:
        o_ref, lse_ref, os_ref = refs

    total = grid[0] * grid[1] * grid[2]
    st = (pl.program_id(0) * grid[1] + pl.program_id(1)) * grid[2] + pl.program_id(2)

    def fetch(g, step, slot):
        src = caches_hbm[g].at[pl.ds((first_block + step) * n_seq, n_seq)]
        return pltpu.make_async_copy(src, rings[g].at[slot], sems.at[g, slot])

    @pl.when(st == 0)
    def _():
        for ahead in range(min(CACHE_SLOTS - 1, total)):
            for g in range(N_GROUPS):
                fetch(g, ahead, ahead).start(priority=CACHE_DMA_PRIORITY)

    @pl.when(st + (CACHE_SLOTS - 1) < total)
    def _():
        for g in range(N_GROUPS):
            fetch(g, st + (CACHE_SLOTS - 1), lax.rem(st + (CACHE_SLOTS - 1), CACHE_SLOTS)).start(
                priority=CACHE_DMA_PRIORITY)

    slot = lax.rem(st, CACHE_SLOTS)
    for g in range(N_GROUPS):
        fetch(g, st, slot).wait()
    caches = [r.at[slot] for r in rings]

    def sample_part():
        for seq in range(n_seq):
            _sample_attn(qn_ref, kn_ref, vn_ref, caches, biases, bn_ref, os_ref, seq, n_heads=n_heads)

    if hgrn is not None:
        _hgrn_prompt_part(*rec_in, ob_ref, st_ref, st == 0, c=hgrn[0], n_heads=hgrn[1])
        sample_part()
        _prompt_attn(*prompt_in, o_ref, lse_ref, n_heads=n_heads, n_blk=n_blk)
    else:
        _prompt_attn(*prompt_in, o_ref, lse_ref, n_heads=n_heads, n_blk=n_blk)
        sample_part()


def _attn_tables(dil, slopes_g):
    i = np.arange(ATTN_BLOCK)[:, None]
    j = np.arange(2 * ATTN_BLOCK)[None, :]
    delta = i + ATTN_BLOCK - j
    valid = (delta >= 0) & (delta <= N_BACK)
    dist = (delta * dil).astype(np.float32)
    bias = np.where(valid[None], -slopes_g[:, None, None] * dist[None] * LOG2E, NEG).astype(np.float32)
    first = np.where((j >= ATTN_BLOCK)[None], bias, NEG).astype(np.float32)
    return jnp.asarray(np.stack([bias, first]))


def _sattn_tables(n_heads, t_new, slopes, wbs):
    t = np.arange(t_new)[:, None]
    biases, bn = [], []
    for g in range(N_GROUPS):
        d, wb = DILATIONS[g], wbs[g]
        dist = wb + t - np.arange(wb)[None, :]
        valid = (dist % d == 0) & (dist // d <= N_BACK)
        biases.append(np.where(valid[None], -slopes[g][:, None, None] * dist[None] * LOG2E, NEG).astype(np.float32))
        dist = t - np.arange(t_new)[None, :]
        valid = (dist >= 0) & (dist % d == 0) & (dist // d <= N_BACK)
        bn.append(np.where(valid[None], -slopes[g][:, None, None] * dist[None] * LOG2E, NEG).astype(np.float32))
    return [jnp.asarray(b) for b in biases], jnp.asarray(np.stack(bn))


def _attn_steps(q):
    n, dil, sub, _ = q.shape
    nb = sub // ATTN_BLOCK
    n_blk = min(ATTN_BLOCKS_PER_STEP, nb)
    assert sub % ATTN_BLOCK == 0 and nb % n_blk == 0
    return n_blk, nb // n_blk, n * dil * (nb // n_blk)


def _attn_call(q, k, v, g, n_heads, slopes, sample, first_block, n_seq, rec=None):
    n, dil, sub, width = q.shape
    assert dil == DILATIONS[g] and width == n_heads * HEAD_DIM and n_heads <= LANES
    n_blk, steps_i, steps = _attn_steps(q)
    cur = pl.BlockSpec((None, None, n_blk * ATTN_BLOCK, width), lambda b, r, i: (b, r, i, 0))
    prev = pl.BlockSpec((None, None, ATTN_BLOCK, width), lambda b, r, i: (b, r, jnp.maximum(i * n_blk - 1, 0), 0))
    lse = pl.BlockSpec((None, None, n_blk * ATTN_BLOCK, LANES), lambda b, r, i: (b, r, i, 0))
    bias = _attn_tables(dil, slopes[g])

    qn, kn, vn, caches = sample
    t_new = qn.shape[1]
    wbs = [c.shape[1] for c in caches]
    sbiases, bn = _sattn_tables(n_heads, t_new, slopes, wbs)
    views = [jnp.transpose(c, (0, 2, 3, 4, 1)) for c in caches]

    def step(b, r, i):
        return (b * dil + r) * steps_i + i

    new = pl.BlockSpec((n_seq, t_new, N_GROUPS * width), lambda b, r, i: (first_block + step(b, r, i), 0, 0))
    cache_specs = [pl.BlockSpec(memory_space=pl.ANY)] * N_GROUPS
    scratch = [pltpu.VMEM((CACHE_SLOTS, n_seq, 2, n_heads, HEAD_DIM, wb), F32) for wb in wbs]
    scratch.append(pltpu.SemaphoreType.DMA((N_GROUPS, CACHE_SLOTS)))
    in_specs = ([cur, prev, cur, prev, cur, _resident(bias.shape)] + [new] * 3 + cache_specs
                + [_resident(b.shape) for b in sbiases] + [_resident(bn.shape)])
    out_specs = [cur, lse, pl.BlockSpec((n_seq, t_new, width), lambda b, r, i: (step(b, r, i), 0, 0))]
    out_shape = [jax.ShapeDtypeStruct(q.shape, BF16), jax.ShapeDtypeStruct((n, dil, sub, LANES), F32),
                 jax.ShapeDtypeStruct((steps * n_seq, t_new, width), BF16)]
    operands = [q, k, k, v, v, bias, qn, kn, vn, *views, *sbiases, bn]
    hgrn = None
    if rec is not None:
        sq, g_log, kf, iv, og, rec_heads = rec
        n_r, t_r, width_r = sq.shape
        c = min(HGRN_CHUNK, t_r)
        assert t_r % steps == 0 and (t_r // steps) % c == 0 and n_r <= HGRN_MAX_SEQS
        tri, masks, signs = _hgrn_tables(c)
        tok = pl.BlockSpec((n_r, t_r // steps, width_r), lambda b, r, i: (0, step(b, r, i), 0))
        in_specs += [tok] * 4 + [_resident(tri.shape), _resident(masks.shape), _resident(signs.shape),
                                 _resident((1, HGRN_HEAD))]
        out_specs += [tok, pl.BlockSpec((n_r, rec_heads, HGRN_HEAD, HGRN_HEAD), lambda b, r, i: (0, 0, 0, 0))]
        out_shape += [jax.ShapeDtypeStruct((n_r, t_r, width_r), BF16),
                      jax.ShapeDtypeStruct((n_r, rec_heads, HGRN_HEAD, HGRN_HEAD), F32)]
        operands += [sq, g_log, kf, iv, tri, masks, signs, og.reshape(1, HGRN_HEAD)]
        hgrn = (c, rec_heads)
    return pl.pallas_call(
        functools.partial(_attn_kernel, n_heads=n_heads, n_blk=n_blk, n_seq=n_seq, hgrn=hgrn, first_block=first_block,
                          grid=(n, dil, steps_i)),
        grid=(n, dil, steps_i),
        in_specs=in_specs,
        out_specs=out_specs,
        out_shape=out_shape,
        scratch_shapes=scratch,
        compiler_params=pltpu.CompilerParams(dimension_semantics=("arbitrary", "arbitrary", "arbitrary"),
                                             vmem_limit_bytes=V7X_VMEM_LIMIT),
        name=f"attn_g{g}",
    )(*operands)


def _hgrn_tables(c):
    t = np.arange(c)
    tri = (t[None, :] <= t[:, None]).astype(np.float32)
    masks = [np.eye(c, dtype=bool)]
    signs = []
    w = c // 2
    while w >= 1:
        blk = t // (2 * w)
        late = (t // w) % 2 == 1
        masks.append((blk[:, None] == blk[None, :]) & late[:, None] & ~late[None, :])
        if w < SUBLANES:
            signs.append(np.where(late, 1.0, -1.0))
        w //= 2
    masks = np.stack(masks).astype(np.float32)
    signs = np.repeat(np.stack(signs).reshape(-1, 1), HGRN_HEAD, axis=1).astype(np.float32)
    return jnp.asarray(tri, BF16), jnp.asarray(masks), jnp.asarray(signs)


def _level_exponents(b, g, c):
    width = b.shape[1]
    coarse, fine = [], []
    w = c // 2
    while w >= SUBLANES:
        parts = []
        for k in range(c // (2 * w)):
            lo, mid, hi = k * 2 * w, k * 2 * w + w, (k + 1) * 2 * w
            ref = b[mid - 1:mid]
            parts.append((slice(lo, mid), False, ref - b[lo:mid]))
            parts.append((slice(mid, hi), True, b[mid:hi] - ref))
        coarse.append(parts)
        w //= 2
    assert SUBLANES == 8, "the three finest levels below are written for 8-row registers"
    b3 = b.reshape(c // SUBLANES, SUBLANES, width)
    sub = lax.broadcasted_iota(jnp.int32, b3.shape, 1)
    if c >= SUBLANES:
        fine.append((b3 - jnp.broadcast_to(b3[:, 3:4], b3.shape)).reshape(c, width))
    ref = jnp.where(sub < 4, jnp.broadcast_to(b3[:, 1:2], b3.shape), jnp.broadcast_to(b3[:, 5:6], b3.shape))
    fine.append((b3 - ref).reshape(c, width))
    row = lax.broadcasted_iota(jnp.int32, b.shape, 0)
    fine.append(jnp.where(row % 2 == 1, g, 0.0))
    return coarse, fine


def _hgrn_chunks(seqs, tri, masks, signs, og, c):
    n_lev = masks.shape[0] - 1
    tri_v = tri[...]
    og = og[...]
    pre = []
    for sq, g, kf, iv, state in seqs:
        g1 = g.astype(BF16)
        r1 = g - g1.astype(F32)
        g2 = r1.astype(BF16)
        g3 = (r1 - g2.astype(F32)).astype(BF16)
        b = _dot(tri_v, g1) + _dot(tri_v, g2) + _dot(tri_v, g3)
        pre.append((b, _level_exponents(b, g, c)))
    heads = []
    for (sq, g, kf, iv, state), (b, (coarse, fine)) in zip(seqs, pre):
        e_cum = jnp.exp2(b)
        e_tail = jnp.exp2(b[c - 1:c] - b)
        for h in range(len(state)):
            hs = slice(h * HGRN_HEAD, (h + 1) * HGRN_HEAD)
            sqh, kfh, ivh = sq[:, hs], kf[:, hs], iv[:, hs]
            sqf, kff = sqh.astype(F32), kfh.astype(F32)
            eb = e_cum[:, hs]
            prods = [_dot_nt(sqh, kfh)]
            for parts in coarse:
                x = jnp.concatenate([(sqf if late else kff)[rows] * jnp.exp2(e[:, hs]) for rows, late, e in parts],
                                    axis=0).astype(BF16)
                prods.append(_dot_nt(x, x))
            for lv, e in enumerate(fine):
                sgn = signs[lv * c:(lv + 1) * c]
                x = (jnp.where(sgn > 0.0, sqf, kff) * jnp.exp2(e[:, hs] * sgn)).astype(BF16)
                prods.append(_dot_nt(x, x))
            o_inter = _dot((sqf * eb).astype(BF16), state[h].astype(BF16))
            kt = (kff * e_tail[:, hs]).astype(BF16)
            heads.append((prods, o_inter, _dot_tn(kt, ivh), eb, ivh, state[h]))
    results = []
    for prods, o_inter, upd, eb, ivh, st in heads:
        att = masks[0] * prods[0]
        for lv in range(n_lev):
            att = att + masks[1 + lv] * prods[1 + lv]
        o = o_inter + _dot(att.astype(BF16), ivh)
        decay = jnp.broadcast_to(eb[c - 1:c, :], (HGRN_HEAD, HGRN_HEAD)).T
        results.append((o * lax.rsqrt(jnp.mean(o * o, axis=-1, keepdims=True) + EPS) * og, decay * st + upd))
    n_h = len(seqs[0][4])
    return [([r[0] for r in results[i * n_h:(i + 1) * n_h]], [r[1] for r in results[i * n_h:(i + 1) * n_h]])
            for i in range(len(seqs))]


def _hgrn_prompt_part(sq_ref, g_ref, kf_ref, iv_ref, tri_ref, masks_ref, signs_ref, og_ref, o_ref, s_ref, first,
                      *, c, n_heads):
    @pl.when(first)
    def _():
        s_ref[...] = jnp.zeros_like(s_ref)

    def body(ci, carry):
        rows = pl.ds(pl.multiple_of(ci * c, c), c)
        seqs = [(sq_ref[b, rows, :], g_ref[b, rows, :], kf_ref[b, rows, :], iv_ref[b, rows, :],
                 [s_ref[b, h] for h in range(n_heads)]) for b in range(sq_ref.shape[0])]
        res = _hgrn_chunks(seqs, tri_ref, masks_ref, signs_ref, og_ref, c)
        for b, (outs, new_state) in enumerate(res):
            for h in range(n_heads):
                s_ref[b, h] = new_state[h]
                o_ref[b, rows, h * HGRN_HEAD:(h + 1) * HGRN_HEAD] = outs[h].astype(o_ref.dtype)
        return carry

    lax.fori_loop(0, sq_ref.shape[1] // c, body, 0)


def _hgrn_sample_kernel(sq_ref, g_ref, kf_ref, iv_ref, s0_ref, tri_ref, masks_ref, signs_ref, og_ref, o_ref, s_ref,
                        *, c, n_heads):
    seqs = [(sq_ref[b], g_ref[b], kf_ref[b], iv_ref[b], [s0_ref[b, h] for h in range(n_heads)])
            for b in range(sq_ref.shape[0])]
    res = _hgrn_chunks(seqs, tri_ref, masks_ref, signs_ref, og_ref, c)
    for b, (outs, new_state) in enumerate(res):
        for h in range(n_heads):
            s_ref[b, h] = new_state[h]
            o_ref[b, :, h * HGRN_HEAD:(h + 1) * HGRN_HEAD] = outs[h].astype(o_ref.dtype)


def _hgrn_sample_call(sq, g, kf, iv, s0, og, n, t, n_heads):
    width = n_heads * HGRN_HEAD
    bn = min(HGRN_SAMPLE_SEQS, n)
    assert n % bn == 0 and t % SUBLANES == 0
    tri, masks, signs = _hgrn_tables(t)
    tok = pl.BlockSpec((bn, t, width), lambda i: (i, 0, 0))
    st = pl.BlockSpec((bn, n_heads, HGRN_HEAD, HGRN_HEAD), lambda i: (i, 0, 0, 0))
    o, s = pl.pallas_call(
        functools.partial(_hgrn_sample_kernel, c=t, n_heads=n_heads),
        grid=(n // bn,),
        in_specs=[tok, tok, tok, tok, st, _resident(tri.shape), _resident(masks.shape), _resident(signs.shape),
                  _resident((1, HGRN_HEAD))],
        out_specs=[tok, st],
        out_shape=[jax.ShapeDtypeStruct((n, t, width), BF16),
                   jax.ShapeDtypeStruct((n, n_heads, HGRN_HEAD, HGRN_HEAD), F32)],
        compiler_params=pltpu.CompilerParams(dimension_semantics=("arbitrary",), vmem_limit_bytes=V7X_VMEM_LIMIT),
        name="hgrn_sample",
    )(sq.reshape(n, t, width), g.reshape(n, t, width), kf.reshape(n, t, width), iv.reshape(n, t, width),
      s0, tri, masks, signs, og.reshape(1, HGRN_HEAD))
    return o.reshape(n * t, width), s


def _load_tokens(ref, scr):
    bs, dil, rows, width = ref.shape
    if dil == 1:
        return ref[...].reshape(bs * rows, width).astype(F32)
    for r in range(dil):
        for c in range(width // LANES):
            scr[c, pl.ds(r, rows, stride=dil), :] = ref[0, r, :, c * LANES:(c + 1) * LANES].astype(F32)
    return jnp.concatenate([scr[c] for c in range(width // LANES)], axis=1)


def _out_kernel(*refs, n_att):
    att = refs[:2 * n_att] if n_att > 1 else refs[:1]
    n_in = len(att)
    sza_ref, ob_ref, szb_ref, sga_ref, sgb_ref, x_ref, gate_ref, ex_ref, wa_ref, wb_ref, wo_ref, y_ref, scr = refs[n_in:]
    if n_att > 1:
        ls_ = [_load_tokens(r, scr) for r in att[n_att:]]
        m = functools.reduce(jnp.maximum, ls_)
        es = [jnp.exp2(l - m) for l in ls_]
        inv = 1.0 / functools.reduce(lambda a, b: a + b, es)
        alphas = [_dot((e * inv).astype(BF16), ex_ref[...]) for e in es]
        o_a = functools.reduce(lambda a, b: a + b, [al * _load_tokens(r, scr) for al, r in zip(alphas, att[:n_att])])
    else:
        o_a = _load_tokens(att[0], scr)
    branch_a = _dot((o_a * sza_ref[...].astype(F32)).astype(BF16), wa_ref[...])
    branch_b = _dot((ob_ref[...].astype(F32) * szb_ref[...].astype(F32)).astype(BF16), wb_ref[...])
    merged = sga_ref[...].astype(F32) * branch_a + sgb_ref[...].astype(F32) * branch_b
    upd = _dot(merged.astype(BF16), wo_ref[...])
    x = x_ref[...]
    y_ref[...] = x + gate_ref[...] * upd.reshape(x.shape)


def _out_call(att_inputs, sza, ob, szb, sga, sgb, x, gate, wa_bf, wb_bf, wo_bf):
    s, t, d = x.shape
    bs, bt = _token_tiling(s, t, OUT_TOKENS)
    tm = bs * bt
    n_t = t // bt

    def tok_spec(a):
        return pl.BlockSpec((tm, a.shape[1]), lambda i, j: (i * n_t + j, 0))

    def att_spec(a):
        dil = a.shape[1]
        assert dil == 1 or bs == 1
        return pl.BlockSpec((bs, dil, bt // dil, a.shape[3]), lambda i, j: (i, 0, j, 0))

    n_att = (len(att_inputs) + 1) // 2
    toks = [sza, ob, szb, sga, sgb]
    width = att_inputs[0].shape[3]
    lane_head = np.arange(LANES)[:, None] == (np.arange(width) // HEAD_DIM)[None, :]
    expand = jnp.asarray(lane_head, BF16)
    return pl.pallas_call(
        functools.partial(_out_kernel, n_att=n_att),
        grid=(s // bs, n_t),
        in_specs=[att_spec(a) for a in att_inputs] + [tok_spec(a) for a in toks]
        + [pl.BlockSpec((bs, bt, d), lambda i, j: (i, j, 0)),
           pl.BlockSpec((bs, 1, d), lambda i, j: (i, 0, 0)),
           _resident(expand.shape), _resident(wa_bf.shape), _resident(wb_bf.shape), _resident(wo_bf.shape)],
        out_specs=pl.BlockSpec((bs, bt, d), lambda i, j: (i, j, 0)),
        out_shape=jax.ShapeDtypeStruct((s, t, d), F32),
        scratch_shapes=[pltpu.VMEM((width // LANES, tm, LANES), F32)],
        compiler_params=pltpu.CompilerParams(dimension_semantics=("arbitrary", "arbitrary"),
                                             vmem_limit_bytes=V7X_VMEM_LIMIT),
        name="out",
    )(*att_inputs, *toks, x, gate, expand, wa_bf, wb_bf, wo_bf)


def _kvnew_kernel(k_ref, v_ref, *out_refs):
    width = out_refs[0].shape[2]
    for g, o_ref in enumerate(out_refs):
        for kv, src in enumerate((k_ref, v_ref)):
            for t in range(src.shape[1]):
                o_ref[t, kv] = src[:, t, g * width:(g + 1) * width].T


def _kvnew_call(kn, vn, n_groups):
    n, t, cols = kn.shape
    return pl.pallas_call(
        _kvnew_kernel,
        out_shape=[jax.ShapeDtypeStruct((t, 2, cols // n_groups, n), F32) for _ in range(n_groups)],
        compiler_params=pltpu.CompilerParams(vmem_limit_bytes=V7X_VMEM_LIMIT),
        name="kvnew",
    )(kn, vn)


def kernel(x_prompt, x_sample, cache_kv_g0, cache_kv_g1, cache_kv_g2, state_hgrn, c_prompt, c_sample, norm_gain, w_ada, b_ada, w_in, q_norm_gain, k_norm_gain, hgrn_lb_logits, hgrn_out_norm_gain, w_branch_a, w_branch_b, w_out):
    depth = w_in.shape[0]
    assert depth == 1, "single-layer step"
    n_p, t_p, d = x_prompt.shape
    n_s, t_s, _ = x_sample.shape
    a_out = w_branch_a.shape[1]
    n_heads = a_out // HEAD_DIM
    b_f = w_branch_b.shape[1]
    b_heads = b_f // HGRN_HEAD
    slopes = _alibi_slopes(n_heads)

    n_c = n_p + n_s
    pad = (-n_c) % SUBLANES
    c_all = jnp.concatenate([c_prompt, c_sample, jnp.zeros((pad, d), F32)], axis=0)
    ada = _ada_call(c_all, w_ada[0], b_ada[0])
    shift, scale, gate = ada[:, :d], ada[:, d:2 * d], ada[:, 2 * d:]

    w_in_bf = w_in[0].astype(BF16)
    wa_bf, wb_bf, wo_bf = w_branch_a[0].astype(BF16), w_branch_b[0].astype(BF16), w_out[0].astype(BF16)
    qg = (jnp.tile(q_norm_gain[0], n_heads) * (ATTN_SCALE * LOG2E)).reshape(1, a_out)
    kg = jnp.tile(k_norm_gain[0], n_heads).reshape(1, a_out)
    og = hgrn_out_norm_gain[0]

    def cond(lo, n):
        return [a[lo:lo + n].reshape(n, 1, d) for a in (shift, scale, gate)]

    sh, sc, gt_p = cond(0, n_p)
    outs = _proj_call(x_prompt, sh, sc, norm_gain[0], w_in_bf, qg, kg, hgrn_lb_logits, n_groups=N_GROUPS,
                      dils=DILATIONS, a_out=a_out, b_f=b_f, kv_dtype=BF16,
                      windows=tuple(min(w, t_p) for w in WINDOWS))
    qs, ks, vs = outs[0:N_GROUPS], outs[N_GROUPS:2 * N_GROUPS], outs[2 * N_GROUPS:3 * N_GROUPS]
    act_p, kv_rows = outs[3 * N_GROUPS:3 * N_GROUPS + 8], outs[3 * N_GROUPS + 8:]
    sh, sc, gt_s = cond(n_p, n_s)
    outs = _proj_call(x_sample, sh, sc, norm_gain[0], w_in_bf, qg, kg, hgrn_lb_logits, n_groups=N_GROUPS,
                      dils=(1,) * N_GROUPS, a_out=a_out, b_f=b_f, kv_dtype=F32)
    qn, kn, vn = outs[:3]
    act_s = outs[3:3 + 8]

    steps = [_attn_steps(qs[gi])[2] for gi in range(N_GROUPS)]
    assert n_s % sum(steps) == 0, "sample sequences must spread evenly over the prompt attention grid steps"
    n_seq = n_s // sum(steps)
    sample = (qn, kn, vn, (cache_kv_g0[0], cache_kv_g1[0], cache_kv_g2[0]))
    sza, sqb, g, kf, ib, szb, sga, sgb = act_p
    carrier = int(np.argmax(steps))
    rec = tuple(a.reshape(n_p, t_p, b_f) for a in (sqb, g, kf, ib)) + (og, b_heads)
    att = [_attn_call(qs[gi], ks[gi], vs[gi], gi, n_heads, slopes, sample, sum(steps[:gi]), n_seq,
                      rec if gi == carrier else None) for gi in range(N_GROUPS)]
    o_a = jnp.concatenate([a[2] for a in att], axis=0)
    ob, hgrn_p = att[carrier][3].reshape(n_p * t_p, b_f), att[carrier][4]

    y_p = _out_call([a[0] for a in att] + [a[1] for a in att], sza, ob, szb, sga, sgb, x_prompt, gt_p,
                    wa_bf, wb_bf, wo_bf)
    kv_p = [jnp.transpose(a.reshape(n_p, 2, n_heads, HEAD_DIM, a.shape[3]), (0, 4, 1, 2, 3))[None]
            for a in kv_rows]

    sza, sqb, g, kf, ib, szb, sga, sgb = act_s
    ob, hgrn_s = _hgrn_sample_call(sqb, g, kf, ib, state_hgrn[0], og, n_s, t_s, b_heads)
    y_s = _out_call([o_a.reshape(n_s, 1, t_s, a_out)], sza, ob, szb, sga, sgb, x_sample, gt_s, wa_bf, wb_bf, wo_bf)
    kv_s = [jnp.transpose(a.reshape(t_s, 2, n_heads, HEAD_DIM, n_s), (4, 0, 1, 2, 3))[None]
            for a in _kvnew_call(kn, vn, N_GROUPS)]

    return (y_p, y_s, kv_p[0], kv_p[1], kv_p[2], hgrn_p[None], kv_s[0], kv_s[1], kv_s[2], hgrn_s[None])
```
